```python
import jax, jax.numpy as jnp
from jax import lax
import numpy as np

D_MODEL = 1024
BATCH = 8
SEQ = 8192
DEPTH = 1
DEC_BATCH = 128
DEC_SEQ = 4
PAST_LEN = 8192
PAGE_SIZE = 128

HEAD_DIM = 64
N_HEADS = D_MODEL // HEAD_DIM
CONV_HEADS = N_HEADS // 4
CONV_DIM = CONV_HEADS * HEAD_DIM
ATTN_HEADS = N_HEADS - CONV_HEADS
ATTN_DIM = ATTN_HEADS * HEAD_DIM
DIL_PAIRS = ((128, 1), (512, 4), (2048, 16))
N_DIL = len(DIL_PAIRS)
GROUP_HEADS = ATTN_HEADS // N_DIL
CONV_WIDTH = 3
N_BUCKETS = 32
MAX_DISTANCE = 2048
D_FF = ((8 * D_MODEL + 3 * 256 - 1) // (3 * 256)) * 256
PROJ_DIM = 3 * CONV_DIM + 3 * ATTN_DIM
EPS = 1e-6
NEG_INF = -1e30
ATTN_SCALE = HEAD_DIM ** -0.5

kernel_name = "hymba_conv_dilated_swa_adaln_step"


def _rmsnorm(x, g):
    xf = x.astype(jnp.float32)
    y = xf * lax.rsqrt(jnp.mean(xf * xf, axis=-1, keepdims=True) + EPS)
    return (y * g.astype(jnp.float32)).astype(x.dtype)


def _t5_buckets(dist):
    dist = np.asarray(dist, np.int32)
    max_exact = N_BUCKETS // 2
    large = max_exact + (np.log(np.maximum(dist, 1).astype(np.float32) / max_exact)
                         / np.log(MAX_DISTANCE / max_exact) * (N_BUCKETS - max_exact)).astype(np.int32)
    large = np.minimum(large, N_BUCKETS - 1)
    return np.where(dist < max_exact, dist, large).astype(np.int32)


def _group_bias(rel_bias, g):
    w, d = DIL_PAIRS[g]
    nk = w // d + 1
    buckets = _t5_buckets(np.arange(nk) * d)
    hs = slice(g * GROUP_HEADS, (g + 1) * GROUP_HEADS)
    return rel_bias[buckets][:, hs].T.astype(jnp.float32)


def _dilated_prompt(q, k, v, bias, d):
    Bn, S, H, Dh = q.shape
    qb = bias.shape[1] - 1
    M = S // d
    nb = -(-M // qb)
    Mp = nb * qb

    def to_res(t):
        t = t.reshape(Bn, M, d, H, Dh).transpose(0, 2, 1, 3, 4)
        return jnp.pad(t, ((0, 0), (0, 0), (0, Mp - M), (0, 0), (0, 0)))

    qr = to_res(q).reshape(Bn, d, nb, qb, H, Dh)
    pad_front = ((0, 0), (0, 0), (qb, 0), (0, 0), (0, 0))
    kp = jnp.pad(to_res(k), pad_front)
    vp = jnp.pad(to_res(v), pad_front)

    def blocks(t):
        prev = t[:, :, :Mp].reshape(Bn, d, nb, qb, H, Dh)
        cur = t[:, :, qb:].reshape(Bn, d, nb, qb, H, Dh)
        return jnp.concatenate([prev, cur], axis=3)

    kb, vb = blocks(kp), blocks(vp)
    qq = np.arange(qb)[:, None]
    kk = np.arange(2 * qb)[None, :]
    steps = qb + qq - kk
    band = (steps >= 0) & (steps <= qb)
    first_ok = kk >= qb
    mask = band[None] & ((np.arange(nb)[:, None, None] > 0) | first_ok[None])
    bias_band = bias[:, np.clip(steps, 0, qb)]

    s = jnp.einsum('bdnqhe,bdnkhe->bdnhqk', qr, kb).astype(jnp.float32) * ATTN_SCALE
    s = s + bias_band[None, None, None]
    s = jnp.where(jnp.asarray(mask)[None, None, :, None], s, NEG_INF)
    m = jnp.max(s, axis=-1, keepdims=True)
    e = jnp.exp(s - m)
    den = jnp.sum(e, axis=-1, keepdims=True)
    lse = (m + jnp.log(den))[..., 0]
    o = jnp.einsum('bdnhqk,bdnkhe->bdnqhe', (e / den).astype(v.dtype), vb)
    o = o.reshape(Bn, d, Mp, H, Dh)[:, :, :M].transpose(0, 2, 1, 3, 4).reshape(Bn, S, H, Dh)
    lse = lse.transpose(0, 1, 2, 4, 3).reshape(Bn, d, Mp, H)[:, :, :M]
    lse = lse.transpose(0, 2, 1, 3).reshape(Bn, S, H)
    return o, lse


def _dilated_sample(q, k, v, kv_buf, bias, d):
    Bn, T, H, Dh = q.shape
    L = kv_buf.shape[1]
    nk = bias.shape[1]
    k_ext = jnp.concatenate([kv_buf[:, :, 0].astype(k.dtype), k], axis=1)
    v_ext = jnp.concatenate([kv_buf[:, :, 1].astype(v.dtype), v], axis=1)
    idx = L + np.arange(T)[:, None] - np.arange(nk)[None, :] * d
    valid = jnp.asarray(idx >= 0)
    idx_c = np.maximum(idx, 0)
    kg = k_ext[:, idx_c]
    vg = v_ext[:, idx_c]
    s = jnp.einsum('bthe,btjhe->bhtj', q, kg).astype(jnp.float32) * ATTN_SCALE
    s = s + bias[None, :, None, :]
    s = jnp.where(valid[None, None], s, NEG_INF)
    m = jnp.max(s, axis=-1, keepdims=True)
    e = jnp.exp(s - m)
    den = jnp.sum(e, axis=-1, keepdims=True)
    lse = (m + jnp.log(den))[..., 0]
    o = jnp.einsum('bhtj,btjhe->bthe', (e / den).astype(v.dtype), vg)
    new_buf = jnp.stack([k_ext[:, T:], v_ext[:, T:]], axis=2)
    return o, lse.transpose(0, 2, 1), new_buf


def _mixer(h, w_in, conv_w, gn_conv, gn_attn, w_out, rel_bias, conv_prev, kv_bufs):
    Bn, T, _ = h.shape
    proj = h @ w_in
    cuts = [CONV_DIM, 2 * CONV_DIM, 3 * CONV_DIM, 3 * CONV_DIM + ATTN_DIM, 3 * CONV_DIM + 2 * ATTN_DIM]
    gb, gc, xin, q, k, v = jnp.split(proj, cuts, axis=-1)

    u = gc * xin
    if conv_prev is None:
        conv_prev = jnp.zeros((Bn, CONV_WIDTH - 1, CONV_DIM), u.dtype)
    u_ext = jnp.concatenate([conv_prev.astype(u.dtype), u], axis=1)
    z = conv_w[0] * u_ext[:, 0:T] + conv_w[1] * u_ext[:, 1:T + 1] + conv_w[2] * u_ext[:, 2:T + 2]
    conv_out = gb * z
    new_conv = u_ext[:, T:]

    q = q.reshape(Bn, T, ATTN_HEADS, HEAD_DIM)
    k = k.reshape(Bn, T, ATTN_HEADS, HEAD_DIM)
    v = v.reshape(Bn, T, ATTN_HEADS, HEAD_DIM)
    outs, lses, new_kv = [], [], []
    for g in range(N_DIL):
        w, d = DIL_PAIRS[g]
        hs = slice(g * GROUP_HEADS, (g + 1) * GROUP_HEADS)
        bias = _group_bias(rel_bias, g)
        qg, kg, vg = q[:, :, hs], k[:, :, hs], v[:, :, hs]
        if kv_bufs is None:
            o, l = _dilated_prompt(qg, kg, vg, bias, d)
            L = min(w, T)
            nb_ = jnp.stack([kg[:, T - L:], vg[:, T - L:]], axis=2)
        else:
            o, l, nb_ = _dilated_sample(qg, kg, vg, kv_bufs[g], bias, d)
        outs.append(o)
        lses.append(l)
        new_kv.append(nb_)
    alpha = jax.nn.softmax(jnp.stack(lses, axis=0), axis=0)
    attn_out = jnp.concatenate(
        [(alpha[g][..., None] * outs[g].astype(jnp.float32)).astype(h.dtype) for g in range(N_DIL)],
        axis=2).reshape(Bn, T, ATTN_DIM)

    mixed = jnp.concatenate([_rmsnorm(conv_out, gn_conv), _rmsnorm(attn_out, gn_attn)], axis=-1)
    return mixed @ w_out, new_conv, new_kv


def _layer(x, c, w_ada, b_ada, norm1_g, norm2_g, w_in, conv_w, gn_conv, gn_attn, w_out,
           w_gate, w_up, w_down, rel_bias, conv_prev, kv_bufs):
    mod = jax.nn.silu(c) @ w_ada + b_ada
    sh1, sc1, ga1, sh2, sc2, ga2 = jnp.split(mod[:, None, :], 6, axis=-1)
    h = _rmsnorm(x, norm1_g) * (1 + sc1) + sh1
    mix, new_conv, new_kv = _mixer(h, w_in, conv_w, gn_conv, gn_attn, w_out, rel_bias, conv_prev, kv_bufs)
    x = x + ga1 * mix
    h2 = _rmsnorm(x, norm2_g) * (1 + sc2) + sh2
    ffn = (jax.nn.silu(h2 @ w_gate) * (h2 @ w_up)) @ w_down
    x = x + ga2 * ffn
    return x, new_conv, new_kv


def setup_inputs(seed: int = 0) -> dict:
    key = jax.random.key(seed)
    ks = jax.random.split(key, 24)
    f32 = jnp.float32
    nrm = lambda k, shape, s=1.0: (jax.random.normal(k, shape, f32) * s)
    buf_lens = [min(w, PAST_LEN) for (w, _) in DIL_PAIRS]
    return {
        "x_prompt": nrm(ks[0], (BATCH, SEQ, D_MODEL)),
        "x_sample": nrm(ks[1], (DEC_BATCH, DEC_SEQ, D_MODEL)),
        "state_conv": nrm(ks[2], (DEPTH, DEC_BATCH, CONV_WIDTH - 1, CONV_DIM)),
        "cache_kv1": nrm(ks[3], (DEPTH, DEC_BATCH, buf_lens[0], 2, GROUP_HEADS, HEAD_DIM)),
        "cache_kv2": nrm(ks[4], (DEPTH, DEC_BATCH, buf_lens[1], 2, GROUP_HEADS, HEAD_DIM)),
        "cache_kv3": nrm(ks[5], (DEPTH, DEC_BATCH, buf_lens[2], 2, GROUP_HEADS, HEAD_DIM)),
        "c_prompt": nrm(ks[6], (BATCH, D_MODEL)),
        "c_sample": nrm(ks[7], (DEC_BATCH, D_MODEL)),
        "w_ada": nrm(ks[8], (DEPTH, D_MODEL, 6 * D_MODEL), 0.5 * D_MODEL ** -0.5),
        "b_ada": nrm(ks[9], (DEPTH, 6 * D_MODEL), 0.01),
        "norm1_g": 1.0 + nrm(ks[10], (DEPTH, D_MODEL), 0.01),
        "norm2_g": 1.0 + nrm(ks[11], (DEPTH, D_MODEL), 0.01),
        "w_in": nrm(ks[12], (DEPTH, D_MODEL, PROJ_DIM), D_MODEL ** -0.5),
        "conv_w": nrm(ks[13], (DEPTH, CONV_WIDTH, CONV_DIM), CONV_WIDTH ** -0.5),
        "gn_conv": 1.0 + nrm(ks[14], (DEPTH, CONV_DIM), 0.01),
        "gn_attn": 1.0 + nrm(ks[15], (DEPTH, ATTN_DIM), 0.01),
        "w_out": nrm(ks[16], (DEPTH, D_MODEL, D_MODEL), D_MODEL ** -0.5),
        "w_gate": nrm(ks[17], (DEPTH, D_MODEL, D_FF), D_MODEL ** -0.5),
        "w_up": nrm(ks[18], (DEPTH, D_MODEL, D_FF), D_MODEL ** -0.5),
        "w_down": nrm(ks[19], (DEPTH, D_FF, D_MODEL), D_FF ** -0.5),
        "rel_bias": nrm(ks[20], (N_BUCKETS, ATTN_HEADS), 0.5),
        "final_g": 1.0 + nrm(ks[21], (D_MODEL,), 0.01),
    }


def reference(x_prompt, x_sample, state_conv, cache_kv1, cache_kv2, cache_kv3, c_prompt, c_sample,
              w_ada, b_ada, norm1_g, norm2_g, w_in, conv_w, gn_conv, gn_attn, w_out,
              w_gate, w_up, w_down, rel_bias, final_g):
    xp, xs = x_prompt, x_sample
    conv_p, kv1_p, kv2_p, kv3_p = [], [], [], []
    conv_s, kv1_s, kv2_s, kv3_s = [], [], [], []
    for l in range(DEPTH):
        xp, cp, kvp = _layer(xp, c_prompt, w_ada[l], b_ada[l], norm1_g[l], norm2_g[l], w_in[l], conv_w[l],
                             gn_conv[l], gn_attn[l], w_out[l], w_gate[l], w_up[l], w_down[l], rel_bias,
                             None, None)
        xs, cs, kvs = _layer(xs, c_sample, w_ada[l], b_ada[l], norm1_g[l], norm2_g[l], w_in[l], conv_w[l],
                             gn_conv[l], gn_attn[l], w_out[l], w_gate[l], w_up[l], w_down[l], rel_bias,
                             state_conv[l], (cache_kv1[l], cache_kv2[l], cache_kv3[l]))
        conv_p.append(cp); kv1_p.append(kvp[0]); kv2_p.append(kvp[1]); kv3_p.append(kvp[2])
        conv_s.append(cs); kv1_s.append(kvs[0]); kv2_s.append(kvs[1]); kv3_s.append(kvs[2])
    y_prompt = _rmsnorm(xp, final_g)
    y_sample = _rmsnorm(xs, final_g)
    return (y_prompt, y_sample,
            jnp.stack(conv_p), jnp.stack(kv1_p), jnp.stack(kv2_p), jnp.stack(kv3_p),
            jnp.stack(conv_s), jnp.stack(kv1_s), jnp.stack(kv2_s), jnp.stack(kv3_s))
```

```python
import functools

import numpy as np
import jax
import jax.numpy as jnp
from jax import lax
from jax.experimental import pallas as pl
from jax.experimental.pallas import tpu as pltpu

F32 = jnp.float32
BF16 = jnp.bfloat16

D_MODEL = 1024
HEAD_DIM = 64
CONV_DIM = 256
ATTN_DIM = 768
GROUP_HEADS = 4
GROUP_DIM = GROUP_HEADS * HEAD_DIM
DIL_PAIRS = ((128, 1), (512, 4), (2048, 16))
N_DIL = len(DIL_PAIRS)
QB = 128
N_BUCKETS = 32
MAX_DISTANCE = 2048
D_FF = 2816
PROJ_DIM = 3072
EPS = 1e-6
NEG_INF = -1e30
ATTN_SCALE = HEAD_DIM ** -0.5

LANES = 128
VMEM_LIMIT = 56 * 1024 * 1024


def _cparams(sem):
    return pltpu.CompilerParams(dimension_semantics=sem, vmem_limit_bytes=VMEM_LIMIT)


def _const_spec(shape):
    nd = len(shape)
    return pl.BlockSpec(shape, lambda *_: (0,) * nd, pipeline_mode=pl.Buffered(1))


def _head_masks(width, dtype):
    lane = lax.broadcasted_iota(jnp.int32, (1, width), 1)
    if width == GROUP_DIM:
        lane = lane // HEAD_DIM
    return [(lane == h).astype(dtype) for h in range(GROUP_HEADS)]


def _ada_kernel(c_ref, w_ref, b_ref, o_ref):
    c = c_ref[...]
    s = (c * jax.nn.sigmoid(c)).astype(BF16)
    o_ref[...] = jnp.dot(s, w_ref[...].astype(BF16), preferred_element_type=F32) + b_ref[...]


def _ada(c_all, w_ada, b_ada):
    rows = c_all.shape[0]
    tn = 1536
    return pl.pallas_call(
        _ada_kernel,
        grid=(6 * D_MODEL // tn,),
        in_specs=[pl.BlockSpec((rows, D_MODEL), lambda j: (0, 0)),
                  pl.BlockSpec((D_MODEL, tn), lambda j: (0, j)),
                  pl.BlockSpec((1, tn), lambda j: (0, j))],
        out_specs=pl.BlockSpec((rows, tn), lambda j: (0, j)),
        out_shape=jax.ShapeDtypeStruct((rows, 6 * D_MODEL), F32),
        compiler_params=_cparams(("arbitrary",)),
        name="ada",
    )(c_all, w_ada, b_ada)


def _rms(x, g):
    return x * lax.rsqrt(jnp.mean(x * x, axis=-1, keepdims=True) + EPS) * g


def _inproj_prompt_kernel(x_ref, sh_ref, sc_ref, g1_ref, w_ref, cw_ref, gnc_ref,
                          convn_ref, q_ref, k_ref, v_ref, nconv_ref, kv1_ref, kv2_ref, kv3_ref,
                          ubuf_ref, *, tm):
    i = pl.program_id(1)
    n = pl.num_programs(1)
    h = _rms(x_ref[0], g1_ref[...]) * (1.0 + sc_ref[0]) + sh_ref[0]
    hb = h.astype(BF16)

    pc = jnp.dot(hb, w_ref[:, 0:3 * CONV_DIM], preferred_element_type=F32)
    gb = pc[:, 0:CONV_DIM]
    u = pc[:, CONV_DIM:2 * CONV_DIM] * pc[:, 2 * CONV_DIM:3 * CONV_DIM]

    @pl.when(i == 0)
    def _():
        ubuf_ref[0:8, :] = jnp.zeros((8, CONV_DIM), F32)

    @pl.when(i > 0)
    def _():
        ubuf_ref[0:8, :] = ubuf_ref[tm:tm + 8, :]

    ubuf_ref[8:tm + 8, :] = u
    um1 = ubuf_ref[7:tm + 7, :]
    um2 = ubuf_ref[6:tm + 6, :]
    cw = cw_ref[...]
    z = cw[0:1, :] * um2 + cw[1:2, :] * um1 + cw[2:3, :] * u
    convn_ref[0] = _rms(gb * z, gnc_ref[...]).astype(BF16)

    @pl.when(i == n - 1)
    def _():
        nconv_ref[0] = ubuf_ref[tm + 6:tm + 8, :]

    c0 = 3 * CONV_DIM
    pq = jnp.dot(hb, w_ref[:, c0:c0 + ATTN_DIM], preferred_element_type=F32)
    q_ref[0] = (pq * ATTN_SCALE).astype(BF16)
    pk = jnp.dot(hb, w_ref[:, c0 + ATTN_DIM:c0 + 2 * ATTN_DIM], preferred_element_type=F32)
    k_ref[0] = pk.astype(BF16)
    pv = jnp.dot(hb, w_ref[:, c0 + 2 * ATTN_DIM:c0 + 3 * ATTN_DIM], preferred_element_type=F32)
    v_ref[0] = pv.astype(BF16)

    def put(ref, g, rows):
        lo = g * GROUP_DIM
        ref[0, 0] = pk[tm - rows:tm, lo:lo + GROUP_DIM].T
        ref[0, 1] = pv[tm - rows:tm, lo:lo + GROUP_DIM].T

    w3 = DIL_PAIRS[2][0]

    @pl.when(i >= n - w3 // tm)
    def _():
        put(kv3_ref, 2, tm)

    @pl.when(i == n - 1)
    def _():
        put(kv2_ref, 1, DIL_PAIRS[1][0])
        put(kv1_ref, 0, DIL_PAIRS[0][0])


def _inproj_prompt(x, mod, g1, w_in, conv_w, gn_conv, tm=512):
    B, S, _ = x.shape
    n = S // tm
    w1, w2, w3 = (w for w, _ in DIL_PAIRS)
    assert S % tm == 0 and w3 % tm == 0 and tm >= w2
    n3 = w3 // tm
    row = lambda b, i: (b, i, 0)
    outs = pl.pallas_call(
        functools.partial(_inproj_prompt_kernel, tm=tm),
        grid=(B, n),
        in_specs=[pl.BlockSpec((1, tm, D_MODEL), row),
                  pl.BlockSpec((1, 1, D_MODEL), lambda b, i: (b, 0, 0)),
                  pl.BlockSpec((1, 1, D_MODEL), lambda b, i: (b, 0, 1)),
                  _const_spec((1, D_MODEL)),
                  _const_spec((D_MODEL, PROJ_DIM)),
                  _const_spec((3, CONV_DIM)),
                  _const_spec((1, CONV_DIM))],
        out_specs=[pl.BlockSpec((1, tm, CONV_DIM), row),
                   pl.BlockSpec((1, tm, ATTN_DIM), row),
                   pl.BlockSpec((1, tm, ATTN_DIM), row),
                   pl.BlockSpec((1, tm, ATTN_DIM), row),
                   pl.BlockSpec((1, 2, CONV_DIM), lambda b, i: (b, 0, 0)),
                   pl.BlockSpec((1, 2, GROUP_DIM, w1), lambda b, i: (b, 0, 0, 0)),
                   pl.BlockSpec((1, 2, GROUP_DIM, w2), lambda b, i: (b, 0, 0, 0)),
                   pl.BlockSpec((1, 2, GROUP_DIM, tm),
                                lambda b, i: (b, 0, 0, jnp.maximum(i - (n - n3), 0)))],
        out_shape=[jax.ShapeDtypeStruct((B, S, CONV_DIM), BF16),
                   jax.ShapeDtypeStruct((B, S, ATTN_DIM), BF16),
                   jax.ShapeDtypeStruct((B, S, ATTN_DIM), BF16),
                   jax.ShapeDtypeStruct((B, S, ATTN_DIM), BF16),
                   jax.ShapeDtypeStruct((B, 2, CONV_DIM), F32),
                   jax.ShapeDtypeStruct((B, 2, GROUP_DIM, w1), F32),
                   jax.ShapeDtypeStruct((B, 2, GROUP_DIM, w2), F32),
                   jax.ShapeDtypeStruct((B, 2, GROUP_DIM, w3), F32)],
        scratch_shapes=[pltpu.VMEM((tm + 8, CONV_DIM), F32)],
        compiler_params=_cparams(("arbitrary", "arbitrary")),
        name="inproj_prompt",
    )(x, mod, mod, g1, w_in, conv_w, gn_conv)
    return outs


def _inproj_sample_kernel(x_ref, sh_ref, sc_ref, g1_ref, w_ref, cw_ref, gnc_ref, p0_ref, p1_ref,
                          convn_ref, u_ref, q_ref, k_ref, v_ref, kt_ref, vt_ref, ubuf_ref, *, tm, t_len):
    h = _rms(x_ref[...], g1_ref[...]) * (1.0 + sc_ref[...]) + sh_ref[...]
    hb = h.astype(BF16)

    pc = jnp.dot(hb, w_ref[:, 0:3 * CONV_DIM], preferred_element_type=F32)
    gb = pc[:, 0:CONV_DIM]
    u = pc[:, CONV_DIM:2 * CONV_DIM] * pc[:, 2 * CONV_DIM:3 * CONV_DIM]
    u_ref[...] = u

    ubuf_ref[0:8, :] = jnp.zeros((8, CONV_DIM), F32)
    ubuf_ref[8:tm + 8, :] = u
    t = lax.broadcasted_iota(jnp.int32, (tm, 1), 0) % t_len
    um1 = jnp.where(t == 0, p1_ref[...], ubuf_ref[7:tm + 7, :])
    um2 = jnp.where(t == 0, p0_ref[...], jnp.where(t == 1, p1_ref[...], ubuf_ref[6:tm + 6, :]))
    cw = cw_ref[...]
    z = cw[0:1, :] * um2 + cw[1:2, :] * um1 + cw[2:3, :] * u
    convn_ref[...] = _rms(gb * z, gnc_ref[...]).astype(BF16)

    c0 = 3 * CONV_DIM
    pq = jnp.dot(hb, w_ref[:, c0:c0 + ATTN_DIM], preferred_element_type=F32)
    q_ref[...] = pq * ATTN_SCALE
    pk = jnp.dot(hb, w_ref[:, c0 + ATTN_DIM:c0 + 2 * ATTN_DIM], preferred_element_type=F32)
    k_ref[...] = pk
    kt_ref[...] = pk.T
    pv = jnp.dot(hb, w_ref[:, c0 + 2 * ATTN_DIM:c0 + 3 * ATTN_DIM], preferred_element_type=F32)
    v_ref[...] = pv
    vt_ref[...] = pv.T


def _inproj_sample(x, mod, g1, w_in, conv_w, gn_conv, prev0, prev1, t_len):
    tm = x.shape[0]
    full = lambda shape: pl.BlockSpec(shape, lambda i: (0,) * len(shape))
    return pl.pallas_call(
        functools.partial(_inproj_sample_kernel, tm=tm, t_len=t_len),
        grid=(1,),
        in_specs=[full((tm, D_MODEL)),
                  pl.BlockSpec((tm, D_MODEL), lambda i: (0, 0)),
                  pl.BlockSpec((tm, D_MODEL), lambda i: (0, 1)),
                  full((1, D_MODEL)), full((D_MODEL, PROJ_DIM)), full((3, CONV_DIM)), full((1, CONV_DIM)),
                  full((tm, CONV_DIM)), full((tm, CONV_DIM))],
        out_specs=[full((tm, CONV_DIM)), full((tm, CONV_DIM)),
                   full((tm, ATTN_DIM)), full((tm, ATTN_DIM)), full((tm, ATTN_DIM)),
                   full((ATTN_DIM, tm)), full((ATTN_DIM, tm))],
        out_shape=[jax.ShapeDtypeStruct((tm, CONV_DIM), BF16),
                   jax.ShapeDtypeStruct((tm, CONV_DIM), F32),
                   jax.ShapeDtypeStruct((tm, ATTN_DIM), F32),
                   jax.ShapeDtypeStruct((tm, ATTN_DIM), F32),
                   jax.ShapeDtypeStruct((tm, ATTN_DIM), F32),
                   jax.ShapeDtypeStruct((ATTN_DIM, tm), F32),
                   jax.ShapeDtypeStruct((ATTN_DIM, tm), F32)],
        scratch_shapes=[pltpu.VMEM((tm + 8, CONV_DIM), F32)],
        compiler_params=_cparams(("arbitrary",)),
        name="inproj_sample",
    )(x, mod, mod, g1, w_in, conv_w, gn_conv, prev0, prev1)


def _t5_buckets(dist):
    dist = np.asarray(dist, np.int32)
    max_exact = N_BUCKETS // 2
    large = max_exact + (np.log(np.maximum(dist, 1).astype(np.float32) / max_exact)
                         / np.log(MAX_DISTANCE / max_exact) * (N_BUCKETS - max_exact)).astype(np.int32)
    large = np.minimum(large, N_BUCKETS - 1)
    return np.where(dist < max_exact, dist, large).astype(np.int32)


def _group_table(rel_bias, g):
    w, d = DIL_PAIRS[g]
    buckets = _t5_buckets(np.arange(w // d + 1) * d)
    return rel_bias[buckets][:, g * GROUP_HEADS:(g + 1) * GROUP_HEADS].T.astype(F32)


def _prompt_bias(table):
    qq = np.arange(QB)[:, None]
    kk = np.arange(2 * QB)[None, :]
    steps = QB + qq - kk
    band = (steps >= 0) & (steps <= QB)
    bb = table[:, np.clip(steps, 0, QB)]
    full = jnp.where(band[None], bb, NEG_INF)
    first = jnp.where((band & (kk >= QB))[None], bb, NEG_INF)
    return jnp.stack([full, first]).reshape(2, GROUP_HEADS * QB, 2 * QB)


def _sample_bias(table, buf_len, d, t_len):
    nk = table.shape[1]
    jc = -np.ones((8, buf_len), np.int32)
    jn = -np.ones((8, LANES), np.int32)
    for t in range(t_len):
        for j in range(nk):
            idx = buf_len + t - j * d
            if idx < 0:
                continue
            if idx < buf_len:
                jc[t, idx] = j
            else:
                jn[t, idx - buf_len] = j
    jc[t_len:] = jc[0]
    jn[t_len:] = jn[0]
    bc = jnp.where((jc >= 0)[None], table[:, np.maximum(jc, 0)], NEG_INF)
    bn = jnp.where((jn >= 0)[None], table[:, np.maximum(jn, 0)], NEG_INF)
    return bc.reshape(GROUP_HEADS * 8, buf_len), bn.reshape(GROUP_HEADS * 8, LANES)


def _attn_prompt_kernel(q_ref, kc_ref, vc_ref, kp_ref, vp_ref, bias_ref, o_ref, lse_ref,
                        kcat_ref, vcat_ref, *, tq):
    i = pl.program_id(1)
    kcat_ref[0:QB, :] = kp_ref[0]
    kcat_ref[QB:QB + tq, :] = kc_ref[0]
    vcat_ref[0:QB, :] = vp_ref[0]
    vcat_ref[QB:QB + tq, :] = vc_ref[0]
    hm_b = _head_masks(GROUP_DIM, BF16)
    hm_f = _head_masks(GROUP_DIM, F32)
    lm_f = _head_masks(LANES, F32)
    first = jnp.where(i == 0, 1, 0)
    for j in range(tq // QB):
        q = q_ref[0, j * QB:(j + 1) * QB, :]
        q4 = jnp.concatenate([q * hm_b[h] for h in range(GROUP_HEADS)], axis=0)
        kk = kcat_ref[j * QB:(j + 2) * QB, :]
        vv = vcat_ref[j * QB:(j + 2) * QB, :]
        s = lax.dot_general(q4, kk, (((1,), (1,)), ((), ())), preferred_element_type=F32)
        s = s + (bias_ref[first] if j == 0 else bias_ref[0])
        m = jnp.max(s, axis=-1, keepdims=True)
        e = jnp.exp(s - m)
        den = jnp.sum(e, axis=-1, keepdims=True)
        lse = m + jnp.log(den)
        pv = jnp.dot(e.astype(BF16), vv, preferred_element_type=F32)
        pv = pv / den
        o = jnp.zeros((QB, GROUP_DIM), F32)
        l = jnp.zeros((QB, LANES), F32)
        for h in range(GROUP_HEADS):
            o = o + pv[h * QB:(h + 1) * QB, :] * hm_f[h]
            l = l + lse[h * QB:(h + 1) * QB, :] * lm_f[h]
        o_ref[0, j * QB:(j + 1) * QB, :] = o
        lse_ref[0, j * QB:(j + 1) * QB, :] = l


def _attn_prompt(q, k, v, bias, lane_block, tq=512):
    ns, M, _ = q.shape
    tq = min(tq, M)
    r = tq // QB
    cur = pl.BlockSpec((1, tq, GROUP_DIM), lambda s, i: (s, i, lane_block))
    prev = pl.BlockSpec((1, QB, GROUP_DIM), lambda s, i: (s, jnp.maximum(i * r - 1, 0), lane_block))
    return pl.pallas_call(
        functools.partial(_attn_prompt_kernel, tq=tq),
        grid=(ns, M // tq),
        in_specs=[cur, cur, cur, prev, prev, _const_spec(bias.shape)],
        out_specs=[pl.BlockSpec((1, tq, GROUP_DIM), lambda s, i: (s, i, 0)),
                   pl.BlockSpec((1, tq, LANES), lambda s, i: (s, i, 0))],
        out_shape=[jax.ShapeDtypeStruct((ns, M, GROUP_DIM), F32),
                   jax.ShapeDtypeStruct((ns, M, LANES), F32)],
        scratch_shapes=[pltpu.VMEM((tq + QB, GROUP_DIM), BF16), pltpu.VMEM((tq + QB, GROUP_DIM), BF16)],
        compiler_params=_cparams(("arbitrary", "arbitrary")),
        name="attn_prompt",
    )(q, k, v, k, v, bias)


def _attn_sample_kernel(q_ref, k8_ref, v8_ref, kt_ref, vt_ref, cache_ref, bc_ref, bn_ref,
                        o_ref, lse_ref, newc_ref, *, buf_len, t_len):
    b = pl.program_id(0)
    hm_f = _head_masks(GROUP_DIM, F32)
    lm_f = _head_masks(LANES, F32)
    lane = lax.broadcasted_iota(jnp.int32, (1, LANES), 1)

    q8 = q_ref[0]
    q4b = jnp.concatenate([q8 * hm_f[h] for h in range(GROUP_HEADS)], axis=0).astype(BF16)
    kt = cache_ref[0, 0]
    vt = cache_ref[0, 1]
    s_c = jnp.dot(q4b, kt.astype(BF16), preferred_element_type=F32) + bc_ref[...]

    q4r = q4b.astype(F32)
    k8r = k8_ref[0].astype(BF16).astype(F32)
    v8r = v8_ref[0].astype(BF16).astype(F32)
    s_n = bn_ref[...]
    for t in range(t_len):
        col = jnp.sum(q4r * k8r[t:t + 1, :], axis=-1, keepdims=True)
        s_n = s_n + jnp.where(lane == t, col, 0.0)

    m = jnp.maximum(jnp.max(s_c, axis=-1, keepdims=True), jnp.max(s_n, axis=-1, keepdims=True))
    e_c = jnp.exp(s_c - m)
    e_n = jnp.exp(s_n - m)
    den = jnp.sum(e_c, axis=-1, keepdims=True) + jnp.sum(e_n, axis=-1, keepdims=True)
    pv = lax.dot_general(e_c.astype(BF16), vt.astype(BF16), (((1,), (1,)), ((), ())),
                         preferred_element_type=F32)
    e_nr = e_n.astype(BF16).astype(F32)
    for t in range(t_len):
        w_t = jnp.sum(jnp.where(lane == t, e_nr, 0.0), axis=-1, keepdims=True)
        pv = pv + w_t * v8r[t:t + 1, :]
    pv = pv / den
    lse = m + jnp.log(den)
    o = jnp.zeros((8, GROUP_DIM), F32)
    l = jnp.zeros((8, LANES), F32)
    for h in range(GROUP_HEADS):
        o = o + pv[h * 8:(h + 1) * 8, :] * hm_f[h]
        l = l + lse[h * 8:(h + 1) * 8, :] * lm_f[h]
    o_ref[0] = o
    lse_ref[0] = l

    keep = LANES - t_len
    per_block = LANES // t_len
    shift_new = keep - (b % per_block) * t_len
    for kv, new_ref in ((0, kt_ref), (1, vt_ref)):
        new_r = pltpu.roll(new_ref[...], shift_new, 1)
        prev_r = None
        for c in range(buf_len // LANES):
            r_c = pltpu.roll(cache_ref[0, kv, :, c * LANES:(c + 1) * LANES], keep, 1)
            if c > 0:
                newc_ref[0, kv, :, (c - 1) * LANES:c * LANES] = jnp.where(lane < keep, prev_r, r_c)
            prev_r = r_c
        newc_ref[0, kv, :, buf_len - LANES:buf_len] = jnp.where(lane < keep, prev_r, new_r)


def _attn_sample(q8, k8, v8, kt, vt, cache, bc, bn, g, t_len):
    nb, _, _, buf_len = cache.shape
    per_block = LANES // t_len
    new_spec = pl.BlockSpec((1, 8, GROUP_DIM), lambda b: (b, 0, g))
    newt_spec = pl.BlockSpec((GROUP_DIM, LANES), lambda b: (g, b // per_block))
    cache_spec = pl.BlockSpec((1, 2, GROUP_DIM, buf_len), lambda b: (b, 0, 0, 0))
    return pl.pallas_call(
        functools.partial(_attn_sample_kernel, buf_len=buf_len, t_len=t_len),
        grid=(nb,),
        in_specs=[new_spec, new_spec, new_spec, newt_spec, newt_spec, cache_spec,
                  _const_spec(bc.shape), _const_spec(bn.shape)],
        out_specs=[pl.BlockSpec((1, 8, GROUP_DIM), lambda b: (b, 0, 0)),
                   pl.BlockSpec((1, 8, LANES), lambda b: (b, 0, 0)),
                   cache_spec],
        out_shape=[jax.ShapeDtypeStruct((nb, 8, GROUP_DIM), F32),
                   jax.ShapeDtypeStruct((nb, 8, LANES), F32),
                   jax.ShapeDtypeStruct(cache.shape, F32)],
        compiler_params=_cparams(("arbitrary",)),
        name="attn_sample",
    )(q8, k8, v8, kt, vt, cache, bc, bn)


def _out_kernel(x_ref, ga1_ref, sh2_ref, sc2_ref, ga2_ref, convn_ref,
                o1_ref, o2_ref, o3_ref, l1_ref, l2_ref, l3_ref,
                gna_ref, wout_ref, g2_ref, wg_ref, wu_ref, wd_ref, fg_ref, y_ref, *, ff_chunk):
    x = x_ref[0]
    hm_f = _head_masks(GROUP_DIM, F32)
    lm_f = _head_masks(LANES, F32)

    ls = [l1_ref[0], l2_ref[0], l3_ref[0]]
    m = jnp.maximum(jnp.maximum(ls[0], ls[1]), ls[2])
    es = [jnp.exp(l - m) for l in ls]
    tot = es[0] + es[1] + es[2]
    parts = []
    ssq = jnp.zeros((x.shape[0], 1), F32)
    for g, o_ref in enumerate((o1_ref, o2_ref, o3_ref)):
        alpha = es[g] / tot
        wide = jnp.zeros((x.shape[0], GROUP_DIM), F32)
        for h in range(GROUP_HEADS):
            a_h = jnp.sum(alpha * lm_f[h], axis=-1, keepdims=True)
            wide = wide + a_h * hm_f[h]
        ao = wide * o_ref[0]
        ssq = ssq + jnp.sum(ao * ao, axis=-1, keepdims=True)
        parts.append(ao)
    rinv = lax.rsqrt(ssq / ATTN_DIM + EPS)
    gna = gna_ref[...]
    mixed = jnp.concatenate(
        [convn_ref[0]] + [(parts[g] * rinv * gna[:, g * GROUP_DIM:(g + 1) * GROUP_DIM]).astype(BF16)
                          for g in range(N_DIL)], axis=-1)
    mix = jnp.dot(mixed, wout_ref[...], preferred_element_type=F32)
    x1 = x + ga1_ref[0] * mix

    h2 = (_rms(x1, g2_ref[...]) * (1.0 + sc2_ref[0]) + sh2_ref[0]).astype(BF16)
    ffn = jnp.zeros_like(x1)
    for c in range(D_FF // ff_chunk):
        sl = slice(c * ff_chunk, (c + 1) * ff_chunk)
        gate = jnp.dot(h2, wg_ref[:, sl], preferred_element_type=F32)
        up = jnp.dot(h2, wu_ref[:, sl], preferred_element_type=F32)
        act = (gate * jax.nn.sigmoid(gate) * up).astype(BF16)
        ffn = ffn + jnp.dot(act, wd_ref[sl, :], preferred_element_type=F32)
    x2 = x1 + ga2_ref[0] * ffn
    y_ref[0] = _rms(x2, fg_ref[...])


def _out(x, mod, convn, os_, ls_, gn_attn, w_out, g2, w_gate, w_up, w_down, final_g, tm=512, ff_chunk=1408):
    nb, R, _ = x.shape
    tm = min(tm, R)
    per_row = mod.shape[1] != 1
    mrows = tm if per_row else 1
    row = lambda b, i: (b, i, 0)

    def mod_spec(c):
        return pl.BlockSpec((1, mrows, D_MODEL), lambda b, i: (b, i if per_row else 0, c))

    return pl.pallas_call(
        functools.partial(_out_kernel, ff_chunk=ff_chunk),
        grid=(nb, R // tm),
        in_specs=[pl.BlockSpec((1, tm, D_MODEL), row),
                  mod_spec(2), mod_spec(3), mod_spec(4), mod_spec(5),
                  pl.BlockSpec((1, tm, CONV_DIM), row)]
                 + [pl.BlockSpec((1, tm, GROUP_DIM), row)] * N_DIL
                 + [pl.BlockSpec((1, tm, LANES), row)] * N_DIL
                 + [_const_spec((1, ATTN_DIM)), _const_spec((D_MODEL, D_MODEL)), _const_spec((1, D_MODEL)),
                    _const_spec((D_MODEL, D_FF)), _const_spec((D_MODEL, D_FF)), _const_spec((D_FF, D_MODEL)),
                    _const_spec((1, D_MODEL))],
        out_specs=pl.BlockSpec((1, tm, D_MODEL), row),
        out_shape=jax.ShapeDtypeStruct(x.shape, F32),
        compiler_params=_cparams(("arbitrary", "arbitrary")),
        name="out",
    )(x, mod, mod, mod, mod, convn, *os_, *ls_, gn_attn, w_out, g2, w_gate, w_up, w_down, final_g)


def _to_streams(a, d):
    B, S, C = a.shape
    return a.reshape(B, S // d, d, C).transpose(0, 2, 1, 3).reshape(B * d, S // d, C)


def _from_streams(a, d, B):
    _, M, C = a.shape
    return a.reshape(B, d, M, C).transpose(0, 2, 1, 3).reshape(B, M * d, C)


def _buffer_to_channel_major(cache):
    B, L = cache.shape[:2]
    return cache.transpose(0, 2, 3, 4, 1).reshape(B, 2, GROUP_DIM, L)


def _buffer_from_channel_major(buf):
    B, _, _, L = buf.shape
    return buf.reshape(B, 2, GROUP_HEADS, HEAD_DIM, L).transpose(0, 4, 1, 2, 3)[None]


def kernel(x_prompt, x_sample, state_conv, cache_kv1, cache_kv2, cache_kv3, c_prompt, c_sample, w_ada, b_ada, norm1_g, norm2_g, w_in, conv_w, gn_conv, gn_attn, w_out, w_gate, w_up, w_down, rel_bias, final_g):
    B, S, _ = x_prompt.shape
    nb, t_len, _ = x_sample.shape
    rows_s = nb * t_len
    assert w_ada.shape[0] == 1 and LANES % t_len == 0 and 2 <= t_len <= 8 and rows_s % LANES == 0

    w_in_b = w_in[0].astype(BF16)
    w_out_b = w_out[0].astype(BF16)
    w_gate_b = w_gate[0].astype(BF16)
    w_up_b = w_up[0].astype(BF16)
    w_down_b = w_down[0].astype(BF16)
    final_g2 = final_g.reshape(1, D_MODEL)
    tables = [_group_table(rel_bias, g) for g in range(N_DIL)]

    c_all = jnp.concatenate([c_prompt, jnp.repeat(c_sample, t_len, axis=0)], axis=0)
    mod = _ada(c_all, w_ada[0], b_ada)
    mod_p = mod[:B].reshape(B, 1, 6 * D_MODEL)
    mod_s = mod[B:]

    convn, q, k, v, nconv_p, kv1_p, kv2_p, kv3_p = _inproj_prompt(
        x_prompt, mod_p, norm1_g, w_in_b, conv_w[0], gn_conv)
    os_, ls_ = [], []
    for g, (_, d) in enumerate(DIL_PAIRS):
        bias = _prompt_bias(tables[g])
        if d == 1:
            o, l = _attn_prompt(q, k, v, bias, g)
        else:
            sl = slice(g * GROUP_DIM, (g + 1) * GROUP_DIM)
            o, l = _attn_prompt(_to_streams(q[:, :, sl], d), _to_streams(k[:, :, sl], d),
                                _to_streams(v[:, :, sl], d), bias, 0)
            o, l = _from_streams(o, d, B), _from_streams(l, d, B)
        os_.append(o)
        ls_.append(l)
    y_prompt = _out(x_prompt, mod_p, convn, os_, ls_, gn_attn, w_out_b, norm2_g,
                    w_gate_b, w_up_b, w_down_b, final_g2)

    xs = x_sample.reshape(rows_s, D_MODEL)
    prev0 = jnp.repeat(state_conv[0, :, 0], t_len, axis=0)
    prev1 = jnp.repeat(state_conv[0, :, 1], t_len, axis=0)
    convn_s, u_s, q_s, k_s, v_s, kt_s, vt_s = _inproj_sample(
        xs, mod_s, norm1_g, w_in_b, conv_w[0], gn_conv, prev0, prev1, t_len)
    pad8 = lambda a: jnp.pad(a.reshape(nb, t_len, ATTN_DIM), ((0, 0), (0, 8 - t_len), (0, 0)))
    q8, k8, v8 = pad8(q_s), pad8(k_s), pad8(v_s)
    os_s, ls_s, new_bufs = [], [], []
    for g, (cache, (_, d)) in enumerate(zip((cache_kv1, cache_kv2, cache_kv3), DIL_PAIRS)):
        buf = _buffer_to_channel_major(cache[0])
        bc, bn = _sample_bias(tables[g], buf.shape[-1], d, t_len)
        o, l, nbuf = _attn_sample(q8, k8, v8, kt_s, vt_s, buf, bc, bn, g, t_len)
        os_s.append(o[:, :t_len].reshape(1, rows_s, GROUP_DIM))
        ls_s.append(l[:, :t_len].reshape(1, rows_s, LANES))
        new_bufs.append(_buffer_from_channel_major(nbuf))
    y_sample = _out(xs[None], mod_s[None], convn_s[None], os_s, ls_s, gn_attn, w_out_b, norm2_g,
                    w_gate_b, w_up_b, w_down_b, final_g2)
    y_sample = y_sample.reshape(nb, t_len, D_MODEL)
    nconv_s = u_s.reshape(nb, t_len, CONV_DIM)[:, t_len - 2:]

    return (y_prompt, y_sample, nconv_p[None],
            _buffer_from_channel_major(kv1_p), _buffer_from_channel_major(kv2_p),
            _buffer_from_channel_major(kv3_p),
            nconv_s[None], new_bufs[0], new_bufs[1], new_bufs[2])
```

```python
import functools

import numpy as np
import jax
import jax.numpy as jnp
from jax import lax
from jax.experimental import pallas as pl
from jax.experimental.pallas import tpu as pltpu

F32 = jnp.float32
BF16 = jnp.bfloat16

D_MODEL = 1024
HEAD_DIM = 64
CONV_DIM = 256
ATTN_DIM = 768
GROUP_HEADS = 4
GROUP_DIM = GROUP_HEADS * HEAD_DIM
DIL_PAIRS = ((128, 1), (512, 4), (2048, 16))
N_DIL = len(DIL_PAIRS)
QB = 128
N_BUCKETS = 32
MAX_DISTANCE = 2048
D_FF = 2816
PROJ_DIM = 3072
EPS = 1e-6
NEG_INF = -1e30
ATTN_SCALE = HEAD_DIM ** -0.5

LANES = 128
VMEM_LIMIT = 56 * 1024 * 1024


def _cparams(sem):
    return pltpu.CompilerParams(dimension_semantics=sem, vmem_limit_bytes=VMEM_LIMIT)


def _const_spec(shape):
    nd = len(shape)
    return pl.BlockSpec(shape, lambda *_: (0,) * nd, pipeline_mode=pl.Buffered(1))


def _head_masks(width, dtype):
    lane = lax.broadcasted_iota(jnp.int32, (1, width), 1)
    if width == GROUP_DIM:
        lane = lane // HEAD_DIM
    return [(lane == h).astype(dtype) for h in range(GROUP_HEADS)]


def _ada_kernel(c_ref, w_ref, b_ref, o_ref):
    c = c_ref[...]
    s = (c * jax.nn.sigmoid(c)).astype(BF16)
    o_ref[...] = jnp.dot(s, w_ref[...].astype(BF16), preferred_element_type=F32) + b_ref[...]


def _ada(c_all, w_ada, b_ada):
    rows = c_all.shape[0]
    tn = 1536
    return pl.pallas_call(
        _ada_kernel,
        grid=(6 * D_MODEL // tn,),
        in_specs=[pl.BlockSpec((rows, D_MODEL), lambda j: (0, 0)),
                  pl.BlockSpec((D_MODEL, tn), lambda j: (0, j)),
                  pl.BlockSpec((1, tn), lambda j: (0, j))],
        out_specs=pl.BlockSpec((rows, tn), lambda j: (0, j)),
        out_shape=jax.ShapeDtypeStruct((rows, 6 * D_MODEL), F32),
        compiler_params=_cparams(("arbitrary",)),
        name="ada",
    )(c_all, w_ada, b_ada)


def _rms(x, g):
    return x * lax.rsqrt(jnp.mean(x * x, axis=-1, keepdims=True) + EPS) * g


def _store_streams(p, out_refs, slab_ref, tm):
    for g, ((_, d), out_ref) in enumerate(zip(DIL_PAIRS, out_refs)):
        lo = g * GROUP_DIM
        if d == 1:
            out_ref[0, 0] = p[:, lo:lo + GROUP_DIM].astype(BF16)
            continue
        for s in range(GROUP_DIM // LANES):
            slab_ref[s] = p[:, lo + s * LANES:lo + (s + 1) * LANES]
        for r in range(d):
            for s in range(GROUP_DIM // LANES):
                out_ref[0, r, :, s * LANES:(s + 1) * LANES] = (
                    slab_ref[s, pl.ds(r, tm // d, stride=d), :].astype(BF16))


def _inproj_prompt_kernel(x_ref, sh_ref, sc_ref, g1_ref, w_ref, cw_ref, gnc_ref,
                          convn_ref, q1_ref, q2_ref, q3_ref, k1_ref, k2_ref, k3_ref,
                          v1_ref, v2_ref, v3_ref, nconv_ref, kv1_ref, kv2_ref, kv3_ref,
                          ubuf_ref, slab_ref, *, tm):
    i = pl.program_id(1)
    n = pl.num_programs(1)
    h = _rms(x_ref[0], g1_ref[...]) * (1.0 + sc_ref[0]) + sh_ref[0]
    hb = h.astype(BF16)

    pc = jnp.dot(hb, w_ref[:, 0:3 * CONV_DIM], preferred_element_type=F32)
    gb = pc[:, 0:CONV_DIM]
    u = pc[:, CONV_DIM:2 * CONV_DIM] * pc[:, 2 * CONV_DIM:3 * CONV_DIM]

    @pl.when(i == 0)
    def _():
        ubuf_ref[0:8, :] = jnp.zeros((8, CONV_DIM), F32)

    @pl.when(i > 0)
    def _():
        ubuf_ref[0:8, :] = ubuf_ref[tm:tm + 8, :]

    ubuf_ref[8:tm + 8, :] = u
    um1 = ubuf_ref[7:tm + 7, :]
    um2 = ubuf_ref[6:tm + 6, :]
    cw = cw_ref[...]
    z = cw[0:1, :] * um2 + cw[1:2, :] * um1 + cw[2:3, :] * u
    convn_ref[0] = _rms(gb * z, gnc_ref[...]).astype(BF16)

    @pl.when(i == n - 1)
    def _():
        nconv_ref[0] = ubuf_ref[tm + 6:tm + 8, :]

    c0 = 3 * CONV_DIM
    pq = jnp.dot(hb, w_ref[:, c0:c0 + ATTN_DIM], preferred_element_type=F32)
    _store_streams(pq * ATTN_SCALE, (q1_ref, q2_ref, q3_ref), slab_ref, tm)
    pk = jnp.dot(hb, w_ref[:, c0 + ATTN_DIM:c0 + 2 * ATTN_DIM], preferred_element_type=F32)
    _store_streams(pk, (k1_ref, k2_ref, k3_ref), slab_ref, tm)
    pv = jnp.dot(hb, w_ref[:, c0 + 2 * ATTN_DIM:c0 + 3 * ATTN_DIM], preferred_element_type=F32)
    _store_streams(pv, (v1_ref, v2_ref, v3_ref), slab_ref, tm)

    def put(ref, g, rows):
        lo = g * GROUP_DIM
        ref[0, 0] = pk[tm - rows:tm, lo:lo + GROUP_DIM].T
        ref[0, 1] = pv[tm - rows:tm, lo:lo + GROUP_DIM].T

    w3 = DIL_PAIRS[2][0]

    @pl.when(i >= n - w3 // tm)
    def _():
        put(kv3_ref, 2, tm)

    @pl.when(i == n - 1)
    def _():
        put(kv2_ref, 1, DIL_PAIRS[1][0])
        put(kv1_ref, 0, DIL_PAIRS[0][0])


def _inproj_prompt(x, mod, g1, w_in, conv_w, gn_conv, tm=512):
    B, S, _ = x.shape
    n = S // tm
    w1, w2, w3 = (w for w, _ in DIL_PAIRS)
    assert S % tm == 0 and w3 % tm == 0 and tm >= w2
    n3 = w3 // tm
    row = lambda b, i: (b, i, 0)
    stream_specs = [pl.BlockSpec((1, d, tm // d, GROUP_DIM), lambda b, i: (b, 0, i, 0)) for _, d in DIL_PAIRS]
    stream_shapes = [jax.ShapeDtypeStruct((B, d, S // d, GROUP_DIM), BF16) for _, d in DIL_PAIRS]
    outs = pl.pallas_call(
        functools.partial(_inproj_prompt_kernel, tm=tm),
        grid=(B, n),
        in_specs=[pl.BlockSpec((1, tm, D_MODEL), row),
                  pl.BlockSpec((1, 1, D_MODEL), lambda b, i: (b, 0, 0)),
                  pl.BlockSpec((1, 1, D_MODEL), lambda b, i: (b, 0, 1)),
                  _const_spec((1, D_MODEL)),
                  _const_spec((D_MODEL, PROJ_DIM)),
                  _const_spec((3, CONV_DIM)),
                  _const_spec((1, CONV_DIM))],
        out_specs=[pl.BlockSpec((1, tm, CONV_DIM), row)] + stream_specs * 3 + [
                   pl.BlockSpec((1, 2, CONV_DIM), lambda b, i: (b, 0, 0)),
                   pl.BlockSpec((1, 2, GROUP_DIM, w1), lambda b, i: (b, 0, 0, 0)),
                   pl.BlockSpec((1, 2, GROUP_DIM, w2), lambda b, i: (b, 0, 0, 0)),
                   pl.BlockSpec((1, 2, GROUP_DIM, tm),
                                lambda b, i: (b, 0, 0, jnp.maximum(i - (n - n3), 0)))],
        out_shape=[jax.ShapeDtypeStruct((B, S, CONV_DIM), BF16)] + stream_shapes * 3 + [
                   jax.ShapeDtypeStruct((B, 2, CONV_DIM), F32),
                   jax.ShapeDtypeStruct((B, 2, GROUP_DIM, w1), F32),
                   jax.ShapeDtypeStruct((B, 2, GROUP_DIM, w2), F32),
                   jax.ShapeDtypeStruct((B, 2, GROUP_DIM, w3), F32)],
        scratch_shapes=[pltpu.VMEM((tm + 8, CONV_DIM), F32),
                        pltpu.VMEM((GROUP_DIM // LANES, tm, LANES), F32)],
        compiler_params=_cparams(("arbitrary", "arbitrary")),
        name="inproj_prompt",
    )(x, mod, mod, g1, w_in, conv_w, gn_conv)
    return outs


def _inproj_sample_kernel(x_ref, sh_ref, sc_ref, g1_ref, w_ref, cw_ref, gnc_ref, p0_ref, p1_ref,
                          convn_ref, u_ref, q_ref, k_ref, v_ref, kt_ref, vt_ref, ubuf_ref, *, tm, t_len):
    h = _rms(x_ref[...], g1_ref[...]) * (1.0 + sc_ref[...]) + sh_ref[...]
    hb = h.astype(BF16)

    pc = jnp.dot(hb, w_ref[:, 0:3 * CONV_DIM], preferred_element_type=F32)
    gb = pc[:, 0:CONV_DIM]
    u = pc[:, CONV_DIM:2 * CONV_DIM] * pc[:, 2 * CONV_DIM:3 * CONV_DIM]
    u_ref[...] = u

    ubuf_ref[0:8, :] = jnp.zeros((8, CONV_DIM), F32)
    ubuf_ref[8:tm + 8, :] = u
    t = lax.broadcasted_iota(jnp.int32, (tm, 1), 0) % t_len
    um1 = jnp.where(t == 0, p1_ref[...], ubuf_ref[7:tm + 7, :])
    um2 = jnp.where(t == 0, p0_ref[...], jnp.where(t == 1, p1_ref[...], ubuf_ref[6:tm + 6, :]))
    cw = cw_ref[...]
    z = cw[0:1, :] * um2 + cw[1:2, :] * um1 + cw[2:3, :] * u
    convn_ref[...] = _rms(gb * z, gnc_ref[...]).astype(BF16)

    c0 = 3 * CONV_DIM
    pq = jnp.dot(hb, w_ref[:, c0:c0 + ATTN_DIM], preferred_element_type=F32)
    q_ref[...] = pq * ATTN_SCALE
    pk = jnp.dot(hb, w_ref[:, c0 + ATTN_DIM:c0 + 2 * ATTN_DIM], preferred_element_type=F32)
    k_ref[...] = pk
    kt_ref[...] = pk.T
    pv = jnp.dot(hb, w_ref[:, c0 + 2 * ATTN_DIM:c0 + 3 * ATTN_DIM], preferred_element_type=F32)
    v_ref[...] = pv
    vt_ref[...] = pv.T


def _inproj_sample(x, mod, g1, w_in, conv_w, gn_conv, prev0, prev1, t_len):
    tm = x.shape[0]
    full = lambda shape: pl.BlockSpec(shape, lambda i: (0,) * len(shape))
    return pl.pallas_call(
        functools.partial(_inproj_sample_kernel, tm=tm, t_len=t_len),
        grid=(1,),
        in_specs=[full((tm, D_MODEL)),
                  pl.BlockSpec((tm, D_MODEL), lambda i: (0, 0)),
                  pl.BlockSpec((tm, D_MODEL), lambda i: (0, 1)),
                  full((1, D_MODEL)), full((D_MODEL, PROJ_DIM)), full((3, CONV_DIM)), full((1, CONV_DIM)),
                  full((tm, CONV_DIM)), full((tm, CONV_DIM))],
        out_specs=[full((tm, CONV_DIM)), full((tm, CONV_DIM)),
                   full((tm, ATTN_DIM)), full((tm, ATTN_DIM)), full((tm, ATTN_DIM)),
                   full((ATTN_DIM, tm)), full((ATTN_DIM, tm))],
        out_shape=[jax.ShapeDtypeStruct((tm, CONV_DIM), BF16),
                   jax.ShapeDtypeStruct((tm, CONV_DIM), F32),
                   jax.ShapeDtypeStruct((tm, ATTN_DIM), F32),
                   jax.ShapeDtypeStruct((tm, ATTN_DIM), F32),
                   jax.ShapeDtypeStruct((tm, ATTN_DIM), F32),
                   jax.ShapeDtypeStruct((ATTN_DIM, tm), F32),
                   jax.ShapeDtypeStruct((ATTN_DIM, tm), F32)],
        scratch_shapes=[pltpu.VMEM((tm + 8, CONV_DIM), F32)],
        compiler_params=_cparams(("arbitrary",)),
        name="inproj_sample",
    )(x, mod, mod, g1, w_in, conv_w, gn_conv, prev0, prev1)


def _t5_buckets(dist):
    dist = np.asarray(dist, np.int32)
    max_exact = N_BUCKETS // 2
    large = max_exact + (np.log(np.maximum(dist, 1).astype(np.float32) / max_exact)
                         / np.log(MAX_DISTANCE / max_exact) * (N_BUCKETS - max_exact)).astype(np.int32)
    large = np.minimum(large, N_BUCKETS - 1)
    return np.where(dist < max_exact, dist, large).astype(np.int32)


def _group_table(rel_bias, g):
    w, d = DIL_PAIRS[g]
    buckets = _t5_buckets(np.arange(w // d + 1) * d)
    return rel_bias[buckets][:, g * GROUP_HEADS:(g + 1) * GROUP_HEADS].T.astype(F32)


def _prompt_bias(table):
    period = 3 * QB - 1
    base = jnp.concatenate([table[:, ::-1], jnp.full((GROUP_HEADS, period - QB - 1), NEG_INF, F32)], axis=1)
    skew = jnp.tile(base, (1, QB))[:, :QB * (period - 1)].reshape(GROUP_HEADS, QB, period - 1)
    full = skew[:, :, :2 * QB]
    first = jnp.where((np.arange(2 * QB) >= QB)[None, None, :], full, NEG_INF)
    return jnp.stack([full, first]).reshape(2, GROUP_HEADS * QB, 2 * QB)


def _sample_bias(table, buf_len, d, t_len):
    assert buf_len == QB * d
    asc = table[:, ::-1]
    inter = jnp.concatenate([asc[:, :, None], jnp.full((GROUP_HEADS, QB + 1, d - 1), NEG_INF, F32)],
                            axis=2).reshape(GROUP_HEADS, (QB + 1) * d)
    ext_len = buf_len + LANES
    rows = []
    for t in range(8):
        lead = t if t < t_len else 0
        rows.append(jnp.pad(inter, ((0, 0), (lead, ext_len - lead - (QB + 1) * d)), constant_values=NEG_INF))
    ext = jnp.stack(rows, axis=1)
    return (ext[:, :, :buf_len].reshape(GROUP_HEADS * 8, buf_len),
            ext[:, :, buf_len:].reshape(GROUP_HEADS * 8, LANES))


def _attn_prompt_kernel(q_ref, kc_ref, vc_ref, kp_ref, vp_ref, bias_ref, o_ref, lse_ref,
                        kcat_ref, vcat_ref, *, tq):
    i = pl.program_id(1)
    kcat_ref[0:QB, :] = kp_ref[0]
    kcat_ref[QB:QB + tq, :] = kc_ref[0]
    vcat_ref[0:QB, :] = vp_ref[0]
    vcat_ref[QB:QB + tq, :] = vc_ref[0]
    hm_b = _head_masks(GROUP_DIM, BF16)
    hm_f = _head_masks(GROUP_DIM, F32)
    lm_f = _head_masks(LANES, F32)
    first = jnp.where(i == 0, 1, 0)
    for j in range(tq // QB):
        q = q_ref[0, j * QB:(j + 1) * QB, :]
        q4 = jnp.concatenate([q * hm_b[h] for h in range(GROUP_HEADS)], axis=0)
        kk = kcat_ref[j * QB:(j + 2) * QB, :]
        vv = vcat_ref[j * QB:(j + 2) * QB, :]
        s = lax.dot_general(q4, kk, (((1,), (1,)), ((), ())), preferred_element_type=F32)
        s = s + (bias_ref[first] if j == 0 else bias_ref[0])
        m = jnp.max(s, axis=-1, keepdims=True)
        e = jnp.exp(s - m)
        den = jnp.sum(e, axis=-1, keepdims=True)
        lse = m + jnp.log(den)
        pv = jnp.dot(e.astype(BF16), vv, preferred_element_type=F32)
        pv = pv / den
        o = jnp.zeros((QB, GROUP_DIM), F32)
        l = jnp.zeros((QB, LANES), F32)
        for h in range(GROUP_HEADS):
            o = o + pv[h * QB:(h + 1) * QB, :] * hm_f[h]
            l = l + lse[h * QB:(h + 1) * QB, :] * lm_f[h]
        o_ref[0, j * QB:(j + 1) * QB, :] = o
        lse_ref[0, j * QB:(j + 1) * QB, :] = l


def _attn_prompt(q, k, v, bias, tq=512):
    ns, M, _ = q.shape
    tq = min(tq, M)
    r = tq // QB
    cur = pl.BlockSpec((1, tq, GROUP_DIM), lambda s, i: (s, i, 0))
    prev = pl.BlockSpec((1, QB, GROUP_DIM), lambda s, i: (s, jnp.maximum(i * r - 1, 0), 0))
    return pl.pallas_call(
        functools.partial(_attn_prompt_kernel, tq=tq),
        grid=(ns, M // tq),
        in_specs=[cur, cur, cur, prev, prev, _const_spec(bias.shape)],
        out_specs=[pl.BlockSpec((1, tq, GROUP_DIM), lambda s, i: (s, i, 0)),
                   pl.BlockSpec((1, tq, LANES), lambda s, i: (s, i, 0))],
        out_shape=[jax.ShapeDtypeStruct((ns, M, GROUP_DIM), F32),
                   jax.ShapeDtypeStruct((ns, M, LANES), F32)],
        scratch_shapes=[pltpu.VMEM((tq + QB, GROUP_DIM), BF16), pltpu.VMEM((tq + QB, GROUP_DIM), BF16)],
        compiler_params=_cparams(("arbitrary", "arbitrary")),
        name="attn_prompt",
    )(q, k, v, k, v, bias)


def _attn_sample_kernel(q_ref, k8_ref, v8_ref, kt_ref, vt_ref, cache_ref, bc_ref, bn_ref,
                        o_ref, lse_ref, newc_ref, *, buf_len, t_len):
    b = pl.program_id(0)
    hm_f = _head_masks(GROUP_DIM, F32)
    lm_f = _head_masks(LANES, F32)
    lane = lax.broadcasted_iota(jnp.int32, (1, LANES), 1)

    q8 = q_ref[0]
    q4b = jnp.concatenate([q8 * hm_f[h] for h in range(GROUP_HEADS)], axis=0).astype(BF16)
    kt = cache_ref[0, 0]
    vt = cache_ref[0, 1]
    s_c = jnp.dot(q4b, kt.astype(BF16), preferred_element_type=F32) + bc_ref[...]

    q4r = q4b.astype(F32)
    k8r = k8_ref[0].astype(BF16).astype(F32)
    v8r = v8_ref[0].astype(BF16).astype(F32)
    s_n = bn_ref[...]
    for t in range(t_len):
        col = jnp.sum(q4r * k8r[t:t + 1, :], axis=-1, keepdims=True)
        s_n = s_n + jnp.where(lane == t, col, 0.0)

    m = jnp.maximum(jnp.max(s_c, axis=-1, keepdims=True), jnp.max(s_n, axis=-1, keepdims=True))
    e_c = jnp.exp(s_c - m)
    e_n = jnp.exp(s_n - m)
    den = jnp.sum(e_c, axis=-1, keepdims=True) + jnp.sum(e_n, axis=-1, keepdims=True)
    pv = lax.dot_general(e_c.astype(BF16), vt.astype(BF16), (((1,), (1,)), ((), ())),
                         preferred_element_type=F32)
    e_nr = e_n.astype(BF16).astype(F32)
    for t in range(t_len):
        w_t = jnp.sum(jnp.where(lane == t, e_nr, 0.0), axis=-1, keepdims=True)
        pv = pv + w_t * v8r[t:t + 1, :]
    pv = pv / den
    lse = m + jnp.log(den)
    o = jnp.zeros((8, GROUP_DIM), F32)
    l = jnp.zeros((8, LANES), F32)
    for h in range(GROUP_HEADS):
        o = o + pv[h * 8:(h + 1) * 8, :] * hm_f[h]
        l = l + lse[h * 8:(h + 1) * 8, :] * lm_f[h]
    o_ref[0] = o
    lse_ref[0] = l

    keep = LANES - t_len
    per_block = LANES // t_len
    shift_new = keep - (b % per_block) * t_len
    for kv, new_ref in ((0, kt_ref), (1, vt_ref)):
        new_r = pltpu.roll(new_ref[...], shift_new, 1)
        prev_r = None
        for c in range(buf_len // LANES):
            r_c = pltpu.roll(cache_ref[0, kv, :, c * LANES:(c + 1) * LANES], keep, 1)
            if c > 0:
                newc_ref[0, kv, :, (c - 1) * LANES:c * LANES] = jnp.where(lane < keep, prev_r, r_c)
            prev_r = r_c
        newc_ref[0, kv, :, buf_len - LANES:buf_len] = jnp.where(lane < keep, prev_r, new_r)


def _attn_sample(q8, k8, v8, kt, vt, cache, bc, bn, g, t_len):
    nb, _, _, buf_len = cache.shape
    per_block = LANES // t_len
    new_spec = pl.BlockSpec((1, 8, GROUP_DIM), lambda b: (b, 0, g))
    newt_spec = pl.BlockSpec((GROUP_DIM, LANES), lambda b: (g, b // per_block))
    cache_spec = pl.BlockSpec((1, 2, GROUP_DIM, buf_len), lambda b: (b, 0, 0, 0))
    return pl.pallas_call(
        functools.partial(_attn_sample_kernel, buf_len=buf_len, t_len=t_len),
        grid=(nb,),
        in_specs=[new_spec, new_spec, new_spec, newt_spec, newt_spec, cache_spec,
                  _const_spec(bc.shape), _const_spec(bn.shape)],
        out_specs=[pl.BlockSpec((1, 8, GROUP_DIM), lambda b: (b, 0, 0)),
                   pl.BlockSpec((1, 8, LANES), lambda b: (b, 0, 0)),
                   cache_spec],
        out_shape=[jax.ShapeDtypeStruct((nb, 8, GROUP_DIM), F32),
                   jax.ShapeDtypeStruct((nb, 8, LANES), F32),
                   jax.ShapeDtypeStruct(cache.shape, F32)],
        compiler_params=_cparams(("arbitrary",)),
        name="attn_sample",
    )(q8, k8, v8, kt, vt, cache, bc, bn)


def _load_positions(ref, slab_ref):
    _, d, m, width = ref.shape
    if d == 1:
        return ref[0, 0]
    for r in range(d):
        for s in range(width // LANES):
            slab_ref[s, pl.ds(r, m, stride=d), :] = ref[0, r, :, s * LANES:(s + 1) * LANES]
    return jnp.concatenate([slab_ref[s] for s in range(width // LANES)], axis=-1)


def _out_kernel(x_ref, ga1_ref, sh2_ref, sc2_ref, ga2_ref, convn_ref,
                o1_ref, o2_ref, o3_ref, l1_ref, l2_ref, l3_ref,
                gna_ref, wout_ref, g2_ref, wg_ref, wu_ref, wd_ref, fg_ref, y_ref, slab_ref, *, ff_chunk):
    x = x_ref[0]
    hm_f = _head_masks(GROUP_DIM, F32)
    lm_f = _head_masks(LANES, F32)

    ls = [_load_positions(l_ref, slab_ref) for l_ref in (l1_ref, l2_ref, l3_ref)]
    m = jnp.maximum(jnp.maximum(ls[0], ls[1]), ls[2])
    es = [jnp.exp(l - m) for l in ls]
    tot = es[0] + es[1] + es[2]
    parts = []
    ssq = jnp.zeros((x.shape[0], 1), F32)
    for g, o_ref in enumerate((o1_ref, o2_ref, o3_ref)):
        alpha = es[g] / tot
        wide = jnp.zeros((x.shape[0], GROUP_DIM), F32)
        for h in range(GROUP_HEADS):
            a_h = jnp.sum(alpha * lm_f[h], axis=-1, keepdims=True)
            wide = wide + a_h * hm_f[h]
        ao = wide * _load_positions(o_ref, slab_ref)
        ssq = ssq + jnp.sum(ao * ao, axis=-1, keepdims=True)
        parts.append(ao)
    rinv = lax.rsqrt(ssq / ATTN_DIM + EPS)
    gna = gna_ref[...]
    mixed = jnp.concatenate(
        [convn_ref[0]] + [(parts[g] * rinv * gna[:, g * GROUP_DIM:(g + 1) * GROUP_DIM]).astype(BF16)
                          for g in range(N_DIL)], axis=-1)
    mix = jnp.dot(mixed, wout_ref[...], preferred_element_type=F32)
    x1 = x + ga1_ref[0] * mix

    h2 = (_rms(x1, g2_ref[...]) * (1.0 + sc2_ref[0]) + sh2_ref[0]).astype(BF16)
    ffn = jnp.zeros_like(x1)
    for c in range(D_FF // ff_chunk):
        sl = slice(c * ff_chunk, (c + 1) * ff_chunk)
        gate = jnp.dot(h2, wg_ref[:, sl], preferred_element_type=F32)
        up = jnp.dot(h2, wu_ref[:, sl], preferred_element_type=F32)
        act = (gate * jax.nn.sigmoid(gate) * up).astype(BF16)
        ffn = ffn + jnp.dot(act, wd_ref[sl, :], preferred_element_type=F32)
    x2 = x1 + ga2_ref[0] * ffn
    y_ref[0] = _rms(x2, fg_ref[...])


def _out(x, mod, convn, os_, ls_, gn_attn, w_out, g2, w_gate, w_up, w_down, final_g, tm=512, ff_chunk=1408):
    nb, R, _ = x.shape
    tm = min(tm, R)
    per_row = mod.shape[1] != 1
    mrows = tm if per_row else 1
    row = lambda b, i: (b, i, 0)

    def mod_spec(c):
        return pl.BlockSpec((1, mrows, D_MODEL), lambda b, i: (b, i if per_row else 0, c))

    def stream_spec(a):
        _, d, _, width = a.shape
        return pl.BlockSpec((1, d, tm // d, width), lambda b, i: (b, 0, i, 0))

    return pl.pallas_call(
        functools.partial(_out_kernel, ff_chunk=ff_chunk),
        grid=(nb, R // tm),
        in_specs=[pl.BlockSpec((1, tm, D_MODEL), row),
                  mod_spec(2), mod_spec(3), mod_spec(4), mod_spec(5),
                  pl.BlockSpec((1, tm, CONV_DIM), row)]
                 + [stream_spec(a) for a in os_] + [stream_spec(a) for a in ls_]
                 + [_const_spec((1, ATTN_DIM)), _const_spec((D_MODEL, D_MODEL)), _const_spec((1, D_MODEL)),
                    _const_spec((D_MODEL, D_FF)), _const_spec((D_MODEL, D_FF)), _const_spec((D_FF, D_MODEL)),
                    _const_spec((1, D_MODEL))],
        out_specs=pl.BlockSpec((1, tm, D_MODEL), row),
        out_shape=jax.ShapeDtypeStruct(x.shape, F32),
        scratch_shapes=[pltpu.VMEM((GROUP_DIM // LANES, tm, LANES), F32)],
        compiler_params=_cparams(("arbitrary", "arbitrary")),
        name="out",
    )(x, mod, mod, mod, mod, convn, *os_, *ls_, gn_attn, w_out, g2, w_gate, w_up, w_down, final_g)


def _buffer_to_channel_major(cache):
    B, L = cache.shape[:2]
    return cache.transpose(0, 2, 3, 4, 1).reshape(B, 2, GROUP_DIM, L)


def _buffer_from_channel_major(buf):
    B, _, _, L = buf.shape
    return buf.reshape(B, 2, GROUP_HEADS, HEAD_DIM, L).transpose(0, 4, 1, 2, 3)[None]


def kernel(x_prompt, x_sample, state_conv, cache_kv1, cache_kv2, cache_kv3, c_prompt, c_sample, w_ada, b_ada, norm1_g, norm2_g, w_in, conv_w, gn_conv, gn_attn, w_out, w_gate, w_up, w_down, rel_bias, final_g):
    B, S, _ = x_prompt.shape
    nb, t_len, _ = x_sample.shape
    rows_s = nb * t_len
    assert w_ada.shape[0] == 1 and LANES % t_len == 0 and 2 <= t_len <= 8 and rows_s % LANES == 0

    w_in_b = w_in[0].astype(BF16)
    w_out_b = w_out[0].astype(BF16)
    w_gate_b = w_gate[0].astype(BF16)
    w_up_b = w_up[0].astype(BF16)
    w_down_b = w_down[0].astype(BF16)
    final_g2 = final_g.reshape(1, D_MODEL)
    tables = [_group_table(rel_bias, g) for g in range(N_DIL)]

    c_all = jnp.concatenate([c_prompt, jnp.repeat(c_sample, t_len, axis=0)], axis=0)
    mod = _ada(c_all, w_ada[0], b_ada)
    mod_p = mod[:B].reshape(B, 1, 6 * D_MODEL)
    mod_s = mod[B:]

    (convn, q1, q2, q3, k1, k2, k3, v1, v2, v3,
     nconv_p, kv1_p, kv2_p, kv3_p) = _inproj_prompt(x_prompt, mod_p, norm1_g, w_in_b, conv_w[0], gn_conv)
    os_, ls_ = [], []
    for g, (qg, kg, vg) in enumerate(((q1, k1, v1), (q2, k2, v2), (q3, k3, v3))):
        _, d, M, _ = qg.shape
        merge = lambda a: a.reshape(B * d, M, GROUP_DIM)
        o, l = _attn_prompt(merge(qg), merge(kg), merge(vg), _prompt_bias(tables[g]))
        os_.append(o.reshape(B, d, M, GROUP_DIM))
        ls_.append(l.reshape(B, d, M, LANES))
    y_prompt = _out(x_prompt, mod_p, convn, os_, ls_, gn_attn, w_out_b, norm2_g,
                    w_gate_b, w_up_b, w_down_b, final_g2)

    xs = x_sample.reshape(rows_s, D_MODEL)
    prev0 = jnp.repeat(state_conv[0, :, 0], t_len, axis=0)
    prev1 = jnp.repeat(state_conv[0, :, 1], t_len, axis=0)
    convn_s, u_s, q_s, k_s, v_s, kt_s, vt_s = _inproj_sample(
        xs, mod_s, norm1_g, w_in_b, conv_w[0], gn_conv, prev0, prev1, t_len)
    pad8 = lambda a: jnp.pad(a.reshape(nb, t_len, ATTN_DIM), ((0, 0), (0, 8 - t_len), (0, 0)))
    q8, k8, v8 = pad8(q_s), pad8(k_s), pad8(v_s)
    os_s, ls_s, new_bufs = [], [], []
    for g, (cache, (_, d)) in enumerate(zip((cache_kv1, cache_kv2, cache_kv3), DIL_PAIRS)):
        buf = _buffer_to_channel_major(cache[0])
        bc, bn = _sample_bias(tables[g], buf.shape[-1], d, t_len)
        o, l, nbuf = _attn_sample(q8, k8, v8, kt_s, vt_s, buf, bc, bn, g, t_len)
        os_s.append(o[:, :t_len].reshape(1, 1, rows_s, GROUP_DIM))
        ls_s.append(l[:, :t_len].reshape(1, 1, rows_s, LANES))
        new_bufs.append(_buffer_from_channel_major(nbuf))
    y_sample = _out(xs[None], mod_s[None], convn_s[None], os_s, ls_s, gn_attn, w_out_b, norm2_g,
                    w_gate_b, w_up_b, w_down_b, final_g2)
    y_sample = y_sample.reshape(nb, t_len, D_MODEL)
    nconv_s = u_s.reshape(nb, t_len, CONV_DIM)[:, t_len - 2:]

    return (y_prompt, y_sample, nconv_p[None],
            _buffer_from_channel_major(kv1_p), _buffer_from_channel_major(kv2_p),
            _buffer_from_channel_major(kv3_p),
            nconv_s[None], new_bufs[0], new_bufs[1], new_bufs[2])
```

```python
import functools

import numpy as np
import jax
import jax.numpy as jnp
from jax import lax
from jax.experimental import pallas as pl
from jax.experimental.pallas import tpu as pltpu

F32 = jnp.float32
BF16 = jnp.bfloat16

D_MODEL = 1024
HEAD_DIM = 64
CONV_DIM = 256
ATTN_DIM = 768
GROUP_HEADS = 4
GROUP_DIM = GROUP_HEADS * HEAD_DIM
DIL_PAIRS = ((128, 1), (512, 4), (2048, 16))
N_DIL = len(DIL_PAIRS)
QB = 128
N_BUCKETS = 32
MAX_DISTANCE = 2048
D_FF = 2816
PROJ_DIM = 3072
EPS = 1e-6
NEG_INF = -1e30
ATTN_SCALE = HEAD_DIM ** -0.5

LANES = 128
VMEM_LIMIT = 56 * 1024 * 1024


def _cparams(sem):
    return pltpu.CompilerParams(dimension_semantics=sem, vmem_limit_bytes=VMEM_LIMIT)


def _const_spec(shape):
    nd = len(shape)
    return pl.BlockSpec(shape, lambda *_: (0,) * nd, pipeline_mode=pl.Buffered(1))


def _head_masks(width, dtype):
    lane = lax.broadcasted_iota(jnp.int32, (1, width), 1)
    if width == GROUP_DIM:
        lane = lane // HEAD_DIM
    return [(lane == h).astype(dtype) for h in range(GROUP_HEADS)]


def _ada_kernel(c_ref, w_ref, b_ref, o_ref):
    c = c_ref[...]
    s = (c * jax.nn.sigmoid(c)).astype(BF16)
    o_ref[...] = jnp.dot(s, w_ref[...].astype(BF16), preferred_element_type=F32) + b_ref[...]


def _ada(c_all, w_ada, b_ada):
    rows = c_all.shape[0]
    tn = 1536
    return pl.pallas_call(
        _ada_kernel,
        grid=(6 * D_MODEL // tn,),
        in_specs=[pl.BlockSpec((rows, D_MODEL), lambda j: (0, 0)),
                  pl.BlockSpec((D_MODEL, tn), lambda j: (0, j)),
                  pl.BlockSpec((1, tn), lambda j: (0, j))],
        out_specs=pl.BlockSpec((rows, tn), lambda j: (0, j)),
        out_shape=jax.ShapeDtypeStruct((rows, 6 * D_MODEL), F32),
        compiler_params=_cparams(("arbitrary",)),
        name="ada",
    )(c_all, w_ada, b_ada)


def _rms(x, g):
    return x * lax.rsqrt(jnp.mean(x * x, axis=-1, keepdims=True) + EPS) * g


def _store_streams(p, out_refs, slab_ref, tm):
    for g, ((_, d), out_ref) in enumerate(zip(DIL_PAIRS, out_refs)):
        lo = g * GROUP_DIM
        if d == 1:
            out_ref[0, 0] = p[:, lo:lo + GROUP_DIM].astype(BF16)
            continue
        for s in range(GROUP_DIM // LANES):
            slab_ref[s] = p[:, lo + s * LANES:lo + (s + 1) * LANES]
        for r in range(d):
            for s in range(GROUP_DIM // LANES):
                out_ref[0, r, :, s * LANES:(s + 1) * LANES] = (
                    slab_ref[s, pl.ds(r, tm // d, stride=d), :].astype(BF16))


def _inproj_prompt_kernel(x_ref, sh_ref, sc_ref, g1_ref, w_ref, cw_ref, gnc_ref,
                          convn_ref, q1_ref, q2_ref, q3_ref, k1_ref, k2_ref, k3_ref,
                          v1_ref, v2_ref, v3_ref, nconv_ref, kv1_ref, kv2_ref, kv3_ref,
                          ubuf_ref, slab_ref, *, tm):
    i = pl.program_id(1)
    n = pl.num_programs(1)
    h = _rms(x_ref[0], g1_ref[...]) * (1.0 + sc_ref[0]) + sh_ref[0]
    hb = h.astype(BF16)

    pc = jnp.dot(hb, w_ref[:, 0:3 * CONV_DIM], preferred_element_type=F32)
    gb = pc[:, 0:CONV_DIM]
    u = pc[:, CONV_DIM:2 * CONV_DIM] * pc[:, 2 * CONV_DIM:3 * CONV_DIM]

    @pl.when(i == 0)
    def _():
        ubuf_ref[0:8, :] = jnp.zeros((8, CONV_DIM), F32)

    @pl.when(i > 0)
    def _():
        ubuf_ref[0:8, :] = ubuf_ref[tm:tm + 8, :]

    ubuf_ref[8:tm + 8, :] = u
    um1 = ubuf_ref[7:tm + 7, :]
    um2 = ubuf_ref[6:tm + 6, :]
    cw = cw_ref[...]
    z = cw[0:1, :] * um2 + cw[1:2, :] * um1 + cw[2:3, :] * u
    convn_ref[0] = _rms(gb * z, gnc_ref[...]).astype(BF16)

    @pl.when(i == n - 1)
    def _():
        nconv_ref[0] = ubuf_ref[tm + 6:tm + 8, :]

    c0 = 3 * CONV_DIM
    pq = jnp.dot(hb, w_ref[:, c0:c0 + ATTN_DIM], preferred_element_type=F32)
    _store_streams(pq * ATTN_SCALE, (q1_ref, q2_ref, q3_ref), slab_ref, tm)
    pk = jnp.dot(hb, w_ref[:, c0 + ATTN_DIM:c0 + 2 * ATTN_DIM], preferred_element_type=F32)
    _store_streams(pk, (k1_ref, k2_ref, k3_ref), slab_ref, tm)
    pv = jnp.dot(hb, w_ref[:, c0 + 2 * ATTN_DIM:c0 + 3 * ATTN_DIM], preferred_element_type=F32)
    _store_streams(pv, (v1_ref, v2_ref, v3_ref), slab_ref, tm)

    def put(ref, g, rows):
        lo = g * GROUP_DIM
        ref[0, 0] = pk[tm - rows:tm, lo:lo + GROUP_DIM].T
        ref[0, 1] = pv[tm - rows:tm, lo:lo + GROUP_DIM].T

    w3 = DIL_PAIRS[2][0]

    @pl.when(i >= n - w3 // tm)
    def _():
        put(kv3_ref, 2, tm)

    @pl.when(i == n - 1)
    def _():
        put(kv2_ref, 1, DIL_PAIRS[1][0])
        put(kv1_ref, 0, DIL_PAIRS[0][0])


def _inproj_prompt(x, mod, g1, w_in, conv_w, gn_conv, tm=512):
    B, S, _ = x.shape
    n = S // tm
    w1, w2, w3 = (w for w, _ in DIL_PAIRS)
    assert S % tm == 0 and w3 % tm == 0 and tm >= w2
    n3 = w3 // tm
    row = lambda b, i: (b, i, 0)
    stream_specs = [pl.BlockSpec((1, d, tm // d, GROUP_DIM), lambda b, i: (b, 0, i, 0)) for _, d in DIL_PAIRS]
    stream_shapes = [jax.ShapeDtypeStruct((B, d, S // d, GROUP_DIM), BF16) for _, d in DIL_PAIRS]
    outs = pl.pallas_call(
        functools.partial(_inproj_prompt_kernel, tm=tm),
        grid=(B, n),
        in_specs=[pl.BlockSpec((1, tm, D_MODEL), row),
                  pl.BlockSpec((1, 1, D_MODEL), lambda b, i: (b, 0, 0)),
                  pl.BlockSpec((1, 1, D_MODEL), lambda b, i: (b, 0, 1)),
                  _const_spec((1, D_MODEL)),
                  _const_spec((D_MODEL, PROJ_DIM)),
                  _const_spec((3, CONV_DIM)),
                  _const_spec((1, CONV_DIM))],
        out_specs=[pl.BlockSpec((1, tm, CONV_DIM), row)] + stream_specs * 3 + [
                   pl.BlockSpec((1, 2, CONV_DIM), lambda b, i: (b, 0, 0)),
                   pl.BlockSpec((1, 2, GROUP_DIM, w1), lambda b, i: (b, 0, 0, 0)),
                   pl.BlockSpec((1, 2, GROUP_DIM, w2), lambda b, i: (b, 0, 0, 0)),
                   pl.BlockSpec((1, 2, GROUP_DIM, tm),
                                lambda b, i: (b, 0, 0, jnp.maximum(i - (n - n3), 0)))],
        out_shape=[jax.ShapeDtypeStruct((B, S, CONV_DIM), BF16)] + stream_shapes * 3 + [
                   jax.ShapeDtypeStruct((B, 2, CONV_DIM), F32),
                   jax.ShapeDtypeStruct((B, 2, GROUP_DIM, w1), F32),
                   jax.ShapeDtypeStruct((B, 2, GROUP_DIM, w2), F32),
                   jax.ShapeDtypeStruct((B, 2, GROUP_DIM, w3), F32)],
        scratch_shapes=[pltpu.VMEM((tm + 8, CONV_DIM), F32),
                        pltpu.VMEM((GROUP_DIM // LANES, tm, LANES), F32)],
        compiler_params=_cparams(("arbitrary", "arbitrary")),
        name="inproj_prompt",
    )(x, mod, mod, g1, w_in, conv_w, gn_conv)
    return outs


def _inproj_sample_kernel(x_ref, sh_ref, sc_ref, g1_ref, w_ref, cw_ref, gnc_ref, p0_ref, p1_ref,
                          convn_ref, u_ref, q_ref, k_ref, v_ref, kt_ref, vt_ref, ubuf_ref, *, tm, t_len):
    h = _rms(x_ref[...], g1_ref[...]) * (1.0 + sc_ref[...]) + sh_ref[...]
    hb = h.astype(BF16)

    pc = jnp.dot(hb, w_ref[:, 0:3 * CONV_DIM], preferred_element_type=F32)
    gb = pc[:, 0:CONV_DIM]
    u = pc[:, CONV_DIM:2 * CONV_DIM] * pc[:, 2 * CONV_DIM:3 * CONV_DIM]
    u_ref[...] = u

    ubuf_ref[0:8, :] = jnp.zeros((8, CONV_DIM), F32)
    ubuf_ref[8:tm + 8, :] = u
    t = lax.broadcasted_iota(jnp.int32, (tm, 1), 0) % t_len
    um1 = jnp.where(t == 0, p1_ref[...], ubuf_ref[7:tm + 7, :])
    um2 = jnp.where(t == 0, p0_ref[...], jnp.where(t == 1, p1_ref[...], ubuf_ref[6:tm + 6, :]))
    cw = cw_ref[...]
    z = cw[0:1, :] * um2 + cw[1:2, :] * um1 + cw[2:3, :] * u
    convn_ref[...] = _rms(gb * z, gnc_ref[...]).astype(BF16)

    c0 = 3 * CONV_DIM
    pq = jnp.dot(hb, w_ref[:, c0:c0 + ATTN_DIM], preferred_element_type=F32)
    q_ref[...] = pq * ATTN_SCALE
    pk = jnp.dot(hb, w_ref[:, c0 + ATTN_DIM:c0 + 2 * ATTN_DIM], preferred_element_type=F32)
    k_ref[...] = pk
    kt_ref[...] = pk.T
    pv = jnp.dot(hb, w_ref[:, c0 + 2 * ATTN_DIM:c0 + 3 * ATTN_DIM], preferred_element_type=F32)
    v_ref[...] = pv
    vt_ref[...] = pv.T


def _inproj_sample(x, mod, g1, w_in, conv_w, gn_conv, prev0, prev1, t_len):
    tm = x.shape[0]
    full = lambda shape: pl.BlockSpec(shape, lambda i: (0,) * len(shape))
    return pl.pallas_call(
        functools.partial(_inproj_sample_kernel, tm=tm, t_len=t_len),
        grid=(1,),
        in_specs=[full((tm, D_MODEL)),
                  pl.BlockSpec((tm, D_MODEL), lambda i: (0, 0)),
                  pl.BlockSpec((tm, D_MODEL), lambda i: (0, 1)),
                  full((1, D_MODEL)), full((D_MODEL, PROJ_DIM)), full((3, CONV_DIM)), full((1, CONV_DIM)),
                  full((tm, CONV_DIM)), full((tm, CONV_DIM))],
        out_specs=[full((tm, CONV_DIM)), full((tm, CONV_DIM)),
                   full((tm, ATTN_DIM)), full((tm, ATTN_DIM)), full((tm, ATTN_DIM)),
                   full((ATTN_DIM, tm)), full((ATTN_DIM, tm))],
        out_shape=[jax.ShapeDtypeStruct((tm, CONV_DIM), BF16),
                   jax.ShapeDtypeStruct((tm, CONV_DIM), F32),
                   jax.ShapeDtypeStruct((tm, ATTN_DIM), F32),
                   jax.ShapeDtypeStruct((tm, ATTN_DIM), F32),
                   jax.ShapeDtypeStruct((tm, ATTN_DIM), F32),
                   jax.ShapeDtypeStruct((ATTN_DIM, tm), F32),
                   jax.ShapeDtypeStruct((ATTN_DIM, tm), F32)],
        scratch_shapes=[pltpu.VMEM((tm + 8, CONV_DIM), F32)],
        compiler_params=_cparams(("arbitrary",)),
        name="inproj_sample",
    )(x, mod, mod, g1, w_in, conv_w, gn_conv, prev0, prev1)


def _t5_buckets(dist):
    dist = np.asarray(dist, np.int32)
    max_exact = N_BUCKETS // 2
    large = max_exact + (np.log(np.maximum(dist, 1).astype(np.float32) / max_exact)
                         / np.log(MAX_DISTANCE / max_exact) * (N_BUCKETS - max_exact)).astype(np.int32)
    large = np.minimum(large, N_BUCKETS - 1)
    return np.where(dist < max_exact, dist, large).astype(np.int32)


def _group_table(rel_bias, g):
    w, d = DIL_PAIRS[g]
    buckets = _t5_buckets(np.arange(w // d + 1) * d)
    return rel_bias[buckets][:, g * GROUP_HEADS:(g + 1) * GROUP_HEADS].T.astype(F32)


def _prompt_bias(table):
    period = 3 * QB - 1
    base = jnp.concatenate([table[:, ::-1], jnp.full((GROUP_HEADS, period - QB - 1), NEG_INF, F32)], axis=1)
    skew = jnp.tile(base, (1, QB))[:, :QB * (period - 1)].reshape(GROUP_HEADS, QB, period - 1)
    prev, cur = skew[:, :, 0:QB], skew[:, :, QB:2 * QB]
    tri = (np.arange(QB)[None, :] <= np.arange(QB)[:, None])[None]
    fold = jnp.stack([jnp.where(tri, cur, prev), jnp.where(tri, cur, NEG_INF)])
    fold = fold.transpose(0, 3, 1, 2).reshape(2, QB, GROUP_HEADS * QB)
    lone = jnp.broadcast_to(table[:, QB:QB + 1], (GROUP_HEADS, QB)).reshape(1, 1, GROUP_HEADS * QB)
    return fold, jnp.concatenate([lone, jnp.full_like(lone, NEG_INF)], axis=0)


def _sample_bias(table, buf_len, d, t_len):
    assert buf_len == QB * d
    asc = table[:, ::-1]
    inter = jnp.concatenate([asc[:, :, None], jnp.full((GROUP_HEADS, QB + 1, d - 1), NEG_INF, F32)],
                            axis=2).reshape(GROUP_HEADS, (QB + 1) * d)
    ext_len = buf_len + LANES
    rows = []
    for t in range(8):
        lead = t if t < t_len else 0
        rows.append(jnp.pad(inter, ((0, 0), (lead, ext_len - lead - (QB + 1) * d)), constant_values=NEG_INF))
    ext = jnp.stack(rows, axis=1)
    return (ext[:, :, :buf_len].reshape(GROUP_HEADS * 8, buf_len),
            ext[:, :, buf_len:].reshape(GROUP_HEADS * 8, LANES))


def _attn_prompt_kernel(q_ref, kc_ref, vc_ref, kp_ref, vp_ref, bias_ref, dbias_ref, o_ref, lse_ref):
    i = pl.program_id(1)
    sb, tq, _ = q_ref.shape
    hm_b = _head_masks(GROUP_DIM, BF16)
    first = jnp.where(i == 0, 1, 0)
    c_idx = lax.broadcasted_iota(jnp.int32, (QB, GROUP_HEADS * QB), 0)
    q_idx = lax.broadcasted_iota(jnp.int32, (QB, GROUP_HEADS * QB), 1) & (QB - 1)
    tri = c_idx <= q_idx
    tri_f = tri.astype(F32)
    diag_f = (c_idx == q_idx).astype(F32)
    work = [(s, j) for s in range(sb) for j in range(tq // QB)]

    def scores(s, j):
        q = q_ref[s, j * QB:(j + 1) * QB, :]
        q4 = jnp.concatenate([q * hm_b[h] for h in range(GROUP_HEADS)], axis=0)
        if j == 0:
            kk = jnp.concatenate([kp_ref[s], kc_ref[s, 0:QB, :]], axis=0)
        else:
            kk = kc_ref[s, (j - 1) * QB:(j + 1) * QB, :]
        return lax.dot_general(kk, q4, (((1,), (1,)), ((), ())), preferred_element_type=F32)

    def values_t(s, j):
        if j == 0:
            return jnp.concatenate([vp_ref[s].T, vc_ref[s, 0:QB, :].T], axis=1)
        return vc_ref[s, (j - 1) * QB:(j + 1) * QB, :].T

    st_next = scores(*work[0])
    for n, (s, j) in enumerate(work):
        st = st_next
        if n + 1 < len(work):
            st_next = scores(*work[n + 1])
        slot = first if j == 0 else 0
        sp = st[0:QB]
        sc = st[QB:2 * QB]
        w = jnp.where(tri, sc, sp) + bias_ref[slot]
        sd = jnp.sum(sp * diag_f, axis=0, keepdims=True) + dbias_ref[slot]
        m = jnp.maximum(jnp.max(w, axis=0, keepdims=True), sd)
        e = jnp.exp(w - m)
        ed = jnp.exp(sd - m)
        den = jnp.sum(e, axis=0, keepdims=True) + ed
        lse = m + jnp.log(den)
        rden = 1.0 / den
        p_cur = e * tri_f
        pt = jnp.concatenate([e - p_cur, p_cur], axis=0).astype(BF16)
        ed_r = ed.astype(BF16).astype(F32)
        vt = values_t(s, j)
        ots = []
        for h in range(GROUP_HEADS):
            cols = slice(h * QB, (h + 1) * QB)
            rows = slice(h * HEAD_DIM, (h + 1) * HEAD_DIM)
            ot = jnp.dot(vt[rows, :], pt[:, cols], preferred_element_type=F32)
            ot = ot + vt[rows, 0:QB].astype(F32) * ed_r[:, cols]
            ots.append(ot * rden[:, cols])
        o_ref[s, j * QB:(j + 1) * QB, :] = jnp.concatenate(ots, axis=0).T
        lrows = jnp.concatenate([lse[:, h * QB:(h + 1) * QB] for h in range(GROUP_HEADS)]
                                + [jnp.zeros((LANES - GROUP_HEADS, QB), F32)], axis=0)
        lse_ref[s, j * QB:(j + 1) * QB, :] = lrows.T


def _attn_prompt(q, k, v, bias, dbias, block_rows=2048):
    ns, M, _ = q.shape
    tq = min(block_rows, M)
    sb = block_rows // tq
    r = tq // QB
    cur = pl.BlockSpec((sb, tq, GROUP_DIM), lambda s, i: (s, i, 0))
    prev = pl.BlockSpec((sb, QB, GROUP_DIM), lambda s, i: (s, jnp.maximum(i * r - 1, 0), 0))
    return pl.pallas_call(
        _attn_prompt_kernel,
        grid=(ns // sb, M // tq),
        in_specs=[cur, cur, cur, prev, prev, _const_spec(bias.shape), _const_spec(dbias.shape)],
        out_specs=[pl.BlockSpec((sb, tq, GROUP_DIM), lambda s, i: (s, i, 0)),
                   pl.BlockSpec((sb, tq, LANES), lambda s, i: (s, i, 0))],
        out_shape=[jax.ShapeDtypeStruct((ns, M, GROUP_DIM), F32),
                   jax.ShapeDtypeStruct((ns, M, LANES), F32)],
        compiler_params=_cparams(("arbitrary", "arbitrary")),
        name="attn_prompt",
    )(q, k, v, k, v, bias, dbias)


def _attn_sample_kernel(q_ref, k8_ref, v8_ref, kt_ref, vt_ref, cache_ref, bc_ref, bn_ref,
                        o_ref, lse_ref, newc_ref, *, buf_len, t_len):
    b = pl.program_id(0)
    hm_f = _head_masks(GROUP_DIM, F32)
    lm_f = _head_masks(LANES, F32)
    lane = lax.broadcasted_iota(jnp.int32, (1, LANES), 1)

    q8 = q_ref[0]
    q4b = jnp.concatenate([q8 * hm_f[h] for h in range(GROUP_HEADS)], axis=0).astype(BF16)
    kt = cache_ref[0, 0]
    vt = cache_ref[0, 1]
    s_c = jnp.dot(q4b, kt.astype(BF16), preferred_element_type=F32) + bc_ref[...]

    q4r = q4b.astype(F32)
    k8r = k8_ref[0].astype(BF16).astype(F32)
    v8r = v8_ref[0].astype(BF16).astype(F32)
    s_n = bn_ref[...]
    for t in range(t_len):
        col = jnp.sum(q4r * k8r[t:t + 1, :], axis=-1, keepdims=True)
        s_n = s_n + jnp.where(lane == t, col, 0.0)

    m = jnp.maximum(jnp.max(s_c, axis=-1, keepdims=True), jnp.max(s_n, axis=-1, keepdims=True))
    e_c = jnp.exp(s_c - m)
    e_n = jnp.exp(s_n - m)
    den = jnp.sum(e_c, axis=-1, keepdims=True) + jnp.sum(e_n, axis=-1, keepdims=True)
    pv = lax.dot_general(e_c.astype(BF16), vt.astype(BF16), (((1,), (1,)), ((), ())),
                         preferred_element_type=F32)
    e_nr = e_n.astype(BF16).astype(F32)
    for t in range(t_len):
        w_t = jnp.sum(jnp.where(lane == t, e_nr, 0.0), axis=-1, keepdims=True)
        pv = pv + w_t * v8r[t:t + 1, :]
    pv = pv / den
    lse = m + jnp.log(den)
    o = jnp.zeros((8, GROUP_DIM), F32)
    l = jnp.zeros((8, LANES), F32)
    for h in range(GROUP_HEADS):
        o = o + pv[h * 8:(h + 1) * 8, :] * hm_f[h]
        l = l + lse[h * 8:(h + 1) * 8, :] * lm_f[h]
    o_ref[0] = o
    lse_ref[0] = l

    keep = LANES - t_len
    per_block = LANES // t_len
    shift_new = keep - (b % per_block) * t_len
    for kv, new_ref in ((0, kt_ref), (1, vt_ref)):
        new_r = pltpu.roll(new_ref[...], shift_new, 1)
        prev_r = None
        for c in range(buf_len // LANES):
            r_c = pltpu.roll(cache_ref[0, kv, :, c * LANES:(c + 1) * LANES], keep, 1)
            if c > 0:
                newc_ref[0, kv, :, (c - 1) * LANES:c * LANES] = jnp.where(lane < keep, prev_r, r_c)
            prev_r = r_c
        newc_ref[0, kv, :, buf_len - LANES:buf_len] = jnp.where(lane < keep, prev_r, new_r)


def _attn_sample(q8, k8, v8, kt, vt, cache, bc, bn, g, t_len):
    nb, _, _, buf_len = cache.shape
    per_block = LANES // t_len
    new_spec = pl.BlockSpec((1, 8, GROUP_DIM), lambda b: (b, 0, g))
    newt_spec = pl.BlockSpec((GROUP_DIM, LANES), lambda b: (g, b // per_block))
    cache_spec = pl.BlockSpec((1, 2, GROUP_DIM, buf_len), lambda b: (b, 0, 0, 0))
    return pl.pallas_call(
        functools.partial(_attn_sample_kernel, buf_len=buf_len, t_len=t_len),
        grid=(nb,),
        in_specs=[new_spec, new_spec, new_spec, newt_spec, newt_spec, cache_spec,
                  _const_spec(bc.shape), _const_spec(bn.shape)],
        out_specs=[pl.BlockSpec((1, 8, GROUP_DIM), lambda b: (b, 0, 0)),
                   pl.BlockSpec((1, 8, LANES), lambda b: (b, 0, 0)),
                   cache_spec],
        out_shape=[jax.ShapeDtypeStruct((nb, 8, GROUP_DIM), F32),
                   jax.ShapeDtypeStruct((nb, 8, LANES), F32),
                   jax.ShapeDtypeStruct(cache.shape, F32)],
        compiler_params=_cparams(("arbitrary",)),
        name="attn_sample",
    )(q8, k8, v8, kt, vt, cache, bc, bn)


def _load_positions(ref, slab_ref):
    _, d, m, width = ref.shape
    if d == 1:
        return ref[0, 0]
    for r in range(d):
        for s in range(width // LANES):
            slab_ref[s, pl.ds(r, m, stride=d), :] = ref[0, r, :, s * LANES:(s + 1) * LANES]
    return jnp.concatenate([slab_ref[s] for s in range(width // LANES)], axis=-1)


def _out_kernel(x_ref, ga1_ref, sh2_ref, sc2_ref, ga2_ref, convn_ref,
                o1_ref, o2_ref, o3_ref, l1_ref, l2_ref, l3_ref,
                gna_ref, wout_ref, g2_ref, wg_ref, wu_ref, wd_ref, fg_ref, y_ref, slab_ref, *, ff_chunk):
    x = x_ref[0]
    hm_f = _head_masks(GROUP_DIM, F32)
    lm_f = _head_masks(LANES, F32)

    ls = [_load_positions(l_ref, slab_ref) for l_ref in (l1_ref, l2_ref, l3_ref)]
    m = jnp.maximum(jnp.maximum(ls[0], ls[1]), ls[2])
    es = [jnp.exp(l - m) for l in ls]
    tot = es[0] + es[1] + es[2]
    parts = []
    ssq = jnp.zeros((x.shape[0], 1), F32)
    for g, o_ref in enumerate((o1_ref, o2_ref, o3_ref)):
        alpha = es[g] / tot
        wide = jnp.zeros((x.shape[0], GROUP_DIM), F32)
        for h in range(GROUP_HEADS):
            a_h = jnp.sum(alpha * lm_f[h], axis=-1, keepdims=True)
            wide = wide + a_h * hm_f[h]
        ao = wide * _load_positions(o_ref, slab_ref)
        ssq = ssq + jnp.sum(ao * ao, axis=-1, keepdims=True)
        parts.append(ao)
    rinv = lax.rsqrt(ssq / ATTN_DIM + EPS)
    gna = gna_ref[...]
    mixed = jnp.concatenate(
        [convn_ref[0]] + [(parts[g] * rinv * gna[:, g * GROUP_DIM:(g + 1) * GROUP_DIM]).astype(BF16)
                          for g in range(N_DIL)], axis=-1)
    mix = jnp.dot(mixed, wout_ref[...], preferred_element_type=F32)
    x1 = x + ga1_ref[0] * mix

    h2 = (_rms(x1, g2_ref[...]) * (1.0 + sc2_ref[0]) + sh2_ref[0]).astype(BF16)
    ffn = jnp.zeros_like(x1)
    for c in range(D_FF // ff_chunk):
        sl = slice(c * ff_chunk, (c + 1) * ff_chunk)
        gate = jnp.dot(h2, wg_ref[:, sl], preferred_element_type=F32)
        up = jnp.dot(h2, wu_ref[:, sl], preferred_element_type=F32)
        act = (gate * jax.nn.sigmoid(gate) * up).astype(BF16)
        ffn = ffn + jnp.dot(act, wd_ref[sl, :], preferred_element_type=F32)
    x2 = x1 + ga2_ref[0] * ffn
    y_ref[0] = _rms(x2, fg_ref[...])


def _out(x, mod, convn, os_, ls_, gn_attn, w_out, g2, w_gate, w_up, w_down, final_g, tm=512, ff_chunk=1408):
    nb, R, _ = x.shape
    tm = min(tm, R)
    per_row = mod.shape[1] != 1
    mrows = tm if per_row else 1
    row = lambda b, i: (b, i, 0)

    def mod_spec(c):
        return pl.BlockSpec((1, mrows, D_MODEL), lambda b, i: (b, i if per_row else 0, c))

    def stream_spec(a):
        _, d, _, width = a.shape
        return pl.BlockSpec((1, d, tm // d, width), lambda b, i: (b, 0, i, 0))

    return pl.pallas_call(
        functools.partial(_out_kernel, ff_chunk=ff_chunk),
        grid=(nb, R // tm),
        in_specs=[pl.BlockSpec((1, tm, D_MODEL), row),
                  mod_spec(2), mod_spec(3), mod_spec(4), mod_spec(5),
                  pl.BlockSpec((1, tm, CONV_DIM), row)]
                 + [stream_spec(a) for a in os_] + [stream_spec(a) for a in ls_]
                 + [_const_spec((1, ATTN_DIM)), _const_spec((D_MODEL, D_MODEL)), _const_spec((1, D_MODEL)),
                    _const_spec((D_MODEL, D_FF)), _const_spec((D_MODEL, D_FF)), _const_spec((D_FF, D_MODEL)),
                    _const_spec((1, D_MODEL))],
        out_specs=pl.BlockSpec((1, tm, D_MODEL), row),
        out_shape=jax.ShapeDtypeStruct(x.shape, F32),
        scratch_shapes=[pltpu.VMEM((GROUP_DIM // LANES, tm, LANES), F32)],
        compiler_params=_cparams(("arbitrary", "arbitrary")),
        name="out",
    )(x, mod, mod, mod, mod, convn, *os_, *ls_, gn_attn, w_out, g2, w_gate, w_up, w_down, final_g)


def _buffer_to_channel_major(cache):
    B, L = cache.shape[:2]
    return cache.transpose(0, 2, 3, 4, 1).reshape(B, 2, GROUP_DIM, L)


def _buffer_from_channel_major(buf):
    B, _, _, L = buf.shape
    return buf.reshape(B, 2, GROUP_HEADS, HEAD_DIM, L).transpose(0, 4, 1, 2, 3)[None]


def kernel(x_prompt, x_sample, state_conv, cache_kv1, cache_kv2, cache_kv3, c_prompt, c_sample, w_ada, b_ada, norm1_g, norm2_g, w_in, conv_w, gn_conv, gn_attn, w_out, w_gate, w_up, w_down, rel_bias, final_g):
    B, S, _ = x_prompt.shape
    nb, t_len, _ = x_sample.shape
    rows_s = nb * t_len
    assert w_ada.shape[0] == 1 and LANES % t_len == 0 and 2 <= t_len <= 8 and rows_s % LANES == 0

    w_in_b = w_in[0].astype(BF16)
    w_out_b = w_out[0].astype(BF16)
    w_gate_b = w_gate[0].astype(BF16)
    w_up_b = w_up[0].astype(BF16)
    w_down_b = w_down[0].astype(BF16)
    final_g2 = final_g.reshape(1, D_MODEL)
    tables = [_group_table(rel_bias, g) for g in range(N_DIL)]

    c_all = jnp.concatenate([c_prompt, jnp.repeat(c_sample, t_len, axis=0)], axis=0)
    mod = _ada(c_all, w_ada[0], b_ada)
    mod_p = mod[:B].reshape(B, 1, 6 * D_MODEL)
    mod_s = mod[B:]

    (convn, q1, q2, q3, k1, k2, k3, v1, v2, v3,
     nconv_p, kv1_p, kv2_p, kv3_p) = _inproj_prompt(x_prompt, mod_p, norm1_g, w_in_b, conv_w[0], gn_conv)
    os_, ls_ = [], []
    for g, (qg, kg, vg) in enumerate(((q1, k1, v1), (q2, k2, v2), (q3, k3, v3))):
        _, d, M, _ = qg.shape
        merge = lambda a: a.reshape(B * d, M, GROUP_DIM)
        o, l = _attn_prompt(merge(qg), merge(kg), merge(vg), *_prompt_bias(tables[g]))
        os_.append(o.reshape(B, d, M, GROUP_DIM))
        ls_.append(l.reshape(B, d, M, LANES))
    y_prompt = _out(x_prompt, mod_p, convn, os_, ls_, gn_attn, w_out_b, norm2_g,
                    w_gate_b, w_up_b, w_down_b, final_g2)

    xs = x_sample.reshape(rows_s, D_MODEL)
    prev0 = jnp.repeat(state_conv[0, :, 0], t_len, axis=0)
    prev1 = jnp.repeat(state_conv[0, :, 1], t_len, axis=0)
    convn_s, u_s, q_s, k_s, v_s, kt_s, vt_s = _inproj_sample(
        xs, mod_s, norm1_g, w_in_b, conv_w[0], gn_conv, prev0, prev1, t_len)
    pad8 = lambda a: jnp.pad(a.reshape(nb, t_len, ATTN_DIM), ((0, 0), (0, 8 - t_len), (0, 0)))
    q8, k8, v8 = pad8(q_s), pad8(k_s), pad8(v_s)
    os_s, ls_s, new_bufs = [], [], []
    for g, (cache, (_, d)) in enumerate(zip((cache_kv1, cache_kv2, cache_kv3), DIL_PAIRS)):
        buf = _buffer_to_channel_major(cache[0])
        bc, bn = _sample_bias(tables[g], buf.shape[-1], d, t_len)
        o, l, nbuf = _attn_sample(q8, k8, v8, kt_s, vt_s, buf, bc, bn, g, t_len)
        os_s.append(o[:, :t_len].reshape(1, 1, rows_s, GROUP_DIM))
        ls_s.append(l[:, :t_len].reshape(1, 1, rows_s, LANES))
        new_bufs.append(_buffer_from_channel_major(nbuf))
    y_sample = _out(xs[None], mod_s[None], convn_s[None], os_s, ls_s, gn_attn, w_out_b, norm2_g,
                    w_gate_b, w_up_b, w_down_b, final_g2)
    y_sample = y_sample.reshape(nb, t_len, D_MODEL)
    nconv_s = u_s.reshape(nb, t_len, CONV_DIM)[:, t_len - 2:]

    return (y_prompt, y_sample, nconv_p[None],
            _buffer_from_channel_major(kv1_p), _buffer_from_channel_major(kv2_p),
            _buffer_from_channel_major(kv3_p),
            nconv_s[None], new_bufs[0], new_bufs[1], new_bufs[2])
```

```python
import functools

import numpy as np
import jax
import jax.numpy as jnp
from jax import lax
from jax.experimental import pallas as pl
from jax.experimental.pallas import tpu as pltpu

F32 = jnp.float32
BF16 = jnp.bfloat16

D_MODEL = 1024
HEAD_DIM = 64
CONV_DIM = 256
ATTN_DIM = 768
GROUP_HEADS = 4
GROUP_DIM = GROUP_HEADS * HEAD_DIM
DIL_PAIRS = ((128, 1), (512, 4), (2048, 16))
N_DIL = len(DIL_PAIRS)
QB = 128
N_BUCKETS = 32
MAX_DISTANCE = 2048
D_FF = 2816
PROJ_DIM = 3072
EPS = 1e-6
NEG_INF = -1e30
ATTN_SCALE = HEAD_DIM ** -0.5

LANES = 128
VMEM_LIMIT = 56 * 1024 * 1024


def _cparams(sem):
    return pltpu.CompilerParams(dimension_semantics=sem, vmem_limit_bytes=VMEM_LIMIT)


def _const_spec(shape):
    nd = len(shape)
    return pl.BlockSpec(shape, lambda *_: (0,) * nd, pipeline_mode=pl.Buffered(1))


def _head_masks(width, dtype):
    lane = lax.broadcasted_iota(jnp.int32, (1, width), 1)
    if width == GROUP_DIM:
        lane = lane // HEAD_DIM
    return [(lane == h).astype(dtype) for h in range(GROUP_HEADS)]


def _ada_kernel(c_ref, w_ref, b_ref, o_ref):
    c = c_ref[...]
    s = (c * jax.nn.sigmoid(c)).astype(BF16)
    o_ref[...] = jnp.dot(s, w_ref[...].astype(BF16), preferred_element_type=F32) + b_ref[...]


def _ada(c_all, w_ada, b_ada):
    rows = c_all.shape[0]
    tn = 1536
    return pl.pallas_call(
        _ada_kernel,
        grid=(6 * D_MODEL // tn,),
        in_specs=[pl.BlockSpec((rows, D_MODEL), lambda j: (0, 0)),
                  pl.BlockSpec((D_MODEL, tn), lambda j: (0, j)),
                  pl.BlockSpec((1, tn), lambda j: (0, j))],
        out_specs=pl.BlockSpec((rows, tn), lambda j: (0, j)),
        out_shape=jax.ShapeDtypeStruct((rows, 6 * D_MODEL), F32),
        compiler_params=_cparams(("arbitrary",)),
        name="ada",
    )(c_all, w_ada, b_ada)


def _rms(x, g):
    return x * lax.rsqrt(jnp.mean(x * x, axis=-1, keepdims=True) + EPS) * g


STREAM_SLABS = sum(GROUP_DIM // LANES for _, d in DIL_PAIRS if d > 1)
INPROJ_PARTS = 2


def _store_streams(p, out_refs, slab_ref, slab0, r0):
    rows = p.shape[0]
    for g, ((_, d), out_ref) in enumerate(zip(DIL_PAIRS, out_refs)):
        lo = g * GROUP_DIM
        if d == 1:
            out_ref[0, 0, r0:r0 + rows, :] = p[:, lo:lo + GROUP_DIM].astype(BF16)
            continue
        for s in range(GROUP_DIM // LANES):
            slab_ref[slab0 + s] = p[:, lo + s * LANES:lo + (s + 1) * LANES]
        for r in range(d):
            for s in range(GROUP_DIM // LANES):
                out_ref[0, r, r0 // d:(r0 + rows) // d, s * LANES:(s + 1) * LANES] = (
                    slab_ref[slab0 + s, pl.ds(r, rows // d, stride=d), :].astype(BF16))
        slab0 += GROUP_DIM // LANES


def _inproj_prompt_kernel(x_ref, sh_ref, sc_ref, g1_ref, w_ref, cw_ref, gnc_ref,
                          convn_ref, q1_ref, q2_ref, q3_ref, k1_ref, k2_ref, k3_ref,
                          v1_ref, v2_ref, v3_ref, nconv_ref, kv1_ref, kv2_ref, kv3_ref,
                          ubuf_ref, slab_ref, *, tm):
    i = pl.program_id(1)
    n = pl.num_programs(1)
    pr = tm // INPROJ_PARTS
    parts = range(INPROJ_PARTS)
    c0 = 3 * CONV_DIM
    cw = cw_ref[...]

    hbs = [(_rms(x_ref[0, a * pr:(a + 1) * pr, :], g1_ref[...]) * (1.0 + sc_ref[0]) + sh_ref[0]).astype(BF16)
           for a in parts]
    projs = [jnp.dot(hb, w_ref[...], preferred_element_type=F32) for hb in hbs]

    tail = ubuf_ref[tm:tm + 8, :]
    ubuf_ref[0:8, :] = jnp.where(i == 0, jnp.zeros_like(tail), tail)
    for a in parts:
        r0 = a * pr
        p = projs[a]
        gb = p[:, 0:CONV_DIM]
        u = p[:, CONV_DIM:2 * CONV_DIM] * p[:, 2 * CONV_DIM:c0]
        ubuf_ref[8 + r0:8 + r0 + pr, :] = u
        um1 = ubuf_ref[7 + r0:7 + r0 + pr, :]
        um2 = ubuf_ref[6 + r0:6 + r0 + pr, :]
        z = cw[0:1, :] * um2 + cw[1:2, :] * um1 + cw[2:3, :] * u
        convn_ref[0, r0:r0 + pr, :] = _rms(gb * z, gnc_ref[...]).astype(BF16)
        slab0 = a * 3 * STREAM_SLABS
        _store_streams(p[:, c0:c0 + ATTN_DIM] * ATTN_SCALE, (q1_ref, q2_ref, q3_ref), slab_ref, slab0, r0)
        _store_streams(p[:, c0 + ATTN_DIM:c0 + 2 * ATTN_DIM], (k1_ref, k2_ref, k3_ref), slab_ref,
                       slab0 + STREAM_SLABS, r0)
        _store_streams(p[:, c0 + 2 * ATTN_DIM:c0 + 3 * ATTN_DIM], (v1_ref, v2_ref, v3_ref), slab_ref,
                       slab0 + 2 * STREAM_SLABS, r0)
    nconv_ref[0] = ubuf_ref[tm + 6:tm + 8, :]

    def put(ref, g, rows):
        for a in parts:
            lo_r, hi_r = max(tm - rows, a * pr), (a + 1) * pr
            if hi_r <= lo_r:
                continue
            for kv in range(2):
                lo = c0 + (1 + kv) * ATTN_DIM + g * GROUP_DIM
                ref[0, kv, :, lo_r - (tm - rows):hi_r - (tm - rows)] = (
                    projs[a][lo_r - a * pr:hi_r - a * pr, lo:lo + GROUP_DIM].T)

    w3 = DIL_PAIRS[2][0]

    @pl.when(i >= n - w3 // tm)
    def _():
        put(kv3_ref, 2, tm)

    @pl.when(i == n - 1)
    def _():
        put(kv2_ref, 1, DIL_PAIRS[1][0])
        put(kv1_ref, 0, DIL_PAIRS[0][0])


def _inproj_prompt(x, mod, g1, w_in, conv_w, gn_conv, tm=1024):
    B, S, _ = x.shape
    n = S // tm
    w1, w2, w3 = (w for w, _ in DIL_PAIRS)
    assert S % tm == 0 and w3 % tm == 0 and tm >= w2 and tm % (INPROJ_PARTS * LANES) == 0
    n3 = w3 // tm
    row = lambda b, i: (b, i, 0)
    stream_specs = [pl.BlockSpec((1, d, tm // d, GROUP_DIM), lambda b, i: (b, 0, i, 0)) for _, d in DIL_PAIRS]
    stream_shapes = [jax.ShapeDtypeStruct((B, d, S // d, GROUP_DIM), BF16) for _, d in DIL_PAIRS]
    outs = pl.pallas_call(
        functools.partial(_inproj_prompt_kernel, tm=tm),
        grid=(B, n),
        in_specs=[pl.BlockSpec((1, tm, D_MODEL), row),
                  pl.BlockSpec((1, 1, D_MODEL), lambda b, i: (b, 0, 0)),
                  pl.BlockSpec((1, 1, D_MODEL), lambda b, i: (b, 0, 1)),
                  _const_spec((1, D_MODEL)),
                  _const_spec((D_MODEL, PROJ_DIM)),
                  _const_spec((3, CONV_DIM)),
                  _const_spec((1, CONV_DIM))],
        out_specs=[pl.BlockSpec((1, tm, CONV_DIM), row)] + stream_specs * 3 + [
                   pl.BlockSpec((1, 2, CONV_DIM), lambda b, i: (b, 0, 0)),
                   pl.BlockSpec((1, 2, GROUP_DIM, w1), lambda b, i: (b, 0, 0, 0)),
                   pl.BlockSpec((1, 2, GROUP_DIM, w2), lambda b, i: (b, 0, 0, 0)),
                   pl.BlockSpec((1, 2, GROUP_DIM, tm),
                                lambda b, i: (b, 0, 0, jnp.maximum(i - (n - n3), 0)))],
        out_shape=[jax.ShapeDtypeStruct((B, S, CONV_DIM), BF16)] + stream_shapes * 3 + [
                   jax.ShapeDtypeStruct((B, 2, CONV_DIM), F32),
                   jax.ShapeDtypeStruct((B, 2, GROUP_DIM, w1), F32),
                   jax.ShapeDtypeStruct((B, 2, GROUP_DIM, w2), F32),
                   jax.ShapeDtypeStruct((B, 2, GROUP_DIM, w3), F32)],
        scratch_shapes=[pltpu.VMEM((tm + 8, CONV_DIM), F32),
                        pltpu.VMEM((INPROJ_PARTS * 3 * STREAM_SLABS, tm // INPROJ_PARTS, LANES), F32)],
        compiler_params=_cparams(("arbitrary", "arbitrary")),
        name="inproj_prompt",
    )(x, mod, mod, g1, w_in, conv_w, gn_conv)
    return outs


def _inproj_sample_kernel(x_ref, sh_ref, sc_ref, g1_ref, w_ref, cw_ref, gnc_ref, p0_ref, p1_ref,
                          convn_ref, u_ref, q_ref, k_ref, v_ref, kt_ref, vt_ref, ubuf_ref, *, tm, t_len):
    h = _rms(x_ref[...], g1_ref[...]) * (1.0 + sc_ref[...]) + sh_ref[...]
    hb = h.astype(BF16)

    pc = jnp.dot(hb, w_ref[:, 0:3 * CONV_DIM], preferred_element_type=F32)
    gb = pc[:, 0:CONV_DIM]
    u = pc[:, CONV_DIM:2 * CONV_DIM] * pc[:, 2 * CONV_DIM:3 * CONV_DIM]
    u_ref[...] = u

    ubuf_ref[0:8, :] = jnp.zeros((8, CONV_DIM), F32)
    ubuf_ref[8:tm + 8, :] = u
    t = lax.broadcasted_iota(jnp.int32, (tm, 1), 0) % t_len
    um1 = jnp.where(t == 0, p1_ref[...], ubuf_ref[7:tm + 7, :])
    um2 = jnp.where(t == 0, p0_ref[...], jnp.where(t == 1, p1_ref[...], ubuf_ref[6:tm + 6, :]))
    cw = cw_ref[...]
    z = cw[0:1, :] * um2 + cw[1:2, :] * um1 + cw[2:3, :] * u
    convn_ref[...] = _rms(gb * z, gnc_ref[...]).astype(BF16)

    c0 = 3 * CONV_DIM
    pq = jnp.dot(hb, w_ref[:, c0:c0 + ATTN_DIM], preferred_element_type=F32)
    q_ref[...] = pq * ATTN_SCALE
    pk = jnp.dot(hb, w_ref[:, c0 + ATTN_DIM:c0 + 2 * ATTN_DIM], preferred_element_type=F32)
    k_ref[...] = pk
    kt_ref[...] = pk.T
    pv = jnp.dot(hb, w_ref[:, c0 + 2 * ATTN_DIM:c0 + 3 * ATTN_DIM], preferred_element_type=F32)
    v_ref[...] = pv
    vt_ref[...] = pv.T


def _inproj_sample(x, mod, g1, w_in, conv_w, gn_conv, prev0, prev1, t_len):
    tm = x.shape[0]
    full = lambda shape: pl.BlockSpec(shape, lambda i: (0,) * len(shape))
    return pl.pallas_call(
        functools.partial(_inproj_sample_kernel, tm=tm, t_len=t_len),
        grid=(1,),
        in_specs=[full((tm, D_MODEL)),
                  pl.BlockSpec((tm, D_MODEL), lambda i: (0, 0)),
                  pl.BlockSpec((tm, D_MODEL), lambda i: (0, 1)),
                  full((1, D_MODEL)), full((D_MODEL, PROJ_DIM)), full((3, CONV_DIM)), full((1, CONV_DIM)),
                  full((tm, CONV_DIM)), full((tm, CONV_DIM))],
        out_specs=[full((tm, CONV_DIM)), full((tm, CONV_DIM)),
                   full((tm, ATTN_DIM)), full((tm, ATTN_DIM)), full((tm, ATTN_DIM)),
                   full((ATTN_DIM, tm)), full((ATTN_DIM, tm))],
        out_shape=[jax.ShapeDtypeStruct((tm, CONV_DIM), BF16),
                   jax.ShapeDtypeStruct((tm, CONV_DIM), F32),
                   jax.ShapeDtypeStruct((tm, ATTN_DIM), F32),
                   jax.ShapeDtypeStruct((tm, ATTN_DIM), F32),
                   jax.ShapeDtypeStruct((tm, ATTN_DIM), F32),
                   jax.ShapeDtypeStruct((ATTN_DIM, tm), F32),
                   jax.ShapeDtypeStruct((ATTN_DIM, tm), F32)],
        scratch_shapes=[pltpu.VMEM((tm + 8, CONV_DIM), F32)],
        compiler_params=_cparams(("arbitrary",)),
        name="inproj_sample",
    )(x, mod, mod, g1, w_in, conv_w, gn_conv, prev0, prev1)


def _t5_buckets(dist):
    dist = np.asarray(dist, np.int32)
    max_exact = N_BUCKETS // 2
    large = max_exact + (np.log(np.maximum(dist, 1).astype(np.float32) / max_exact)
                         / np.log(MAX_DISTANCE / max_exact) * (N_BUCKETS - max_exact)).astype(np.int32)
    large = np.minimum(large, N_BUCKETS - 1)
    return np.where(dist < max_exact, dist, large).astype(np.int32)


def _group_table(rel_bias, g):
    w, d = DIL_PAIRS[g]
    buckets = _t5_buckets(np.arange(w // d + 1) * d)
    return rel_bias[buckets][:, g * GROUP_HEADS:(g + 1) * GROUP_HEADS].T.astype(F32)


def _prompt_bias(table):
    period = 3 * QB - 1
    base = jnp.concatenate([table[:, ::-1], jnp.full((GROUP_HEADS, period - QB - 1), NEG_INF, F32)], axis=1)
    skew = jnp.tile(base, (1, QB))[:, :QB * (period - 1)].reshape(GROUP_HEADS, QB, period - 1)
    prev, cur = skew[:, :, 0:QB], skew[:, :, QB:2 * QB]
    tri = (np.arange(QB)[None, :] <= np.arange(QB)[:, None])[None]
    fold = jnp.stack([jnp.where(tri, cur, prev), jnp.where(tri, cur, NEG_INF)])
    fold = fold.transpose(0, 3, 1, 2).reshape(2, QB, GROUP_HEADS * QB)
    lone = jnp.broadcast_to(table[:, QB:QB + 1], (GROUP_HEADS, QB)).reshape(1, 1, GROUP_HEADS * QB)
    return fold, jnp.concatenate([lone, jnp.full_like(lone, NEG_INF)], axis=0)


def _sample_bias(table, buf_len, d, t_len):
    assert buf_len == QB * d
    asc = table[:, ::-1]
    inter = jnp.concatenate([asc[:, :, None], jnp.full((GROUP_HEADS, QB + 1, d - 1), NEG_INF, F32)],
                            axis=2).reshape(GROUP_HEADS, (QB + 1) * d)
    ext_len = buf_len + LANES
    rows = []
    for t in range(8):
        lead = t if t < t_len else 0
        rows.append(jnp.pad(inter, ((0, 0), (lead, ext_len - lead - (QB + 1) * d)), constant_values=NEG_INF))
    ext = jnp.stack(rows, axis=1)
    return (ext[:, :, :buf_len].reshape(GROUP_HEADS * 8, buf_len),
            ext[:, :, buf_len:].reshape(GROUP_HEADS * 8, LANES))


def _attn_prompt_kernel(q_ref, kc_ref, vc_ref, kp_ref, vp_ref, bias_ref, dbias_ref, o_ref, lse_ref):
    i = pl.program_id(1)
    sb, tq, _ = q_ref.shape
    hm_b = _head_masks(GROUP_DIM, BF16)
    first = jnp.where(i == 0, 1, 0)
    c_idx = lax.broadcasted_iota(jnp.int32, (QB, GROUP_HEADS * QB), 0)
    q_idx = lax.broadcasted_iota(jnp.int32, (QB, GROUP_HEADS * QB), 1) & (QB - 1)
    tri = c_idx <= q_idx
    tri_f = tri.astype(F32)
    diag_f = (c_idx == q_idx).astype(F32)
    work = [(s, j) for s in range(sb) for j in range(tq // QB)]

    def scores(s, j):
        q = q_ref[s, j * QB:(j + 1) * QB, :]
        q4 = jnp.concatenate([q * hm_b[h] for h in range(GROUP_HEADS)], axis=0)
        if j == 0:
            kk = jnp.concatenate([kp_ref[s], kc_ref[s, 0:QB, :]], axis=0)
        else:
            kk = kc_ref[s, (j - 1) * QB:(j + 1) * QB, :]
        return lax.dot_general(kk, q4, (((1,), (1,)), ((), ())), preferred_element_type=F32)

    def values_t(s, j):
        if j == 0:
            return jnp.concatenate([vp_ref[s].T, vc_ref[s, 0:QB, :].T], axis=1)
        return vc_ref[s, (j - 1) * QB:(j + 1) * QB, :].T

    st_next = scores(*work[0])
    for n, (s, j) in enumerate(work):
        st = st_next
        if n + 1 < len(work):
            st_next = scores(*work[n + 1])
        slot = first if j == 0 else 0
        sp = st[0:QB]
        sc = st[QB:2 * QB]
        w = jnp.where(tri, sc, sp) + bias_ref[slot]
        sd = jnp.sum(sp * diag_f, axis=0, keepdims=True) + dbias_ref[slot]
        m = jnp.maximum(jnp.max(w, axis=0, keepdims=True), sd)
        e = jnp.exp(w - m)
        ed = jnp.exp(sd - m)
        den = jnp.sum(e, axis=0, keepdims=True) + ed
        lse = m + jnp.log(den)
        rden = 1.0 / den
        p_cur = e * tri_f
        pt = jnp.concatenate([e - p_cur, p_cur], axis=0).astype(BF16)
        ed_r = ed.astype(BF16).astype(F32)
        vt = values_t(s, j)
        ots = []
        for h in range(GROUP_HEADS):
            cols = slice(h * QB, (h + 1) * QB)
            rows = slice(h * HEAD_DIM, (h + 1) * HEAD_DIM)
            ot = jnp.dot(vt[rows, :], pt[:, cols], preferred_element_type=F32)
            ot = ot + vt[rows, 0:QB].astype(F32) * ed_r[:, cols]
            ots.append(ot * rden[:, cols])
        o_ref[s, j * QB:(j + 1) * QB, :] = jnp.concatenate(ots, axis=0).T
        lrows = jnp.concatenate([lse[:, h * QB:(h + 1) * QB] for h in range(GROUP_HEADS)]
                                + [jnp.zeros((LANES - GROUP_HEADS, QB), F32)], axis=0)
        lse_ref[s, j * QB:(j + 1) * QB, :] = lrows.T


def _attn_prompt(q, k, v, bias, dbias, block_rows=2048):
    ns, M, _ = q.shape
    tq = min(block_rows, M)
    sb = block_rows // tq
    r = tq // QB
    cur = pl.BlockSpec((sb, tq, GROUP_DIM), lambda s, i: (s, i, 0))
    prev = pl.BlockSpec((sb, QB, GROUP_DIM), lambda s, i: (s, jnp.maximum(i * r - 1, 0), 0))
    return pl.pallas_call(
        _attn_prompt_kernel,
        grid=(ns // sb, M // tq),
        in_specs=[cur, cur, cur, prev, prev, _const_spec(bias.shape), _const_spec(dbias.shape)],
        out_specs=[pl.BlockSpec((sb, tq, GROUP_DIM), lambda s, i: (s, i, 0)),
                   pl.BlockSpec((sb, tq, LANES), lambda s, i: (s, i, 0))],
        out_shape=[jax.ShapeDtypeStruct((ns, M, GROUP_DIM), F32),
                   jax.ShapeDtypeStruct((ns, M, LANES), F32)],
        compiler_params=_cparams(("arbitrary", "arbitrary")),
        name="attn_prompt",
    )(q, k, v, k, v, bias, dbias)


def _attn_sample_kernel(q_ref, k8_ref, v8_ref, kt_ref, vt_ref, cache_ref, bc_ref, bn_ref,
                        o_ref, lse_ref, newc_ref, *, buf_len, t_len):
    b = pl.program_id(0)
    hm_f = _head_masks(GROUP_DIM, F32)
    lm_f = _head_masks(LANES, F32)
    lane = lax.broadcasted_iota(jnp.int32, (1, LANES), 1)

    q8 = q_ref[0]
    q4b = jnp.concatenate([q8 * hm_f[h] for h in range(GROUP_HEADS)], axis=0).astype(BF16)
    kt = cache_ref[0, 0]
    vt = cache_ref[0, 1]
    s_c = jnp.dot(q4b, kt.astype(BF16), preferred_element_type=F32) + bc_ref[...]

    q4r = q4b.astype(F32)
    k8r = k8_ref[0].astype(BF16).astype(F32)
    v8r = v8_ref[0].astype(BF16).astype(F32)
    s_n = bn_ref[...]
    for t in range(t_len):
        col = jnp.sum(q4r * k8r[t:t + 1, :], axis=-1, keepdims=True)
        s_n = s_n + jnp.where(lane == t, col, 0.0)

    m = jnp.maximum(jnp.max(s_c, axis=-1, keepdims=True), jnp.max(s_n, axis=-1, keepdims=True))
    e_c = jnp.exp(s_c - m)
    e_n = jnp.exp(s_n - m)
    den = jnp.sum(e_c, axis=-1, keepdims=True) + jnp.sum(e_n, axis=-1, keepdims=True)
    pv = lax.dot_general(e_c.astype(BF16), vt.astype(BF16), (((1,), (1,)), ((), ())),
                         preferred_element_type=F32)
    e_nr = e_n.astype(BF16).astype(F32)
    for t in range(t_len):
        w_t = jnp.sum(jnp.where(lane == t, e_nr, 0.0), axis=-1, keepdims=True)
        pv = pv + w_t * v8r[t:t + 1, :]
    pv = pv / den
    lse = m + jnp.log(den)
    o = jnp.zeros((8, GROUP_DIM), F32)
    l = jnp.zeros((8, LANES), F32)
    for h in range(GROUP_HEADS):
        o = o + pv[h * 8:(h + 1) * 8, :] * hm_f[h]
        l = l + lse[h * 8:(h + 1) * 8, :] * lm_f[h]
    o_ref[0] = o
    lse_ref[0] = l

    keep = LANES - t_len
    per_block = LANES // t_len
    shift_new = keep - (b % per_block) * t_len
    for kv, new_ref in ((0, kt_ref), (1, vt_ref)):
        new_r = pltpu.roll(new_ref[...], shift_new, 1)
        prev_r = None
        for c in range(buf_len // LANES):
            r_c = pltpu.roll(cache_ref[0, kv, :, c * LANES:(c + 1) * LANES], keep, 1)
            if c > 0:
                newc_ref[0, kv, :, (c - 1) * LANES:c * LANES] = jnp.where(lane < keep, prev_r, r_c)
            prev_r = r_c
        newc_ref[0, kv, :, buf_len - LANES:buf_len] = jnp.where(lane < keep, prev_r, new_r)


def _attn_sample(q8, k8, v8, kt, vt, cache, bc, bn, g, t_len):
    nb, _, _, buf_len = cache.shape
    per_block = LANES // t_len
    new_spec = pl.BlockSpec((1, 8, GROUP_DIM), lambda b: (b, 0, g))
    newt_spec = pl.BlockSpec((GROUP_DIM, LANES), lambda b: (g, b // per_block))
    cache_spec = pl.BlockSpec((1, 2, GROUP_DIM, buf_len), lambda b: (b, 0, 0, 0))
    return pl.pallas_call(
        functools.partial(_attn_sample_kernel, buf_len=buf_len, t_len=t_len),
        grid=(nb,),
        in_specs=[new_spec, new_spec, new_spec, newt_spec, newt_spec, cache_spec,
                  _const_spec(bc.shape), _const_spec(bn.shape)],
        out_specs=[pl.BlockSpec((1, 8, GROUP_DIM), lambda b: (b, 0, 0)),
                   pl.BlockSpec((1, 8, LANES), lambda b: (b, 0, 0)),
                   cache_spec],
        out_shape=[jax.ShapeDtypeStruct((nb, 8, GROUP_DIM), F32),
                   jax.ShapeDtypeStruct((nb, 8, LANES), F32),
                   jax.ShapeDtypeStruct(cache.shape, F32)],
        compiler_params=_cparams(("arbitrary",)),
        name="attn_sample",
    )(q8, k8, v8, kt, vt, cache, bc, bn)


OUT_PARTS = 2
MERGE_SLABS = N_DIL * (GROUP_DIM + LANES) // LANES


def _load_positions(ref, slab_ref, slab0, r0, rows):
    _, d, _, width = ref.shape
    if d == 1:
        return ref[0, 0, r0:r0 + rows, :]
    for r in range(d):
        for s in range(width // LANES):
            slab_ref[slab0 + s, pl.ds(r, rows // d, stride=d), :] = (
                ref[0, r, r0 // d:(r0 + rows) // d, s * LANES:(s + 1) * LANES])
    return jnp.concatenate([slab_ref[slab0 + s] for s in range(width // LANES)], axis=-1)


def _out_kernel(x_ref, ga1_ref, sh2_ref, sc2_ref, ga2_ref, convn_ref,
                o1_ref, o2_ref, o3_ref, l1_ref, l2_ref, l3_ref,
                gna_ref, wout_ref, g2_ref, wg_ref, wu_ref, wd_ref, fg_ref, y_ref, slab_ref):
    tm = x_ref.shape[1]
    pr = tm // OUT_PARTS
    parts = range(OUT_PARTS)
    hm_f = _head_masks(GROUP_DIM, F32)
    lm_f = _head_masks(LANES, F32)
    gna = gna_ref[...]

    def mod_rows(ref, a):
        return ref[0] if ref.shape[1] == 1 else ref[0, a * pr:(a + 1) * pr, :]

    def merge(a):
        r0 = a * pr
        slab0 = a * MERGE_SLABS
        ls = [_load_positions(l_ref, slab_ref, slab0 + g, r0, pr)
              for g, l_ref in enumerate((l1_ref, l2_ref, l3_ref))]
        m = jnp.maximum(jnp.maximum(ls[0], ls[1]), ls[2])
        es = [jnp.exp(l - m) for l in ls]
        tot = es[0] + es[1] + es[2]
        aos = []
        ssq = jnp.zeros((pr, 1), F32)
        for g, o_ref in enumerate((o1_ref, o2_ref, o3_ref)):
            alpha = es[g] / tot
            wide = jnp.zeros((pr, GROUP_DIM), F32)
            for h in range(GROUP_HEADS):
                a_h = jnp.sum(alpha * lm_f[h], axis=-1, keepdims=True)
                wide = wide + a_h * hm_f[h]
            ao = wide * _load_positions(o_ref, slab_ref, slab0 + N_DIL + g * (GROUP_DIM // LANES), r0, pr)
            ssq = ssq + jnp.sum(ao * ao, axis=-1, keepdims=True)
            aos.append(ao)
        rinv = lax.rsqrt(ssq / ATTN_DIM + EPS)
        return jnp.concatenate(
            [convn_ref[0, r0:r0 + pr, :]]
            + [(aos[g] * rinv * gna[:, g * GROUP_DIM:(g + 1) * GROUP_DIM]).astype(BF16) for g in range(N_DIL)],
            axis=-1)

    def out_proj(a, mixed):
        mix = jnp.dot(mixed, wout_ref[...], preferred_element_type=F32)
        x1 = x_ref[0, a * pr:(a + 1) * pr, :] + mod_rows(ga1_ref, a) * mix
        h2 = (_rms(x1, g2_ref[...]) * (1.0 + mod_rows(sc2_ref, a)) + mod_rows(sh2_ref, a)).astype(BF16)
        return x1, h2

    def swiglu(h2):
        gate = jnp.dot(h2, wg_ref[...], preferred_element_type=F32)
        up = jnp.dot(h2, wu_ref[...], preferred_element_type=F32)
        act = (gate * jax.nn.sigmoid(gate) * up).astype(BF16)
        return jnp.dot(act, wd_ref[...], preferred_element_type=F32)

    mixed = [merge(a) for a in parts]
    x1h2 = [out_proj(a, mixed[a]) for a in parts]
    ffn = [swiglu(h2) for _, h2 in x1h2]
    for a in parts:
        x2 = x1h2[a][0] + mod_rows(ga2_ref, a) * ffn[a]
        y_ref[0, a * pr:(a + 1) * pr, :] = _rms(x2, fg_ref[...])


def _out(x, mod, convn, os_, ls_, gn_attn, w_out, g2, w_gate, w_up, w_down, final_g, tm=512):
    nb, R, _ = x.shape
    tm = min(tm, R)
    per_row = mod.shape[1] != 1
    mrows = tm if per_row else 1
    row = lambda b, i: (b, i, 0)

    def mod_spec(c):
        return pl.BlockSpec((1, mrows, D_MODEL), lambda b, i: (b, i if per_row else 0, c))

    def stream_spec(a):
        _, d, _, width = a.shape
        return pl.BlockSpec((1, d, tm // d, width), lambda b, i: (b, 0, i, 0))

    return pl.pallas_call(
        _out_kernel,
        grid=(nb, R // tm),
        in_specs=[pl.BlockSpec((1, tm, D_MODEL), row),
                  mod_spec(2), mod_spec(3), mod_spec(4), mod_spec(5),
                  pl.BlockSpec((1, tm, CONV_DIM), row)]
                 + [stream_spec(a) for a in os_] + [stream_spec(a) for a in ls_]
                 + [_const_spec((1, ATTN_DIM)), _const_spec((D_MODEL, D_MODEL)), _const_spec((1, D_MODEL)),
                    _const_spec((D_MODEL, D_FF)), _const_spec((D_MODEL, D_FF)), _const_spec((D_FF, D_MODEL)),
                    _const_spec((1, D_MODEL))],
        out_specs=pl.BlockSpec((1, tm, D_MODEL), row),
        out_shape=jax.ShapeDtypeStruct(x.shape, F32),
        scratch_shapes=[pltpu.VMEM((OUT_PARTS * MERGE_SLABS, tm // OUT_PARTS, LANES), F32)],
        compiler_params=_cparams(("arbitrary", "arbitrary")),
        name="out",
    )(x, mod, mod, mod, mod, convn, *os_, *ls_, gn_attn, w_out, g2, w_gate, w_up, w_down, final_g)


def _buffer_to_channel_major(cache):
    B, L = cache.shape[:2]
    return cache.transpose(0, 2, 3, 4, 1).reshape(B, 2, GROUP_DIM, L)


def _buffer_from_channel_major(buf):
    B, _, _, L = buf.shape
    return buf.reshape(B, 2, GROUP_HEADS, HEAD_DIM, L).transpose(0, 4, 1, 2, 3)[None]


def kernel(x_prompt, x_sample, state_conv, cache_kv1, cache_kv2, cache_kv3, c_prompt, c_sample, w_ada, b_ada, norm1_g, norm2_g, w_in, conv_w, gn_conv, gn_attn, w_out, w_gate, w_up, w_down, rel_bias, final_g):
    B, S, _ = x_prompt.shape
    nb, t_len, _ = x_sample.shape
    rows_s = nb * t_len
    assert w_ada.shape[0] == 1 and LANES % t_len == 0 and 2 <= t_len <= 8 and rows_s % LANES == 0

    w_in_b = w_in[0].astype(BF16)
    w_out_b = w_out[0].astype(BF16)
    w_gate_b = w_gate[0].astype(BF16)
    w_up_b = w_up[0].astype(BF16)
    w_down_b = w_down[0].astype(BF16)
    final_g2 = final_g.reshape(1, D_MODEL)
    tables = [_group_table(rel_bias, g) for g in range(N_DIL)]

    c_all = jnp.concatenate([c_prompt, jnp.repeat(c_sample, t_len, axis=0)], axis=0)
    mod = _ada(c_all, w_ada[0], b_ada)
    mod_p = mod[:B].reshape(B, 1, 6 * D_MODEL)
    mod_s = mod[B:]

    (convn, q1, q2, q3, k1, k2, k3, v1, v2, v3,
     nconv_p, kv1_p, kv2_p, kv3_p) = _inproj_prompt(x_prompt, mod_p, norm1_g, w_in_b, conv_w[0], gn_conv)
    os_, ls_ = [], []
    for g, (qg, kg, vg) in enumerate(((q1, k1, v1), (q2, k2, v2), (q3, k3, v3))):
        _, d, M, _ = qg.shape
        merge = lambda a: a.reshape(B * d, M, GROUP_DIM)
        o, l = _attn_prompt(merge(qg), merge(kg), merge(vg), *_prompt_bias(tables[g]))
        os_.append(o.reshape(B, d, M, GROUP_DIM))
        ls_.append(l.reshape(B, d, M, LANES))
    y_prompt = _out(x_prompt, mod_p, convn, os_, ls_, gn_attn, w_out_b, norm2_g,
                    w_gate_b, w_up_b, w_down_b, final_g2)

    xs = x_sample.reshape(rows_s, D_MODEL)
    prev0 = jnp.repeat(state_conv[0, :, 0], t_len, axis=0)
    prev1 = jnp.repeat(state_conv[0, :, 1], t_len, axis=0)
    convn_s, u_s, q_s, k_s, v_s, kt_s, vt_s = _inproj_sample(
        xs, mod_s, norm1_g, w_in_b, conv_w[0], gn_conv, prev0, prev1, t_len)
    pad8 = lambda a: jnp.pad(a.reshape(nb, t_len, ATTN_DIM), ((0, 0), (0, 8 - t_len), (0, 0)))
    q8, k8, v8 = pad8(q_s), pad8(k_s), pad8(v_s)
    os_s, ls_s, new_bufs = [], [], []
    for g, (cache, (_, d)) in enumerate(zip((cache_kv1, cache_kv2, cache_kv3), DIL_PAIRS)):
        buf = _buffer_to_channel_major(cache[0])
        bc, bn = _sample_bias(tables[g], buf.shape[-1], d, t_len)
        o, l, nbuf = _attn_sample(q8, k8, v8, kt_s, vt_s, buf, bc, bn, g, t_len)
        os_s.append(o[:, :t_len].reshape(1, 1, rows_s, GROUP_DIM))
        ls_s.append(l[:, :t_len].reshape(1, 1, rows_s, LANES))
        new_bufs.append(_buffer_from_channel_major(nbuf))
    y_sample = _out(xs[None], mod_s[None], convn_s[None], os_s, ls_s, gn_attn, w_out_b, norm2_g,
                    w_gate_b, w_up_b, w_down_b, final_g2)
    y_sample = y_sample.reshape(nb, t_len, D_MODEL)
    nconv_s = u_s.reshape(nb, t_len, CONV_DIM)[:, t_len - 2:]

    return (y_prompt, y_sample, nconv_p[None],
            _buffer_from_channel_major(kv1_p), _buffer_from_channel_major(kv2_p),
            _buffer_from_channel_major(kv3_p),
            nconv_s[None], new_bufs[0], new_bufs[1], new_bufs[2])
```

```python
import functools

import numpy as np
import jax
import jax.numpy as jnp
from jax import lax
from jax.experimental import pallas as pl
from jax.experimental.pallas import tpu as pltpu

F32 = jnp.float32
BF16 = jnp.bfloat16

D_MODEL = 1024
HEAD_DIM = 64
CONV_DIM = 256
ATTN_DIM = 768
GROUP_HEADS = 4
GROUP_DIM = GROUP_HEADS * HEAD_DIM
DIL_PAIRS = ((128, 1), (512, 4), (2048, 16))
N_DIL = len(DIL_PAIRS)
QB = 128
N_BUCKETS = 32
MAX_DISTANCE = 2048
D_FF = 2816
PROJ_DIM = 3072
EPS = 1e-6
NEG_INF = -1e30
ATTN_SCALE = HEAD_DIM ** -0.5

LANES = 128
VMEM_LIMIT = 56 * 1024 * 1024


def _cparams(sem):
    return pltpu.CompilerParams(dimension_semantics=sem, vmem_limit_bytes=VMEM_LIMIT)


def _const_spec(shape):
    nd = len(shape)
    return pl.BlockSpec(shape, lambda *_: (0,) * nd, pipeline_mode=pl.Buffered(1))


def _head_masks(width, dtype):
    lane = lax.broadcasted_iota(jnp.int32, (1, width), 1)
    if width == GROUP_DIM:
        lane = lane // HEAD_DIM
    return [(lane == h).astype(dtype) for h in range(GROUP_HEADS)]


def _ada_kernel(c_ref, w_ref, b_ref, o_ref):
    c = c_ref[...]
    s = (c * jax.nn.sigmoid(c)).astype(BF16)
    o_ref[...] = jnp.dot(s, w_ref[...].astype(BF16), preferred_element_type=F32) + b_ref[...]


def _ada(c_all, w_ada, b_ada):
    rows = c_all.shape[0]
    tn = 1536
    return pl.pallas_call(
        _ada_kernel,
        grid=(6 * D_MODEL // tn,),
        in_specs=[pl.BlockSpec((rows, D_MODEL), lambda j: (0, 0)),
                  pl.BlockSpec((D_MODEL, tn), lambda j: (0, j)),
                  pl.BlockSpec((1, tn), lambda j: (0, j))],
        out_specs=pl.BlockSpec((rows, tn), lambda j: (0, j)),
        out_shape=jax.ShapeDtypeStruct((rows, 6 * D_MODEL), F32),
        compiler_params=_cparams(("arbitrary",)),
        name="ada",
    )(c_all, w_ada, b_ada)


def _rms(x, g):
    return x * lax.rsqrt(jnp.mean(x * x, axis=-1, keepdims=True) + EPS) * g


STREAM_SLABS = sum(GROUP_DIM // LANES for _, d in DIL_PAIRS if d > 1)
INPROJ_PARTS = 2


def _store_streams(p, out_refs, slab_ref, slab0, r0):
    rows = p.shape[0]
    for g, ((_, d), out_ref) in enumerate(zip(DIL_PAIRS, out_refs)):
        lo = g * GROUP_DIM
        if d == 1:
            out_ref[0, 0, r0:r0 + rows, :] = p[:, lo:lo + GROUP_DIM].astype(BF16)
            continue
        for s in range(GROUP_DIM // LANES):
            slab_ref[slab0 + s] = p[:, lo + s * LANES:lo + (s + 1) * LANES]
        for r in range(d):
            for s in range(GROUP_DIM // LANES):
                out_ref[0, r, r0 // d:(r0 + rows) // d, s * LANES:(s + 1) * LANES] = (
                    slab_ref[slab0 + s, pl.ds(r, rows // d, stride=d), :].astype(BF16))
        slab0 += GROUP_DIM // LANES


def _inproj_prompt_kernel(x_ref, sh_ref, sc_ref, g1_ref, w_ref, cw_ref, gnc_ref,
                          convn_ref, q1_ref, q2_ref, q3_ref, k1_ref, k2_ref, k3_ref,
                          v1_ref, v2_ref, v3_ref, nconv_ref, kv1_ref, kv2_ref, kv3_ref,
                          ubuf_ref, slab_ref, *, tm):
    i = pl.program_id(1)
    n = pl.num_programs(1)
    pr = tm // INPROJ_PARTS
    parts = range(INPROJ_PARTS)
    c0 = 3 * CONV_DIM
    cw = cw_ref[...]

    hbs = [(_rms(x_ref[0, a * pr:(a + 1) * pr, :], g1_ref[...]) * (1.0 + sc_ref[0]) + sh_ref[0]).astype(BF16)
           for a in parts]
    projs = [jnp.dot(hb, w_ref[...], preferred_element_type=F32) for hb in hbs]

    tail = ubuf_ref[tm:tm + 8, :]
    ubuf_ref[0:8, :] = jnp.where(i == 0, jnp.zeros_like(tail), tail)
    for a in parts:
        r0 = a * pr
        p = projs[a]
        gb = p[:, 0:CONV_DIM]
        u = p[:, CONV_DIM:2 * CONV_DIM] * p[:, 2 * CONV_DIM:c0]
        ubuf_ref[8 + r0:8 + r0 + pr, :] = u
        um1 = ubuf_ref[7 + r0:7 + r0 + pr, :]
        um2 = ubuf_ref[6 + r0:6 + r0 + pr, :]
        z = cw[0:1, :] * um2 + cw[1:2, :] * um1 + cw[2:3, :] * u
        convn_ref[0, r0:r0 + pr, :] = _rms(gb * z, gnc_ref[...]).astype(BF16)
        slab0 = a * 3 * STREAM_SLABS
        _store_streams(p[:, c0:c0 + ATTN_DIM] * ATTN_SCALE, (q1_ref, q2_ref, q3_ref), slab_ref, slab0, r0)
        _store_streams(p[:, c0 + ATTN_DIM:c0 + 2 * ATTN_DIM], (k1_ref, k2_ref, k3_ref), slab_ref,
                       slab0 + STREAM_SLABS, r0)
        _store_streams(p[:, c0 + 2 * ATTN_DIM:c0 + 3 * ATTN_DIM], (v1_ref, v2_ref, v3_ref), slab_ref,
                       slab0 + 2 * STREAM_SLABS, r0)
    nconv_ref[0] = ubuf_ref[tm + 6:tm + 8, :]

    def put(ref, g, rows):
        for a in parts:
            lo_r, hi_r = max(tm - rows, a * pr), (a + 1) * pr
            if hi_r <= lo_r:
                continue
            for kv in range(2):
                lo = c0 + (1 + kv) * ATTN_DIM + g * GROUP_DIM
                ref[0, kv, :, lo_r - (tm - rows):hi_r - (tm - rows)] = (
                    projs[a][lo_r - a * pr:hi_r - a * pr, lo:lo + GROUP_DIM].T)

    w3 = DIL_PAIRS[2][0]

    @pl.when(i >= n - w3 // tm)
    def _():
        put(kv3_ref, 2, tm)

    @pl.when(i == n - 1)
    def _():
        put(kv2_ref, 1, DIL_PAIRS[1][0])
        put(kv1_ref, 0, DIL_PAIRS[0][0])


def _inproj_prompt(x, mod, g1, w_in, conv_w, gn_conv, tm=1024):
    B, S, _ = x.shape
    n = S // tm
    w1, w2, w3 = (w for w, _ in DIL_PAIRS)
    assert S % tm == 0 and w3 % tm == 0 and tm >= w2 and tm % (INPROJ_PARTS * LANES) == 0
    n3 = w3 // tm
    row = lambda b, i: (b, i, 0)
    stream_specs = [pl.BlockSpec((1, d, tm // d, GROUP_DIM), lambda b, i: (b, 0, i, 0)) for _, d in DIL_PAIRS]
    stream_shapes = [jax.ShapeDtypeStruct((B, d, S // d, GROUP_DIM), BF16) for _, d in DIL_PAIRS]
    outs = pl.pallas_call(
        functools.partial(_inproj_prompt_kernel, tm=tm),
        grid=(B, n),
        in_specs=[pl.BlockSpec((1, tm, D_MODEL), row),
                  pl.BlockSpec((1, 1, D_MODEL), lambda b, i: (b, 0, 0)),
                  pl.BlockSpec((1, 1, D_MODEL), lambda b, i: (b, 0, 1)),
                  _const_spec((1, D_MODEL)),
                  _const_spec((D_MODEL, PROJ_DIM)),
                  _const_spec((3, CONV_DIM)),
                  _const_spec((1, CONV_DIM))],
        out_specs=[pl.BlockSpec((1, tm, CONV_DIM), row)] + stream_specs * 3 + [
                   pl.BlockSpec((1, 2, CONV_DIM), lambda b, i: (b, 0, 0)),
                   pl.BlockSpec((1, 2, GROUP_DIM, w1), lambda b, i: (b, 0, 0, 0)),
                   pl.BlockSpec((1, 2, GROUP_DIM, w2), lambda b, i: (b, 0, 0, 0)),
                   pl.BlockSpec((1, 2, GROUP_DIM, tm),
                                lambda b, i: (b, 0, 0, jnp.maximum(i - (n - n3), 0)))],
        out_shape=[jax.ShapeDtypeStruct((B, S, CONV_DIM), BF16)] + stream_shapes * 3 + [
                   jax.ShapeDtypeStruct((B, 2, CONV_DIM), F32),
                   jax.ShapeDtypeStruct((B, 2, GROUP_DIM, w1), F32),
                   jax.ShapeDtypeStruct((B, 2, GROUP_DIM, w2), F32),
                   jax.ShapeDtypeStruct((B, 2, GROUP_DIM, w3), F32)],
        scratch_shapes=[pltpu.VMEM((tm + 8, CONV_DIM), F32),
                        pltpu.VMEM((INPROJ_PARTS * 3 * STREAM_SLABS, tm // INPROJ_PARTS, LANES), F32)],
        compiler_params=_cparams(("arbitrary", "arbitrary")),
        name="inproj_prompt",
    )(x, mod, mod, g1, w_in, conv_w, gn_conv)
    return outs


def _inproj_sample_kernel(x_ref, sh_ref, sc_ref, g1_ref, w_ref, cw_ref, gnc_ref, p0_ref, p1_ref,
                          convn_ref, u_ref, q_ref, k_ref, v_ref, kt_ref, vt_ref, ubuf_ref, *, tm, t_len):
    h = _rms(x_ref[...], g1_ref[...]) * (1.0 + sc_ref[...]) + sh_ref[...]
    hb = h.astype(BF16)

    pc = jnp.dot(hb, w_ref[:, 0:3 * CONV_DIM], preferred_element_type=F32)
    gb = pc[:, 0:CONV_DIM]
    u = pc[:, CONV_DIM:2 * CONV_DIM] * pc[:, 2 * CONV_DIM:3 * CONV_DIM]
    u_ref[...] = u

    ubuf_ref[0:8, :] = jnp.zeros((8, CONV_DIM), F32)
    ubuf_ref[8:tm + 8, :] = u
    t = lax.broadcasted_iota(jnp.int32, (tm, 1), 0) % t_len
    um1 = jnp.where(t == 0, p1_ref[...], ubuf_ref[7:tm + 7, :])
    um2 = jnp.where(t == 0, p0_ref[...], jnp.where(t == 1, p1_ref[...], ubuf_ref[6:tm + 6, :]))
    cw = cw_ref[...]
    z = cw[0:1, :] * um2 + cw[1:2, :] * um1 + cw[2:3, :] * u
    convn_ref[...] = _rms(gb * z, gnc_ref[...]).astype(BF16)

    c0 = 3 * CONV_DIM
    pq = jnp.dot(hb, w_ref[:, c0:c0 + ATTN_DIM], preferred_element_type=F32)
    q_ref[...] = pq * ATTN_SCALE
    pk = jnp.dot(hb, w_ref[:, c0 + ATTN_DIM:c0 + 2 * ATTN_DIM], preferred_element_type=F32)
    k_ref[...] = pk
    kt_ref[...] = pk.T
    pv = jnp.dot(hb, w_ref[:, c0 + 2 * ATTN_DIM:c0 + 3 * ATTN_DIM], preferred_element_type=F32)
    v_ref[...] = pv
    vt_ref[...] = pv.T


def _inproj_sample(x, mod, g1, w_in, conv_w, gn_conv, prev0, prev1, t_len):
    tm = x.shape[0]
    full = lambda shape: pl.BlockSpec(shape, lambda i: (0,) * len(shape))
    return pl.pallas_call(
        functools.partial(_inproj_sample_kernel, tm=tm, t_len=t_len),
        grid=(1,),
        in_specs=[full((tm, D_MODEL)),
                  pl.BlockSpec((tm, D_MODEL), lambda i: (0, 0)),
                  pl.BlockSpec((tm, D_MODEL), lambda i: (0, 1)),
                  full((1, D_MODEL)), full((D_MODEL, PROJ_DIM)), full((3, CONV_DIM)), full((1, CONV_DIM)),
                  full((tm, CONV_DIM)), full((tm, CONV_DIM))],
        out_specs=[full((tm, CONV_DIM)), full((tm, CONV_DIM)),
                   full((tm, ATTN_DIM)), full((tm, ATTN_DIM)), full((tm, ATTN_DIM)),
                   full((ATTN_DIM, tm)), full((ATTN_DIM, tm))],
        out_shape=[jax.ShapeDtypeStruct((tm, CONV_DIM), BF16),
                   jax.ShapeDtypeStruct((tm, CONV_DIM), F32),
                   jax.ShapeDtypeStruct((tm, ATTN_DIM), F32),
                   jax.ShapeDtypeStruct((tm, ATTN_DIM), F32),
                   jax.ShapeDtypeStruct((tm, ATTN_DIM), F32),
                   jax.ShapeDtypeStruct((ATTN_DIM, tm), F32),
                   jax.ShapeDtypeStruct((ATTN_DIM, tm), F32)],
        scratch_shapes=[pltpu.VMEM((tm + 8, CONV_DIM), F32)],
        compiler_params=_cparams(("arbitrary",)),
        name="inproj_sample",
    )(x, mod, mod, g1, w_in, conv_w, gn_conv, prev0, prev1)


def _t5_buckets(dist):
    dist = np.asarray(dist, np.int32)
    max_exact = N_BUCKETS // 2
    large = max_exact + (np.log(np.maximum(dist, 1).astype(np.float32) / max_exact)
                         / np.log(MAX_DISTANCE / max_exact) * (N_BUCKETS - max_exact)).astype(np.int32)
    large = np.minimum(large, N_BUCKETS - 1)
    return np.where(dist < max_exact, dist, large).astype(np.int32)


def _group_table(rel_bias, g):
    w, d = DIL_PAIRS[g]
    buckets = _t5_buckets(np.arange(w // d + 1) * d)
    return rel_bias[buckets][:, g * GROUP_HEADS:(g + 1) * GROUP_HEADS].T.astype(F32)


def _prompt_bias(table):
    period = 3 * QB - 1
    base = jnp.concatenate([table[:, ::-1], jnp.full((GROUP_HEADS, period - QB - 1), NEG_INF, F32)], axis=1)
    skew = jnp.tile(base, (1, QB))[:, :QB * (period - 1)].reshape(GROUP_HEADS, QB, period - 1)
    prev, cur = skew[:, :, 0:QB], skew[:, :, QB:2 * QB]
    tri = (np.arange(QB)[None, :] <= np.arange(QB)[:, None])[None]
    fold = jnp.stack([jnp.where(tri, cur, prev), jnp.where(tri, cur, NEG_INF)])
    fold = fold.transpose(0, 3, 1, 2).reshape(2, QB, GROUP_HEADS * QB)
    lone = jnp.broadcast_to(table[:, QB:QB + 1], (GROUP_HEADS, QB)).reshape(1, 1, GROUP_HEADS * QB)
    return fold, jnp.concatenate([lone, jnp.full_like(lone, NEG_INF)], axis=0)


def _sample_bias(table, buf_len, d, t_len):
    assert buf_len == QB * d
    asc = table[:, ::-1]
    inter = jnp.concatenate([asc[:, :, None], jnp.full((GROUP_HEADS, QB + 1, d - 1), NEG_INF, F32)],
                            axis=2).reshape(GROUP_HEADS, (QB + 1) * d)
    ext_len = buf_len + LANES
    rows = []
    for t in range(8):
        lead = t if t < t_len else 0
        rows.append(jnp.pad(inter, ((0, 0), (lead, ext_len - lead - (QB + 1) * d)), constant_values=NEG_INF))
    ext = jnp.stack(rows, axis=1)
    return (ext[:, :, :buf_len].reshape(GROUP_HEADS * 8, buf_len),
            ext[:, :, buf_len:].reshape(GROUP_HEADS * 8, LANES))


def _attn_prompt_kernel(q_ref, kc_ref, vc_ref, kp_ref, vp_ref, bias_ref, dbias_ref, o_ref, lse_ref):
    i = pl.program_id(1)
    sb, tq, _ = q_ref.shape
    hm_b = _head_masks(GROUP_DIM, BF16)
    first = jnp.where(i == 0, 1, 0)
    c_idx = lax.broadcasted_iota(jnp.int32, (QB, GROUP_HEADS * QB), 0)
    q_idx = lax.broadcasted_iota(jnp.int32, (QB, GROUP_HEADS * QB), 1) & (QB - 1)
    tri = c_idx <= q_idx
    tri_f = tri.astype(F32)
    diag_f = (c_idx == q_idx).astype(F32)
    work = [(s, j) for s in range(sb) for j in range(tq // QB)]

    def scores(s, j):
        q = q_ref[s, j * QB:(j + 1) * QB, :]
        q4 = jnp.concatenate([q * hm_b[h] for h in range(GROUP_HEADS)], axis=0)
        if j == 0:
            kk = jnp.concatenate([kp_ref[s], kc_ref[s, 0:QB, :]], axis=0)
        else:
            kk = kc_ref[s, (j - 1) * QB:(j + 1) * QB, :]
        return lax.dot_general(kk, q4, (((1,), (1,)), ((), ())), preferred_element_type=F32)

    def values_t(s, j):
        if j == 0:
            return jnp.concatenate([vp_ref[s].T, vc_ref[s, 0:QB, :].T], axis=1)
        return vc_ref[s, (j - 1) * QB:(j + 1) * QB, :].T

    st_next = scores(*work[0])
    for n, (s, j) in enumerate(work):
        st = st_next
        if n + 1 < len(work):
            st_next = scores(*work[n + 1])
        slot = first if j == 0 else 0
        sp = st[0:QB]
        sc = st[QB:2 * QB]
        w = jnp.where(tri, sc, sp) + bias_ref[slot]
        sd = jnp.sum(sp * diag_f, axis=0, keepdims=True) + dbias_ref[slot]
        m = jnp.maximum(jnp.max(w, axis=0, keepdims=True), sd)
        e = jnp.exp(w - m)
        ed = jnp.exp(sd - m)
        den = jnp.sum(e, axis=0, keepdims=True) + ed
        lse = m + jnp.log(den)
        rden = 1.0 / den
        p_cur = e * tri_f
        pt = jnp.concatenate([e - p_cur, p_cur], axis=0).astype(BF16)
        ed_r = ed.astype(BF16).astype(F32)
        vt = values_t(s, j)
        ots = []
        for h in range(GROUP_HEADS):
            cols = slice(h * QB, (h + 1) * QB)
            rows = slice(h * HEAD_DIM, (h + 1) * HEAD_DIM)
            ot = jnp.dot(vt[rows, :], pt[:, cols], preferred_element_type=F32)
            ot = ot + vt[rows, 0:QB].astype(F32) * ed_r[:, cols]
            ots.append(ot * rden[:, cols])
        o_ref[s, j * QB:(j + 1) * QB, :] = jnp.concatenate(ots, axis=0).T
        lrows = jnp.concatenate([lse[:, h * QB:(h + 1) * QB] for h in range(GROUP_HEADS)]
                                + [jnp.zeros((LANES - GROUP_HEADS, QB), F32)], axis=0)
        lse_ref[s, j * QB:(j + 1) * QB, :] = lrows.T


def _attn_prompt(q, k, v, bias, dbias, block_rows=2048):
    ns, M, _ = q.shape
    tq = min(block_rows, M)
    sb = block_rows // tq
    r = tq // QB
    cur = pl.BlockSpec((sb, tq, GROUP_DIM), lambda s, i: (s, i, 0))
    prev = pl.BlockSpec((sb, QB, GROUP_DIM), lambda s, i: (s, jnp.maximum(i * r - 1, 0), 0))
    return pl.pallas_call(
        _attn_prompt_kernel,
        grid=(ns // sb, M // tq),
        in_specs=[cur, cur, cur, prev, prev, _const_spec(bias.shape), _const_spec(dbias.shape)],
        out_specs=[pl.BlockSpec((sb, tq, GROUP_DIM), lambda s, i: (s, i, 0)),
                   pl.BlockSpec((sb, tq, LANES), lambda s, i: (s, i, 0))],
        out_shape=[jax.ShapeDtypeStruct((ns, M, GROUP_DIM), F32),
                   jax.ShapeDtypeStruct((ns, M, LANES), F32)],
        compiler_params=_cparams(("arbitrary", "arbitrary")),
        name="attn_prompt",
    )(q, k, v, k, v, bias, dbias)


def _attn_sample_kernel(q_ref, k8_ref, v8_ref, kt_ref, vt_ref, cache_ref, bc_ref, bn_ref,
                        o_ref, lse_ref, newc_ref, *, buf_len, t_len):
    step = pl.program_id(0)
    rows = range(q_ref.shape[0])
    hm_f = _head_masks(GROUP_DIM, F32)
    lm_f = _head_masks(LANES, F32)
    lane = lax.broadcasted_iota(jnp.int32, (1, LANES), 1)

    q4bs = [jnp.concatenate([q_ref[i] * hm_f[h] for h in range(GROUP_HEADS)], axis=0).astype(BF16)
            for i in rows]
    s_cs = [jnp.dot(q4bs[i], cache_ref[i, 0].astype(BF16), preferred_element_type=F32) + bc_ref[...]
            for i in rows]

    stats = []
    for i in rows:
        q4r = q4bs[i].astype(F32)
        k8r = k8_ref[i].astype(BF16).astype(F32)
        s_n = bn_ref[...]
        for t in range(t_len):
            col = jnp.sum(q4r * k8r[t:t + 1, :], axis=-1, keepdims=True)
            s_n = s_n + jnp.where(lane == t, col, 0.0)
        m = jnp.maximum(jnp.max(s_cs[i], axis=-1, keepdims=True), jnp.max(s_n, axis=-1, keepdims=True))
        e_c = jnp.exp(s_cs[i] - m)
        e_n = jnp.exp(s_n - m)
        den = jnp.sum(e_c, axis=-1, keepdims=True) + jnp.sum(e_n, axis=-1, keepdims=True)
        stats.append((m, e_c, e_n, den))

    pvs = [lax.dot_general(stats[i][1].astype(BF16), cache_ref[i, 1].astype(BF16), (((1,), (1,)), ((), ())),
                           preferred_element_type=F32) for i in rows]
    for i in rows:
        m, _, e_n, den = stats[i]
        pv = pvs[i]
        v8r = v8_ref[i].astype(BF16).astype(F32)
        e_nr = e_n.astype(BF16).astype(F32)
        for t in range(t_len):
            w_t = jnp.sum(jnp.where(lane == t, e_nr, 0.0), axis=-1, keepdims=True)
            pv = pv + w_t * v8r[t:t + 1, :]
        pv = pv / den
        lse = m + jnp.log(den)
        o = jnp.zeros((8, GROUP_DIM), F32)
        l = jnp.zeros((8, LANES), F32)
        for h in range(GROUP_HEADS):
            o = o + pv[h * 8:(h + 1) * 8, :] * hm_f[h]
            l = l + lse[h * 8:(h + 1) * 8, :] * lm_f[h]
        o_ref[i] = o
        lse_ref[i] = l

    keep = LANES - t_len
    per_block = LANES // t_len
    for i in rows:
        shift_new = keep - ((step * len(rows) + i) % per_block) * t_len
        for kv, new_ref in ((0, kt_ref), (1, vt_ref)):
            new_r = pltpu.roll(new_ref[...], shift_new, 1)
            prev_r = None
            for c in range(buf_len // LANES):
                r_c = pltpu.roll(cache_ref[i, kv, :, c * LANES:(c + 1) * LANES], keep, 1)
                if c > 0:
                    newc_ref[i, kv, :, (c - 1) * LANES:c * LANES] = jnp.where(lane < keep, prev_r, r_c)
                prev_r = r_c
            newc_ref[i, kv, :, buf_len - LANES:buf_len] = jnp.where(lane < keep, prev_r, new_r)


SAMPLE_STEP_POSITIONS = 2048
SAMPLE_STEP_ROWS = 8


def _attn_sample(q8, k8, v8, kt, vt, cache, bc, bn, g, t_len):
    nb, _, _, buf_len = cache.shape
    per_block = LANES // t_len
    nbb = max(1, min(per_block, SAMPLE_STEP_ROWS, SAMPLE_STEP_POSITIONS // buf_len))
    assert nb % nbb == 0 and per_block % nbb == 0
    new_spec = pl.BlockSpec((nbb, 8, GROUP_DIM), lambda b: (b, 0, g))
    newt_spec = pl.BlockSpec((GROUP_DIM, LANES), lambda b: (g, (b * nbb) // per_block))
    cache_spec = pl.BlockSpec((nbb, 2, GROUP_DIM, buf_len), lambda b: (b, 0, 0, 0))
    return pl.pallas_call(
        functools.partial(_attn_sample_kernel, buf_len=buf_len, t_len=t_len),
        grid=(nb // nbb,),
        in_specs=[new_spec, new_spec, new_spec, newt_spec, newt_spec, cache_spec,
                  _const_spec(bc.shape), _const_spec(bn.shape)],
        out_specs=[pl.BlockSpec((nbb, 8, GROUP_DIM), lambda b: (b, 0, 0)),
                   pl.BlockSpec((nbb, 8, LANES), lambda b: (b, 0, 0)),
                   cache_spec],
        out_shape=[jax.ShapeDtypeStruct((nb, 8, GROUP_DIM), F32),
                   jax.ShapeDtypeStruct((nb, 8, LANES), F32),
                   jax.ShapeDtypeStruct(cache.shape, F32)],
        compiler_params=_cparams(("arbitrary",)),
        name="attn_sample",
    )(q8, k8, v8, kt, vt, cache, bc, bn)


OUT_PARTS = 2
MERGE_SLABS = N_DIL * (GROUP_DIM + LANES) // LANES


def _load_positions(ref, slab_ref, slab0, r0, rows):
    _, d, _, width = ref.shape
    if d == 1:
        return ref[0, 0, r0:r0 + rows, :]
    for r in range(d):
        for s in range(width // LANES):
            slab_ref[slab0 + s, pl.ds(r, rows // d, stride=d), :] = (
                ref[0, r, r0 // d:(r0 + rows) // d, s * LANES:(s + 1) * LANES])
    return jnp.concatenate([slab_ref[slab0 + s] for s in range(width // LANES)], axis=-1)


def _out_kernel(x_ref, ga1_ref, sh2_ref, sc2_ref, ga2_ref, convn_ref,
                o1_ref, o2_ref, o3_ref, l1_ref, l2_ref, l3_ref,
                gna_ref, wout_ref, g2_ref, wg_ref, wu_ref, wd_ref, fg_ref, y_ref, slab_ref):
    tm = x_ref.shape[1]
    pr = tm // OUT_PARTS
    parts = range(OUT_PARTS)
    hm_f = _head_masks(GROUP_DIM, F32)
    lm_f = _head_masks(LANES, F32)
    gna = gna_ref[...]

    def mod_rows(ref, a):
        return ref[0] if ref.shape[1] == 1 else ref[0, a * pr:(a + 1) * pr, :]

    def merge(a):
        r0 = a * pr
        slab0 = a * MERGE_SLABS
        ls = [_load_positions(l_ref, slab_ref, slab0 + g, r0, pr)
              for g, l_ref in enumerate((l1_ref, l2_ref, l3_ref))]
        m = jnp.maximum(jnp.maximum(ls[0], ls[1]), ls[2])
        es = [jnp.exp(l - m) for l in ls]
        tot = es[0] + es[1] + es[2]
        aos = []
        ssq = jnp.zeros((pr, 1), F32)
        for g, o_ref in enumerate((o1_ref, o2_ref, o3_ref)):
            alpha = es[g] / tot
            wide = jnp.zeros((pr, GROUP_DIM), F32)
            for h in range(GROUP_HEADS):
                a_h = jnp.sum(alpha * lm_f[h], axis=-1, keepdims=True)
                wide = wide + a_h * hm_f[h]
            ao = wide * _load_positions(o_ref, slab_ref, slab0 + N_DIL + g * (GROUP_DIM // LANES), r0, pr)
            ssq = ssq + jnp.sum(ao * ao, axis=-1, keepdims=True)
            aos.append(ao)
        rinv = lax.rsqrt(ssq / ATTN_DIM + EPS)
        return jnp.concatenate(
            [convn_ref[0, r0:r0 + pr, :]]
            + [(aos[g] * rinv * gna[:, g * GROUP_DIM:(g + 1) * GROUP_DIM]).astype(BF16) for g in range(N_DIL)],
            axis=-1)

    def out_proj(a, mixed):
        mix = jnp.dot(mixed, wout_ref[...], preferred_element_type=F32)
        x1 = x_ref[0, a * pr:(a + 1) * pr, :] + mod_rows(ga1_ref, a) * mix
        h2 = (_rms(x1, g2_ref[...]) * (1.0 + mod_rows(sc2_ref, a)) + mod_rows(sh2_ref, a)).astype(BF16)
        return x1, h2

    def swiglu(h2):
        gate = jnp.dot(h2, wg_ref[...], preferred_element_type=F32)
        up = jnp.dot(h2, wu_ref[...], preferred_element_type=F32)
        act = (gate * jax.nn.sigmoid(gate) * up).astype(BF16)
        return jnp.dot(act, wd_ref[...], preferred_element_type=F32)

    mixed = [merge(a) for a in parts]
    x1h2 = [out_proj(a, mixed[a]) for a in parts]
    ffn = [swiglu(h2) for _, h2 in x1h2]
    for a in parts:
        x2 = x1h2[a][0] + mod_rows(ga2_ref, a) * ffn[a]
        y_ref[0, a * pr:(a + 1) * pr, :] = _rms(x2, fg_ref[...])


def _out(x, mod, convn, os_, ls_, gn_attn, w_out, g2, w_gate, w_up, w_down, final_g, tm=512):
    nb, R, _ = x.shape
    tm = min(tm, R)
    per_row = mod.shape[1] != 1
    mrows = tm if per_row else 1
    row = lambda b, i: (b, i, 0)

    def mod_spec(c):
        return pl.BlockSpec((1, mrows, D_MODEL), lambda b, i: (b, i if per_row else 0, c))

    def stream_spec(a):
        _, d, _, width = a.shape
        return pl.BlockSpec((1, d, tm // d, width), lambda b, i: (b, 0, i, 0))

    return pl.pallas_call(
        _out_kernel,
        grid=(nb, R // tm),
        in_specs=[pl.BlockSpec((1, tm, D_MODEL), row),
                  mod_spec(2), mod_spec(3), mod_spec(4), mod_spec(5),
                  pl.BlockSpec((1, tm, CONV_DIM), row)]
                 + [stream_spec(a) for a in os_] + [stream_spec(a) for a in ls_]
                 + [_const_spec((1, ATTN_DIM)), _const_spec((D_MODEL, D_MODEL)), _const_spec((1, D_MODEL)),
                    _const_spec((D_MODEL, D_FF)), _const_spec((D_MODEL, D_FF)), _const_spec((D_FF, D_MODEL)),
                    _const_spec((1, D_MODEL))],
        out_specs=pl.BlockSpec((1, tm, D_MODEL), row),
        out_shape=jax.ShapeDtypeStruct(x.shape, F32),
        scratch_shapes=[pltpu.VMEM((OUT_PARTS * MERGE_SLABS, tm // OUT_PARTS, LANES), F32)],
        compiler_params=_cparams(("arbitrary", "arbitrary")),
        name="out",
    )(x, mod, mod, mod, mod, convn, *os_, *ls_, gn_attn, w_out, g2, w_gate, w_up, w_down, final_g)


def _buffer_to_channel_major(cache):
    B, L = cache.shape[:2]
    return cache.transpose(0, 2, 3, 4, 1).reshape(B, 2, GROUP_DIM, L)


def _buffer_from_channel_major(buf):
    B, _, _, L = buf.shape
    return buf.reshape(B, 2, GROUP_HEADS, HEAD_DIM, L).transpose(0, 4, 1, 2, 3)[None]


def kernel(x_prompt, x_sample, state_conv, cache_kv1, cache_kv2, cache_kv3, c_prompt, c_sample, w_ada, b_ada, norm1_g, norm2_g, w_in, conv_w, gn_conv, gn_attn, w_out, w_gate, w_up, w_down, rel_bias, final_g):
    B, S, _ = x_prompt.shape
    nb, t_len, _ = x_sample.shape
    rows_s = nb * t_len
    assert w_ada.shape[0] == 1 and LANES % t_len == 0 and 2 <= t_len <= 8 and rows_s % LANES == 0

    w_in_b = w_in[0].astype(BF16)
    w_out_b = w_out[0].astype(BF16)
    w_gate_b = w_gate[0].astype(BF16)
    w_up_b = w_up[0].astype(BF16)
    w_down_b = w_down[0].astype(BF16)
    final_g2 = final_g.reshape(1, D_MODEL)
    tables = [_group_table(rel_bias, g) for g in range(N_DIL)]

    c_all = jnp.concatenate([c_prompt, jnp.repeat(c_sample, t_len, axis=0)], axis=0)
    mod = _ada(c_all, w_ada[0], b_ada)
    mod_p = mod[:B].reshape(B, 1, 6 * D_MODEL)
    mod_s = mod[B:]

    (convn, q1, q2, q3, k1, k2, k3, v1, v2, v3,
     nconv_p, kv1_p, kv2_p, kv3_p) = _inproj_prompt(x_prompt, mod_p, norm1_g, w_in_b, conv_w[0], gn_conv)
    os_, ls_ = [], []
    for g, (qg, kg, vg) in enumerate(((q1, k1, v1), (q2, k2, v2), (q3, k3, v3))):
        _, d, M, _ = qg.shape
        merge = lambda a: a.reshape(B * d, M, GROUP_DIM)
        o, l = _attn_prompt(merge(qg), merge(kg), merge(vg), *_prompt_bias(tables[g]))
        os_.append(o.reshape(B, d, M, GROUP_DIM))
        ls_.append(l.reshape(B, d, M, LANES))
    y_prompt = _out(x_prompt, mod_p, convn, os_, ls_, gn_attn, w_out_b, norm2_g,
                    w_gate_b, w_up_b, w_down_b, final_g2)

    xs = x_sample.reshape(rows_s, D_MODEL)
    prev0 = jnp.repeat(state_conv[0, :, 0], t_len, axis=0)
    prev1 = jnp.repeat(state_conv[0, :, 1], t_len, axis=0)
    convn_s, u_s, q_s, k_s, v_s, kt_s, vt_s = _inproj_sample(
        xs, mod_s, norm1_g, w_in_b, conv_w[0], gn_conv, prev0, prev1, t_len)
    pad8 = lambda a: jnp.pad(a.reshape(nb, t_len, ATTN_DIM), ((0, 0), (0, 8 - t_len), (0, 0)))
    q8, k8, v8 = pad8(q_s), pad8(k_s), pad8(v_s)
    os_s, ls_s, new_bufs = [], [], []
    for g, (cache, (_, d)) in enumerate(zip((cache_kv1, cache_kv2, cache_kv3), DIL_PAIRS)):
        buf = _buffer_to_channel_major(cache[0])
        bc, bn = _sample_bias(tables[g], buf.shape[-1], d, t_len)
        o, l, nbuf = _attn_sample(q8, k8, v8, kt_s, vt_s, buf, bc, bn, g, t_len)
        os_s.append(o[:, :t_len].reshape(1, 1, rows_s, GROUP_DIM))
        ls_s.append(l[:, :t_len].reshape(1, 1, rows_s, LANES))
        new_bufs.append(_buffer_from_channel_major(nbuf))
    y_sample = _out(xs[None], mod_s[None], convn_s[None], os_s, ls_s, gn_attn, w_out_b, norm2_g,
                    w_gate_b, w_up_b, w_down_b, final_g2)
    y_sample = y_sample.reshape(nb, t_len, D_MODEL)
    nconv_s = u_s.reshape(nb, t_len, CONV_DIM)[:, t_len - 2:]

    return (y_prompt, y_sample, nconv_p[None],
            _buffer_from_channel_major(kv1_p), _buffer_from_channel_major(kv2_p),
            _buffer_from_channel_major(kv3_p),
            nconv_s[None], new_bufs[0], new_bufs[1], new_bufs[2])
```

```python
import functools

import numpy as np
import jax
import jax.numpy as jnp
from jax import lax
from jax.experimental import pallas as pl
from jax.experimental.pallas import tpu as pltpu

F32 = jnp.float32
BF16 = jnp.bfloat16

D_MODEL = 1024
HEAD_DIM = 64
CONV_DIM = 256
ATTN_DIM = 768
GROUP_HEADS = 4
GROUP_DIM = GROUP_HEADS * HEAD_DIM
DIL_PAIRS = ((128, 1), (512, 4), (2048, 16))
N_DIL = len(DIL_PAIRS)
QB = 128
N_BUCKETS = 32
MAX_DISTANCE = 2048
D_FF = 2816
PROJ_DIM = 3072
EPS = 1e-6
NEG_INF = -1e30
ATTN_SCALE = HEAD_DIM ** -0.5

LANES = 128
VMEM_LIMIT = 56 * 1024 * 1024


def _cparams(sem):
    return pltpu.CompilerParams(dimension_semantics=sem, vmem_limit_bytes=VMEM_LIMIT)


def _const_spec(shape):
    nd = len(shape)
    return pl.BlockSpec(shape, lambda *_: (0,) * nd, pipeline_mode=pl.Buffered(1))


def _head_masks(width, dtype):
    lane = lax.broadcasted_iota(jnp.int32, (1, width), 1)
    if width == GROUP_DIM:
        lane = lane // HEAD_DIM
    return [(lane == h).astype(dtype) for h in range(GROUP_HEADS)]


def _ada_kernel(c_ref, w_ref, b_ref, o_ref):
    c = c_ref[...]
    s = (c * jax.nn.sigmoid(c)).astype(BF16)
    o_ref[...] = jnp.dot(s, w_ref[...].astype(BF16), preferred_element_type=F32) + b_ref[...]


def _ada(c_all, w_ada, b_ada):
    rows = c_all.shape[0]
    tn = 1536
    return pl.pallas_call(
        _ada_kernel,
        grid=(6 * D_MODEL // tn,),
        in_specs=[pl.BlockSpec((rows, D_MODEL), lambda j: (0, 0)),
                  pl.BlockSpec((D_MODEL, tn), lambda j: (0, j)),
                  pl.BlockSpec((1, tn), lambda j: (0, j))],
        out_specs=pl.BlockSpec((rows, tn), lambda j: (0, j)),
        out_shape=jax.ShapeDtypeStruct((rows, 6 * D_MODEL), F32),
        compiler_params=_cparams(("arbitrary",)),
        name="ada",
    )(c_all, w_ada, b_ada)


def _rms(x, g):
    return x * lax.rsqrt(jnp.mean(x * x, axis=-1, keepdims=True) + EPS) * g


STREAM_SLABS = sum(GROUP_DIM // LANES for _, d in DIL_PAIRS if d > 1)
INPROJ_PARTS = 2


def _store_streams(p, out_refs, slab_ref, slab0, r0):
    rows = p.shape[0]
    for g, ((_, d), out_ref) in enumerate(zip(DIL_PAIRS, out_refs)):
        lo = g * GROUP_DIM
        if d == 1:
            out_ref[0, 0, r0:r0 + rows, :] = p[:, lo:lo + GROUP_DIM].astype(BF16)
            continue
        for s in range(GROUP_DIM // LANES):
            slab_ref[slab0 + s] = p[:, lo + s * LANES:lo + (s + 1) * LANES]
        for r in range(d):
            for s in range(GROUP_DIM // LANES):
                out_ref[0, r, r0 // d:(r0 + rows) // d, s * LANES:(s + 1) * LANES] = (
                    slab_ref[slab0 + s, pl.ds(r, rows // d, stride=d), :].astype(BF16))
        slab0 += GROUP_DIM // LANES


def _inproj_prompt_kernel(x_ref, sh_ref, sc_ref, g1_ref, w_ref, cw_ref, gnc_ref,
                          convn_ref, q1_ref, q2_ref, q3_ref, k1_ref, k2_ref, k3_ref,
                          v1_ref, v2_ref, v3_ref, nconv_ref, kv1_ref, kv2_ref, kv3_ref,
                          ubuf_ref, slab_ref, *, tm):
    i = pl.program_id(1)
    n = pl.num_programs(1)
    pr = tm // INPROJ_PARTS
    parts = range(INPROJ_PARTS)
    c0 = 3 * CONV_DIM
    cw = cw_ref[...]

    hbs = [(_rms(x_ref[0, a * pr:(a + 1) * pr, :], g1_ref[...]) * (1.0 + sc_ref[0]) + sh_ref[0]).astype(BF16)
           for a in parts]
    projs = [jnp.dot(hb, w_ref[...], preferred_element_type=F32) for hb in hbs]

    tail = ubuf_ref[tm:tm + 8, :]
    ubuf_ref[0:8, :] = jnp.where(i == 0, jnp.zeros_like(tail), tail)
    for a in parts:
        r0 = a * pr
        p = projs[a]
        gb = p[:, 0:CONV_DIM]
        u = p[:, CONV_DIM:2 * CONV_DIM] * p[:, 2 * CONV_DIM:c0]
        ubuf_ref[8 + r0:8 + r0 + pr, :] = u
        um1 = ubuf_ref[7 + r0:7 + r0 + pr, :]
        um2 = ubuf_ref[6 + r0:6 + r0 + pr, :]
        z = cw[0:1, :] * um2 + cw[1:2, :] * um1 + cw[2:3, :] * u
        convn_ref[0, r0:r0 + pr, :] = _rms(gb * z, gnc_ref[...]).astype(BF16)
        slab0 = a * 3 * STREAM_SLABS
        _store_streams(p[:, c0:c0 + ATTN_DIM] * ATTN_SCALE, (q1_ref, q2_ref, q3_ref), slab_ref, slab0, r0)
        _store_streams(p[:, c0 + ATTN_DIM:c0 + 2 * ATTN_DIM], (k1_ref, k2_ref, k3_ref), slab_ref,
                       slab0 + STREAM_SLABS, r0)
        _store_streams(p[:, c0 + 2 * ATTN_DIM:c0 + 3 * ATTN_DIM], (v1_ref, v2_ref, v3_ref), slab_ref,
                       slab0 + 2 * STREAM_SLABS, r0)
    nconv_ref[0] = ubuf_ref[tm + 6:tm + 8, :]

    def put(ref, g, rows):
        for a in parts:
            lo_r, hi_r = max(tm - rows, a * pr), (a + 1) * pr
            if hi_r <= lo_r:
                continue
            for kv in range(2):
                lo = c0 + (1 + kv) * ATTN_DIM + g * GROUP_DIM
                ref[0, kv, :, lo_r - (tm - rows):hi_r - (tm - rows)] = (
                    projs[a][lo_r - a * pr:hi_r - a * pr, lo:lo + GROUP_DIM].T)

    w3 = DIL_PAIRS[2][0]

    @pl.when(i >= n - w3 // tm)
    def _():
        put(kv3_ref, 2, tm)

    @pl.when(i == n - 1)
    def _():
        put(kv2_ref, 1, DIL_PAIRS[1][0])
        put(kv1_ref, 0, DIL_PAIRS[0][0])


def _inproj_prompt(x, mod, g1, w_in, conv_w, gn_conv, tm=1024):
    B, S, _ = x.shape
    n = S // tm
    w1, w2, w3 = (w for w, _ in DIL_PAIRS)
    assert S % tm == 0 and w3 % tm == 0 and tm >= w2 and tm % (INPROJ_PARTS * LANES) == 0
    n3 = w3 // tm
    row = lambda b, i: (b, i, 0)
    stream_specs = [pl.BlockSpec((1, d, tm // d, GROUP_DIM), lambda b, i: (b, 0, i, 0)) for _, d in DIL_PAIRS]
    stream_shapes = [jax.ShapeDtypeStruct((B, d, S // d, GROUP_DIM), BF16) for _, d in DIL_PAIRS]
    outs = pl.pallas_call(
        functools.partial(_inproj_prompt_kernel, tm=tm),
        grid=(B, n),
        in_specs=[pl.BlockSpec((1, tm, D_MODEL), row),
                  pl.BlockSpec((1, 1, D_MODEL), lambda b, i: (b, 0, 0)),
                  pl.BlockSpec((1, 1, D_MODEL), lambda b, i: (b, 0, 1)),
                  _const_spec((1, D_MODEL)),
                  _const_spec((D_MODEL, PROJ_DIM)),
                  _const_spec((3, CONV_DIM)),
                  _const_spec((1, CONV_DIM))],
        out_specs=[pl.BlockSpec((1, tm, CONV_DIM), row)] + stream_specs * 3 + [
                   pl.BlockSpec((1, 2, CONV_DIM), lambda b, i: (b, 0, 0)),
                   pl.BlockSpec((1, 2, GROUP_DIM, w1), lambda b, i: (b, 0, 0, 0)),
                   pl.BlockSpec((1, 2, GROUP_DIM, w2), lambda b, i: (b, 0, 0, 0)),
                   pl.BlockSpec((1, 2, GROUP_DIM, tm),
                                lambda b, i: (b, 0, 0, jnp.maximum(i - (n - n3), 0)))],
        out_shape=[jax.ShapeDtypeStruct((B, S, CONV_DIM), BF16)] + stream_shapes * 3 + [
                   jax.ShapeDtypeStruct((B, 2, CONV_DIM), F32),
                   jax.ShapeDtypeStruct((B, 2, GROUP_DIM, w1), F32),
                   jax.ShapeDtypeStruct((B, 2, GROUP_DIM, w2), F32),
                   jax.ShapeDtypeStruct((B, 2, GROUP_DIM, w3), F32)],
        scratch_shapes=[pltpu.VMEM((tm + 8, CONV_DIM), F32),
                        pltpu.VMEM((INPROJ_PARTS * 3 * STREAM_SLABS, tm // INPROJ_PARTS, LANES), F32)],
        compiler_params=_cparams(("arbitrary", "arbitrary")),
        name="inproj_prompt",
    )(x, mod, mod, g1, w_in, conv_w, gn_conv)
    return outs


def _inproj_sample_kernel(x_ref, sh_ref, sc_ref, g1_ref, w_ref, cw_ref, gnc_ref, p0_ref, p1_ref,
                          convn_ref, u_ref, q_ref, k_ref, v_ref, kt_ref, vt_ref, ubuf_ref, *, tm, t_len):
    h = _rms(x_ref[...], g1_ref[...]) * (1.0 + sc_ref[...]) + sh_ref[...]
    hb = h.astype(BF16)

    pc = jnp.dot(hb, w_ref[:, 0:3 * CONV_DIM], preferred_element_type=F32)
    gb = pc[:, 0:CONV_DIM]
    u = pc[:, CONV_DIM:2 * CONV_DIM] * pc[:, 2 * CONV_DIM:3 * CONV_DIM]
    u_ref[...] = u

    ubuf_ref[0:8, :] = jnp.zeros((8, CONV_DIM), F32)
    ubuf_ref[8:tm + 8, :] = u
    t = lax.broadcasted_iota(jnp.int32, (tm, 1), 0) % t_len
    um1 = jnp.where(t == 0, p1_ref[...], ubuf_ref[7:tm + 7, :])
    um2 = jnp.where(t == 0, p0_ref[...], jnp.where(t == 1, p1_ref[...], ubuf_ref[6:tm + 6, :]))
    cw = cw_ref[...]
    z = cw[0:1, :] * um2 + cw[1:2, :] * um1 + cw[2:3, :] * u
    convn_ref[...] = _rms(gb * z, gnc_ref[...]).astype(BF16)

    c0 = 3 * CONV_DIM
    pq = jnp.dot(hb, w_ref[:, c0:c0 + ATTN_DIM], preferred_element_type=F32)
    q_ref[...] = pq * ATTN_SCALE
    pk = jnp.dot(hb, w_ref[:, c0 + ATTN_DIM:c0 + 2 * ATTN_DIM], preferred_element_type=F32)
    k_ref[...] = pk
    kt_ref[...] = pk.T
    pv = jnp.dot(hb, w_ref[:, c0 + 2 * ATTN_DIM:c0 + 3 * ATTN_DIM], preferred_element_type=F32)
    v_ref[...] = pv
    vt_ref[...] = pv.T


def _inproj_sample(x, mod, g1, w_in, conv_w, gn_conv, prev0, prev1, t_len):
    tm = x.shape[0]
    full = lambda shape: pl.BlockSpec(shape, lambda i: (0,) * len(shape))
    return pl.pallas_call(
        functools.partial(_inproj_sample_kernel, tm=tm, t_len=t_len),
        grid=(1,),
        in_specs=[full((tm, D_MODEL)),
                  pl.BlockSpec((tm, D_MODEL), lambda i: (0, 0)),
                  pl.BlockSpec((tm, D_MODEL), lambda i: (0, 1)),
                  full((1, D_MODEL)), full((D_MODEL, PROJ_DIM)), full((3, CONV_DIM)), full((1, CONV_DIM)),
                  full((tm, CONV_DIM)), full((tm, CONV_DIM))],
        out_specs=[full((tm, CONV_DIM)), full((tm, CONV_DIM)),
                   full((tm, ATTN_DIM)), full((tm, ATTN_DIM)), full((tm, ATTN_DIM)),
                   full((ATTN_DIM, tm)), full((ATTN_DIM, tm))],
        out_shape=[jax.ShapeDtypeStruct((tm, CONV_DIM), BF16),
                   jax.ShapeDtypeStruct((tm, CONV_DIM), F32),
                   jax.ShapeDtypeStruct((tm, ATTN_DIM), F32),
                   jax.ShapeDtypeStruct((tm, ATTN_DIM), F32),
                   jax.ShapeDtypeStruct((tm, ATTN_DIM), F32),
                   jax.ShapeDtypeStruct((ATTN_DIM, tm), F32),
                   jax.ShapeDtypeStruct((ATTN_DIM, tm), F32)],
        scratch_shapes=[pltpu.VMEM((tm + 8, CONV_DIM), F32)],
        compiler_params=_cparams(("arbitrary",)),
        name="inproj_sample",
    )(x, mod, mod, g1, w_in, conv_w, gn_conv, prev0, prev1)


def _t5_buckets(dist):
    dist = np.asarray(dist, np.int32)
    max_exact = N_BUCKETS // 2
    large = max_exact + (np.log(np.maximum(dist, 1).astype(np.float32) / max_exact)
                         / np.log(MAX_DISTANCE / max_exact) * (N_BUCKETS - max_exact)).astype(np.int32)
    large = np.minimum(large, N_BUCKETS - 1)
    return np.where(dist < max_exact, dist, large).astype(np.int32)


def _group_table(rel_bias, g):
    w, d = DIL_PAIRS[g]
    buckets = _t5_buckets(np.arange(w // d + 1) * d)
    return rel_bias[buckets][:, g * GROUP_HEADS:(g + 1) * GROUP_HEADS].T.astype(F32)


def _prompt_bias(table):
    period = 3 * QB - 1
    base = jnp.concatenate([table[:, ::-1], jnp.full((GROUP_HEADS, period - QB - 1), NEG_INF, F32)], axis=1)
    skew = jnp.tile(base, (1, QB))[:, :QB * (period - 1)].reshape(GROUP_HEADS, QB, period - 1)
    prev, cur = skew[:, :, 0:QB], skew[:, :, QB:2 * QB]
    tri = (np.arange(QB)[None, :] <= np.arange(QB)[:, None])[None]
    fold = jnp.stack([jnp.where(tri, cur, prev), jnp.where(tri, cur, NEG_INF)])
    fold = fold.transpose(0, 3, 1, 2).reshape(2, QB, GROUP_HEADS * QB)
    lone = jnp.broadcast_to(table[:, QB:QB + 1], (GROUP_HEADS, QB)).reshape(1, 1, GROUP_HEADS * QB)
    return fold, jnp.concatenate([lone, jnp.full_like(lone, NEG_INF)], axis=0)


def _sample_bias(table, buf_len, d, t_len):
    assert buf_len == QB * d
    asc = table[:, ::-1]
    inter = jnp.concatenate([asc[:, :, None], jnp.full((GROUP_HEADS, QB + 1, d - 1), NEG_INF, F32)],
                            axis=2).reshape(GROUP_HEADS, (QB + 1) * d)
    ext_len = buf_len + LANES
    rows = []
    for t in range(8):
        lead = t if t < t_len else 0
        rows.append(jnp.pad(inter, ((0, 0), (lead, ext_len - lead - (QB + 1) * d)), constant_values=NEG_INF))
    ext = jnp.stack(rows, axis=1)
    return (ext[:, :, :buf_len].reshape(GROUP_HEADS * 8, buf_len),
            ext[:, :, buf_len:].reshape(GROUP_HEADS * 8, LANES))


def _attn_prompt_kernel(q_ref, kc_ref, vc_ref, kp_ref, vp_ref, bias_ref, dbias_ref, o_ref, lse_ref):
    i = pl.program_id(1)
    sb, tq, _ = q_ref.shape
    hm_b = _head_masks(GROUP_DIM, BF16)
    first = jnp.where(i == 0, 1, 0)
    c_idx = lax.broadcasted_iota(jnp.int32, (QB, GROUP_HEADS * QB), 0)
    q_idx = lax.broadcasted_iota(jnp.int32, (QB, GROUP_HEADS * QB), 1) & (QB - 1)
    tri = c_idx <= q_idx
    tri_f = tri.astype(F32)
    diag_f = (c_idx == q_idx).astype(F32)
    work = [(s, j) for s in range(sb) for j in range(tq // QB)]

    def scores(s, j):
        q = q_ref[s, j * QB:(j + 1) * QB, :]
        q4 = jnp.concatenate([q * hm_b[h] for h in range(GROUP_HEADS)], axis=0)
        if j == 0:
            kk = jnp.concatenate([kp_ref[s], kc_ref[s, 0:QB, :]], axis=0)
        else:
            kk = kc_ref[s, (j - 1) * QB:(j + 1) * QB, :]
        return lax.dot_general(kk, q4, (((1,), (1,)), ((), ())), preferred_element_type=F32)

    def values_t(s, j):
        if j == 0:
            return jnp.concatenate([vp_ref[s].T, vc_ref[s, 0:QB, :].T], axis=1)
        return vc_ref[s, (j - 1) * QB:(j + 1) * QB, :].T

    st_next = scores(*work[0])
    for n, (s, j) in enumerate(work):
        st = st_next
        if n + 1 < len(work):
            st_next = scores(*work[n + 1])
        slot = first if j == 0 else 0
        sp = st[0:QB]
        sc = st[QB:2 * QB]
        w = jnp.where(tri, sc, sp) + bias_ref[slot]
        sd = jnp.sum(sp * diag_f, axis=0, keepdims=True) + dbias_ref[slot]
        m = jnp.maximum(jnp.max(w, axis=0, keepdims=True), sd)
        e = jnp.exp(w - m)
        ed = jnp.exp(sd - m)
        den = jnp.sum(e, axis=0, keepdims=True) + ed
        lse = m + jnp.log(den)
        rden = 1.0 / den
        p_cur = e * tri_f
        pt = jnp.concatenate([e - p_cur, p_cur], axis=0).astype(BF16)
        ed_r = ed.astype(BF16).astype(F32)
        vt = values_t(s, j)
        ots = []
        for h in range(GROUP_HEADS):
            cols = slice(h * QB, (h + 1) * QB)
            rows = slice(h * HEAD_DIM, (h + 1) * HEAD_DIM)
            ot = jnp.dot(vt[rows, :], pt[:, cols], preferred_element_type=F32)
            ot = ot + vt[rows, 0:QB].astype(F32) * ed_r[:, cols]
            ots.append(ot * rden[:, cols])
        o_ref[s, j * QB:(j + 1) * QB, :] = jnp.concatenate(ots, axis=0).T
        lrows = jnp.concatenate([lse[:, h * QB:(h + 1) * QB] for h in range(GROUP_HEADS)]
                                + [jnp.zeros((LANES - GROUP_HEADS, QB), F32)], axis=0)
        lse_ref[s, j * QB:(j + 1) * QB, :] = lrows.T


def _attn_prompt(q, k, v, bias, dbias, block_rows=2048):
    ns, M, _ = q.shape
    tq = min(block_rows, M)
    sb = block_rows // tq
    r = tq // QB
    cur = pl.BlockSpec((sb, tq, GROUP_DIM), lambda s, i: (s, i, 0))
    prev = pl.BlockSpec((sb, QB, GROUP_DIM), lambda s, i: (s, jnp.maximum(i * r - 1, 0), 0))
    return pl.pallas_call(
        _attn_prompt_kernel,
        grid=(ns // sb, M // tq),
        in_specs=[cur, cur, cur, prev, prev, _const_spec(bias.shape), _const_spec(dbias.shape)],
        out_specs=[pl.BlockSpec((sb, tq, GROUP_DIM), lambda s, i: (s, i, 0)),
                   pl.BlockSpec((sb, tq, LANES), lambda s, i: (s, i, 0))],
        out_shape=[jax.ShapeDtypeStruct((ns, M, GROUP_DIM), F32),
                   jax.ShapeDtypeStruct((ns, M, LANES), F32)],
        compiler_params=_cparams(("arbitrary", "arbitrary")),
        name="attn_prompt",
    )(q, k, v, k, v, bias, dbias)


def _attn_sample_kernel(q_ref, k8_ref, v8_ref, kt_ref, vt_ref, cache_ref, bc_ref, bn_ref,
                        o_ref, lse_ref, newc_ref, *, buf_len, t_len):
    _attn_sample_rows(pl.program_id(0), q_ref, k8_ref, v8_ref, kt_ref, vt_ref, cache_ref, bc_ref, bn_ref,
                      o_ref, lse_ref, newc_ref, buf_len, t_len)


def _attn_sample_rows(step, q_ref, k8_ref, v8_ref, kt_ref, vt_ref, cache_ref, bc_ref, bn_ref,
                      o_ref, lse_ref, newc_ref, buf_len, t_len):
    rows = range(q_ref.shape[0])
    hm_f = _head_masks(GROUP_DIM, F32)
    lm_f = _head_masks(LANES, F32)
    lane = lax.broadcasted_iota(jnp.int32, (1, LANES), 1)

    q4bs = [jnp.concatenate([q_ref[i] * hm_f[h] for h in range(GROUP_HEADS)], axis=0).astype(BF16)
            for i in rows]
    s_cs = [jnp.dot(q4bs[i], cache_ref[i, 0].astype(BF16), preferred_element_type=F32) + bc_ref[...]
            for i in rows]

    stats = []
    for i in rows:
        q4r = q4bs[i].astype(F32)
        k8r = k8_ref[i].astype(BF16).astype(F32)
        s_n = bn_ref[...]
        for t in range(t_len):
            col = jnp.sum(q4r * k8r[t:t + 1, :], axis=-1, keepdims=True)
            s_n = s_n + jnp.where(lane == t, col, 0.0)
        m = jnp.maximum(jnp.max(s_cs[i], axis=-1, keepdims=True), jnp.max(s_n, axis=-1, keepdims=True))
        e_c = jnp.exp(s_cs[i] - m)
        e_n = jnp.exp(s_n - m)
        den = jnp.sum(e_c, axis=-1, keepdims=True) + jnp.sum(e_n, axis=-1, keepdims=True)
        stats.append((m, e_c, e_n, den))

    pvs = [lax.dot_general(stats[i][1].astype(BF16), cache_ref[i, 1].astype(BF16), (((1,), (1,)), ((), ())),
                           preferred_element_type=F32) for i in rows]
    for i in rows:
        m, _, e_n, den = stats[i]
        pv = pvs[i]
        v8r = v8_ref[i].astype(BF16).astype(F32)
        e_nr = e_n.astype(BF16).astype(F32)
        for t in range(t_len):
            w_t = jnp.sum(jnp.where(lane == t, e_nr, 0.0), axis=-1, keepdims=True)
            pv = pv + w_t * v8r[t:t + 1, :]
        pv = pv / den
        lse = m + jnp.log(den)
        o = jnp.zeros((8, GROUP_DIM), F32)
        l = jnp.zeros((8, LANES), F32)
        for h in range(GROUP_HEADS):
            o = o + pv[h * 8:(h + 1) * 8, :] * hm_f[h]
            l = l + lse[h * 8:(h + 1) * 8, :] * lm_f[h]
        o_ref[i] = o
        lse_ref[i] = l

    keep = LANES - t_len
    per_block = LANES // t_len
    for i in rows:
        shift_new = keep - ((step * len(rows) + i) % per_block) * t_len
        for kv, new_ref in ((0, kt_ref), (1, vt_ref)):
            new_r = pltpu.roll(new_ref[...], shift_new, 1)
            prev_r = None
            for c in range(buf_len // LANES):
                r_c = pltpu.roll(cache_ref[i, kv, :, c * LANES:(c + 1) * LANES], keep, 1)
                if c > 0:
                    newc_ref[i, kv, :, (c - 1) * LANES:c * LANES] = jnp.where(lane < keep, prev_r, r_c)
                prev_r = r_c
            newc_ref[i, kv, :, buf_len - LANES:buf_len] = jnp.where(lane < keep, prev_r, new_r)


SAMPLE_STEP_POSITIONS = 2048
SAMPLE_STEP_ROWS = 8


def _attn_sample_specs(cache_shape, bc, bn, g, t_len, nbb, step_of):
    nb, _, _, buf_len = cache_shape
    per_block = LANES // t_len
    assert nb % nbb == 0 and per_block % nbb == 0
    new_spec = pl.BlockSpec((nbb, 8, GROUP_DIM), lambda *i: (step_of(*i), 0, g))
    newt_spec = pl.BlockSpec((GROUP_DIM, LANES), lambda *i: (g, (step_of(*i) * nbb) // per_block))
    cache_spec = pl.BlockSpec((nbb, 2, GROUP_DIM, buf_len), lambda *i: (step_of(*i), 0, 0, 0))
    in_specs = [new_spec, new_spec, new_spec, newt_spec, newt_spec, cache_spec,
                _const_spec(bc.shape), _const_spec(bn.shape)]
    out_specs = [pl.BlockSpec((nbb, 8, GROUP_DIM), lambda *i: (step_of(*i), 0, 0)),
                 pl.BlockSpec((nbb, 8, LANES), lambda *i: (step_of(*i), 0, 0)),
                 cache_spec]
    out_shapes = [jax.ShapeDtypeStruct((nb, 8, GROUP_DIM), F32),
                  jax.ShapeDtypeStruct((nb, 8, LANES), F32),
                  jax.ShapeDtypeStruct(cache_shape, F32)]
    return in_specs, out_specs, out_shapes


def _attn_sample(q8, k8, v8, kt, vt, cache, bc, bn, g, t_len):
    nb, _, _, buf_len = cache.shape
    nbb = max(1, min(LANES // t_len, SAMPLE_STEP_ROWS, SAMPLE_STEP_POSITIONS // buf_len))
    in_specs, out_specs, out_shapes = _attn_sample_specs(cache.shape, bc, bn, g, t_len, nbb, lambda b: b)
    return pl.pallas_call(
        functools.partial(_attn_sample_kernel, buf_len=buf_len, t_len=t_len),
        grid=(nb // nbb,),
        in_specs=in_specs, out_specs=out_specs, out_shape=out_shapes,
        compiler_params=_cparams(("arbitrary",)),
        name="attn_sample",
    )(q8, k8, v8, kt, vt, cache, bc, bn)


OUT_PARTS = 2
MERGE_SLABS = N_DIL * (GROUP_DIM + LANES) // LANES


def _load_positions(ref, slab_ref, slab0, r0, rows):
    _, d, _, width = ref.shape
    if d == 1:
        return ref[0, 0, r0:r0 + rows, :]
    for r in range(d):
        for s in range(width // LANES):
            slab_ref[slab0 + s, pl.ds(r, rows // d, stride=d), :] = (
                ref[0, r, r0 // d:(r0 + rows) // d, s * LANES:(s + 1) * LANES])
    return jnp.concatenate([slab_ref[slab0 + s] for s in range(width // LANES)], axis=-1)


N_OUT_INPUTS = 19
N_SAMPLE_INPUTS = 8


def _out_kernel(*refs, hosted):
    (x_ref, ga1_ref, sh2_ref, sc2_ref, ga2_ref, convn_ref, o1_ref, o2_ref, o3_ref, l1_ref, l2_ref, l3_ref,
     gna_ref, wout_ref, g2_ref, wg_ref, wu_ref, wd_ref, fg_ref) = refs[:N_OUT_INPUTS]
    if hosted is None:
        y_ref, slab_ref = refs[N_OUT_INPUTS:]
    else:
        sample_in = refs[N_OUT_INPUTS:N_OUT_INPUTS + N_SAMPLE_INPUTS]
        y_ref, *sample_out, slab_ref = refs[N_OUT_INPUTS + N_SAMPLE_INPUTS:]
    tm = x_ref.shape[1]
    pr = tm // OUT_PARTS
    parts = range(OUT_PARTS)
    hm_f = _head_masks(GROUP_DIM, F32)
    lm_f = _head_masks(LANES, F32)
    gna = gna_ref[...]

    def mod_rows(ref, a):
        return ref[0] if ref.shape[1] == 1 else ref[0, a * pr:(a + 1) * pr, :]

    def merge(a):
        r0 = a * pr
        slab0 = a * MERGE_SLABS
        ls = [_load_positions(l_ref, slab_ref, slab0 + g, r0, pr)
              for g, l_ref in enumerate((l1_ref, l2_ref, l3_ref))]
        m = jnp.maximum(jnp.maximum(ls[0], ls[1]), ls[2])
        es = [jnp.exp(l - m) for l in ls]
        tot = es[0] + es[1] + es[2]
        aos = []
        ssq = jnp.zeros((pr, 1), F32)
        for g, o_ref in enumerate((o1_ref, o2_ref, o3_ref)):
            alpha = es[g] / tot
            wide = jnp.zeros((pr, GROUP_DIM), F32)
            for h in range(GROUP_HEADS):
                a_h = jnp.sum(alpha * lm_f[h], axis=-1, keepdims=True)
                wide = wide + a_h * hm_f[h]
            ao = wide * _load_positions(o_ref, slab_ref, slab0 + N_DIL + g * (GROUP_DIM // LANES), r0, pr)
            ssq = ssq + jnp.sum(ao * ao, axis=-1, keepdims=True)
            aos.append(ao)
        rinv = lax.rsqrt(ssq / ATTN_DIM + EPS)
        return jnp.concatenate(
            [convn_ref[0, r0:r0 + pr, :]]
            + [(aos[g] * rinv * gna[:, g * GROUP_DIM:(g + 1) * GROUP_DIM]).astype(BF16) for g in range(N_DIL)],
            axis=-1)

    def out_proj(a, mixed):
        mix = jnp.dot(mixed, wout_ref[...], preferred_element_type=F32)
        x1 = x_ref[0, a * pr:(a + 1) * pr, :] + mod_rows(ga1_ref, a) * mix
        h2 = (_rms(x1, g2_ref[...]) * (1.0 + mod_rows(sc2_ref, a)) + mod_rows(sh2_ref, a)).astype(BF16)
        return x1, h2

    def swiglu(h2):
        gate = jnp.dot(h2, wg_ref[...], preferred_element_type=F32)
        up = jnp.dot(h2, wu_ref[...], preferred_element_type=F32)
        act = (gate * jax.nn.sigmoid(gate) * up).astype(BF16)
        return jnp.dot(act, wd_ref[...], preferred_element_type=F32)

    mixed = [merge(a) for a in parts]
    x1h2 = [out_proj(a, mixed[a]) for a in parts]
    if hosted is not None:
        step = pl.program_id(0) * pl.num_programs(1) + pl.program_id(1)
        _attn_sample_rows(step, *sample_in, *sample_out, *hosted)
    ffn = [swiglu(h2) for _, h2 in x1h2]
    for a in parts:
        x2 = x1h2[a][0] + mod_rows(ga2_ref, a) * ffn[a]
        y_ref[0, a * pr:(a + 1) * pr, :] = _rms(x2, fg_ref[...])


OUT_TILE = 512
OUT_HOST_VMEM_LIMIT = 62 * 1024 * 1024


def _out(x, mod, convn, os_, ls_, gn_attn, w_out, g2, w_gate, w_up, w_down, final_g, sample=None):
    nb, R, _ = x.shape
    tm = min(OUT_TILE, R)
    n_i = R // tm
    per_row = mod.shape[1] != 1
    mrows = tm if per_row else 1
    row = lambda b, i: (b, i, 0)

    def mod_spec(c):
        return pl.BlockSpec((1, mrows, D_MODEL), lambda b, i: (b, i if per_row else 0, c))

    def stream_spec(a):
        _, d, _, width = a.shape
        return pl.BlockSpec((1, d, tm // d, width), lambda b, i: (b, 0, i, 0))

    in_specs = ([pl.BlockSpec((1, tm, D_MODEL), row),
                 mod_spec(2), mod_spec(3), mod_spec(4), mod_spec(5),
                 pl.BlockSpec((1, tm, CONV_DIM), row)]
                + [stream_spec(a) for a in os_] + [stream_spec(a) for a in ls_]
                + [_const_spec((1, ATTN_DIM)), _const_spec((D_MODEL, D_MODEL)), _const_spec((1, D_MODEL)),
                   _const_spec((D_MODEL, D_FF)), _const_spec((D_MODEL, D_FF)), _const_spec((D_FF, D_MODEL)),
                   _const_spec((1, D_MODEL))])
    operands = [x, mod, mod, mod, mod, convn, *os_, *ls_, gn_attn, w_out, g2, w_gate, w_up, w_down, final_g]
    assert len(operands) == N_OUT_INPUTS
    out_specs = [pl.BlockSpec((1, tm, D_MODEL), row)]
    out_shapes = [jax.ShapeDtypeStruct(x.shape, F32)]
    hosted, limit = None, VMEM_LIMIT
    if sample is not None:
        q8, k8, v8, kt, vt, cache, bc, bn, g, t_len = sample
        assert cache.shape[0] == nb * n_i
        s_in, s_out, s_shapes = _attn_sample_specs(cache.shape, bc, bn, g, t_len, 1, lambda b, i: b * n_i + i)
        in_specs += s_in
        operands += [q8, k8, v8, kt, vt, cache, bc, bn]
        out_specs += s_out
        out_shapes += s_shapes
        hosted, limit = (cache.shape[-1], t_len), OUT_HOST_VMEM_LIMIT
    outs = pl.pallas_call(
        functools.partial(_out_kernel, hosted=hosted),
        grid=(nb, n_i),
        in_specs=in_specs, out_specs=out_specs, out_shape=out_shapes,
        scratch_shapes=[pltpu.VMEM((OUT_PARTS * MERGE_SLABS, tm // OUT_PARTS, LANES), F32)],
        compiler_params=pltpu.CompilerParams(dimension_semantics=("arbitrary", "arbitrary"),
                                             vmem_limit_bytes=limit),
        name="out",
    )(*operands)
    return outs[0] if sample is None else outs


def _buffer_to_channel_major(cache):
    B, L = cache.shape[:2]
    return cache.transpose(0, 2, 3, 4, 1).reshape(B, 2, GROUP_DIM, L)


def _buffer_from_channel_major(buf):
    B, _, _, L = buf.shape
    return buf.reshape(B, 2, GROUP_HEADS, HEAD_DIM, L).transpose(0, 4, 1, 2, 3)[None]


def kernel(x_prompt, x_sample, state_conv, cache_kv1, cache_kv2, cache_kv3, c_prompt, c_sample, w_ada, b_ada, norm1_g, norm2_g, w_in, conv_w, gn_conv, gn_attn, w_out, w_gate, w_up, w_down, rel_bias, final_g):
    B, S, _ = x_prompt.shape
    nb, t_len, _ = x_sample.shape
    rows_s = nb * t_len
    assert w_ada.shape[0] == 1 and LANES % t_len == 0 and 2 <= t_len <= 8 and rows_s % LANES == 0

    w_in_b = w_in[0].astype(BF16)
    w_out_b = w_out[0].astype(BF16)
    w_gate_b = w_gate[0].astype(BF16)
    w_up_b = w_up[0].astype(BF16)
    w_down_b = w_down[0].astype(BF16)
    final_g2 = final_g.reshape(1, D_MODEL)
    tables = [_group_table(rel_bias, g) for g in range(N_DIL)]

    c_all = jnp.concatenate([c_prompt, jnp.repeat(c_sample, t_len, axis=0)], axis=0)
    mod = _ada(c_all, w_ada[0], b_ada)
    mod_p = mod[:B].reshape(B, 1, 6 * D_MODEL)
    mod_s = mod[B:]

    xs = x_sample.reshape(rows_s, D_MODEL)
    prev0 = jnp.repeat(state_conv[0, :, 0], t_len, axis=0)
    prev1 = jnp.repeat(state_conv[0, :, 1], t_len, axis=0)
    convn_s, u_s, q_s, k_s, v_s, kt_s, vt_s = _inproj_sample(
        xs, mod_s, norm1_g, w_in_b, conv_w[0], gn_conv, prev0, prev1, t_len)
    pad8 = lambda a: jnp.pad(a.reshape(nb, t_len, ATTN_DIM), ((0, 0), (0, 8 - t_len), (0, 0)))
    q8, k8, v8 = pad8(q_s), pad8(k_s), pad8(v_s)
    sample_ops = []
    for g, (cache, (_, d)) in enumerate(zip((cache_kv1, cache_kv2, cache_kv3), DIL_PAIRS)):
        buf = _buffer_to_channel_major(cache[0])
        bc, bn = _sample_bias(tables[g], buf.shape[-1], d, t_len)
        sample_ops.append((q8, k8, v8, kt_s, vt_s, buf, bc, bn, g, t_len))

    (convn, q1, q2, q3, k1, k2, k3, v1, v2, v3,
     nconv_p, kv1_p, kv2_p, kv3_p) = _inproj_prompt(x_prompt, mod_p, norm1_g, w_in_b, conv_w[0], gn_conv)
    os_, ls_ = [], []
    for g, (qg, kg, vg) in enumerate(((q1, k1, v1), (q2, k2, v2), (q3, k3, v3))):
        _, d, M, _ = qg.shape
        merge = lambda a: a.reshape(B * d, M, GROUP_DIM)
        o, l = _attn_prompt(merge(qg), merge(kg), merge(vg), *_prompt_bias(tables[g]))
        os_.append(o.reshape(B, d, M, GROUP_DIM))
        ls_.append(l.reshape(B, d, M, LANES))
    host_g = N_DIL - 1 if B * (S // min(OUT_TILE, S)) == nb else None
    outs = _out(x_prompt, mod_p, convn, os_, ls_, gn_attn, w_out_b, norm2_g, w_gate_b, w_up_b, w_down_b,
                final_g2, sample=None if host_g is None else sample_ops[host_g])
    y_prompt = outs if host_g is None else outs[0]

    os_s, ls_s, new_bufs = [], [], []
    for g in range(N_DIL):
        o, l, nbuf = outs[1:] if g == host_g else _attn_sample(*sample_ops[g])
        os_s.append(o[:, :t_len].reshape(1, 1, rows_s, GROUP_DIM))
        ls_s.append(l[:, :t_len].reshape(1, 1, rows_s, LANES))
        new_bufs.append(_buffer_from_channel_major(nbuf))
    y_sample = _out(xs[None], mod_s[None], convn_s[None], os_s, ls_s, gn_attn, w_out_b, norm2_g,
                    w_gate_b, w_up_b, w_down_b, final_g2)
    y_sample = y_sample.reshape(nb, t_len, D_MODEL)
    nconv_s = u_s.reshape(nb, t_len, CONV_DIM)[:, t_len - 2:]

    return (y_prompt, y_sample, nconv_p[None],
            _buffer_from_channel_major(kv1_p), _buffer_from_channel_major(kv2_p),
            _buffer_from_channel_major(kv3_p),
            nconv_s[None], new_bufs[0], new_bufs[1], new_bufs[2])
```

```python
import functools

import numpy as np
import jax
import jax.numpy as jnp
from jax import lax
from jax.experimental import pallas as pl
from jax.experimental.pallas import tpu as pltpu

F32 = jnp.float32
BF16 = jnp.bfloat16

D_MODEL = 1024
HEAD_DIM = 64
CONV_DIM = 256
ATTN_DIM = 768
GROUP_HEADS = 4
GROUP_DIM = GROUP_HEADS * HEAD_DIM
DIL_PAIRS = ((128, 1), (512, 4), (2048, 16))
N_DIL = len(DIL_PAIRS)
QB = 128
N_BUCKETS = 32
MAX_DISTANCE = 2048
D_FF = 2816
PROJ_DIM = 3072
EPS = 1e-6
NEG_INF = -1e30
ATTN_SCALE = HEAD_DIM ** -0.5

LANES = 128
VMEM_LIMIT = 56 * 1024 * 1024
HOST_VMEM_LIMIT = 62 * 1024 * 1024


def _cparams(sem):
    return pltpu.CompilerParams(dimension_semantics=sem, vmem_limit_bytes=VMEM_LIMIT)


def _const_spec(shape):
    nd = len(shape)
    return pl.BlockSpec(shape, lambda *_: (0,) * nd, pipeline_mode=pl.Buffered(1))


def _head_masks(width, dtype):
    lane = lax.broadcasted_iota(jnp.int32, (1, width), 1)
    if width == GROUP_DIM:
        lane = lane // HEAD_DIM
    return [(lane == h).astype(dtype) for h in range(GROUP_HEADS)]


def _ada_kernel(c_ref, w_ref, b_ref, o_ref):
    c = c_ref[...]
    s = (c * jax.nn.sigmoid(c)).astype(BF16)
    o_ref[...] = jnp.dot(s, w_ref[...].astype(BF16), preferred_element_type=F32) + b_ref[...]


def _ada(c_all, w_ada, b_ada):
    rows = c_all.shape[0]
    tn = 1536
    return pl.pallas_call(
        _ada_kernel,
        grid=(6 * D_MODEL // tn,),
        in_specs=[pl.BlockSpec((rows, D_MODEL), lambda j: (0, 0)),
                  pl.BlockSpec((D_MODEL, tn), lambda j: (0, j)),
                  pl.BlockSpec((1, tn), lambda j: (0, j))],
        out_specs=pl.BlockSpec((rows, tn), lambda j: (0, j)),
        out_shape=jax.ShapeDtypeStruct((rows, 6 * D_MODEL), F32),
        compiler_params=_cparams(("arbitrary",)),
        name="ada",
    )(c_all, w_ada, b_ada)


def _rms(x, g):
    return x * lax.rsqrt(jnp.mean(x * x, axis=-1, keepdims=True) + EPS) * g


STREAM_SLABS = sum(GROUP_DIM // LANES for _, d in DIL_PAIRS if d > 1)
INPROJ_PARTS = 2


def _store_streams(p, out_refs, slab_ref, slab0, r0):
    rows = p.shape[0]
    for g, ((_, d), out_ref) in enumerate(zip(DIL_PAIRS, out_refs)):
        lo = g * GROUP_DIM
        if d == 1:
            out_ref[0, 0, r0:r0 + rows, :] = p[:, lo:lo + GROUP_DIM].astype(BF16)
            continue
        for s in range(GROUP_DIM // LANES):
            slab_ref[slab0 + s] = p[:, lo + s * LANES:lo + (s + 1) * LANES]
        for r in range(d):
            for s in range(GROUP_DIM // LANES):
                out_ref[0, r, r0 // d:(r0 + rows) // d, s * LANES:(s + 1) * LANES] = (
                    slab_ref[slab0 + s, pl.ds(r, rows // d, stride=d), :].astype(BF16))
        slab0 += GROUP_DIM // LANES


N_INPROJ_INPUTS = 7
N_INPROJ_OUTPUTS = 14
N_SAMPLE_INPUTS = 8
N_SAMPLE_OUTPUTS = 3


def _inproj_prompt_kernel(*refs, tm, hosted):
    n_in = N_INPROJ_INPUTS + N_SAMPLE_INPUTS * len(hosted)
    x_ref, sh_ref, sc_ref, g1_ref, w_ref, cw_ref, gnc_ref = refs[:N_INPROJ_INPUTS]
    (convn_ref, q1_ref, q2_ref, q3_ref, k1_ref, k2_ref, k3_ref, v1_ref, v2_ref, v3_ref,
     nconv_ref, kv1_ref, kv2_ref, kv3_ref) = refs[n_in:n_in + N_INPROJ_OUTPUTS]
    ubuf_ref, slab_ref = refs[-2:]
    i = pl.program_id(1)
    n = pl.num_programs(1)
    pr = tm // INPROJ_PARTS
    parts = range(INPROJ_PARTS)
    c0 = 3 * CONV_DIM
    cw = cw_ref[...]

    hbs = [(_rms(x_ref[0, a * pr:(a + 1) * pr, :], g1_ref[...]) * (1.0 + sc_ref[0]) + sh_ref[0]).astype(BF16)
           for a in parts]
    projs = [jnp.dot(hb, w_ref[...], preferred_element_type=F32) for hb in hbs]

    for j, dims in enumerate(hosted):
        s_in = refs[N_INPROJ_INPUTS + j * N_SAMPLE_INPUTS:N_INPROJ_INPUTS + (j + 1) * N_SAMPLE_INPUTS]
        s_out = refs[n_in + N_INPROJ_OUTPUTS + j * N_SAMPLE_OUTPUTS:
                     n_in + N_INPROJ_OUTPUTS + (j + 1) * N_SAMPLE_OUTPUTS]
        _attn_sample_rows(pl.program_id(0) * n + i, *s_in, *s_out, *dims)

    tail = ubuf_ref[tm:tm + 8, :]
    ubuf_ref[0:8, :] = jnp.where(i == 0, jnp.zeros_like(tail), tail)
    for a in parts:
        r0 = a * pr
        p = projs[a]
        gb = p[:, 0:CONV_DIM]
        u = p[:, CONV_DIM:2 * CONV_DIM] * p[:, 2 * CONV_DIM:c0]
        ubuf_ref[8 + r0:8 + r0 + pr, :] = u
        um1 = ubuf_ref[7 + r0:7 + r0 + pr, :]
        um2 = ubuf_ref[6 + r0:6 + r0 + pr, :]
        z = cw[0:1, :] * um2 + cw[1:2, :] * um1 + cw[2:3, :] * u
        convn_ref[0, r0:r0 + pr, :] = _rms(gb * z, gnc_ref[...]).astype(BF16)
        slab0 = a * 3 * STREAM_SLABS
        _store_streams(p[:, c0:c0 + ATTN_DIM] * ATTN_SCALE, (q1_ref, q2_ref, q3_ref), slab_ref, slab0, r0)
        _store_streams(p[:, c0 + ATTN_DIM:c0 + 2 * ATTN_DIM], (k1_ref, k2_ref, k3_ref), slab_ref,
                       slab0 + STREAM_SLABS, r0)
        _store_streams(p[:, c0 + 2 * ATTN_DIM:c0 + 3 * ATTN_DIM], (v1_ref, v2_ref, v3_ref), slab_ref,
                       slab0 + 2 * STREAM_SLABS, r0)
    nconv_ref[0] = ubuf_ref[tm + 6:tm + 8, :]

    def put(ref, g, rows):
        for a in parts:
            lo_r, hi_r = max(tm - rows, a * pr), (a + 1) * pr
            if hi_r <= lo_r:
                continue
            for kv in range(2):
                lo = c0 + (1 + kv) * ATTN_DIM + g * GROUP_DIM
                ref[0, kv, :, lo_r - (tm - rows):hi_r - (tm - rows)] = (
                    projs[a][lo_r - a * pr:hi_r - a * pr, lo:lo + GROUP_DIM].T)

    w3 = DIL_PAIRS[2][0]

    @pl.when(i >= n - w3 // tm)
    def _():
        put(kv3_ref, 2, tm)

    @pl.when(i == n - 1)
    def _():
        put(kv2_ref, 1, DIL_PAIRS[1][0])
        put(kv1_ref, 0, DIL_PAIRS[0][0])


INPROJ_TILE = 1024


def _inproj_prompt(x, mod, g1, w_in, conv_w, gn_conv, samples=()):
    B, S, _ = x.shape
    tm = INPROJ_TILE
    n = S // tm
    w1, w2, w3 = (w for w, _ in DIL_PAIRS)
    assert S % tm == 0 and w3 % tm == 0 and tm >= w2 and tm % (INPROJ_PARTS * LANES) == 0
    n3 = w3 // tm
    row = lambda b, i: (b, i, 0)
    stream_specs = [pl.BlockSpec((1, d, tm // d, GROUP_DIM), lambda b, i: (b, 0, i, 0)) for _, d in DIL_PAIRS]
    stream_shapes = [jax.ShapeDtypeStruct((B, d, S // d, GROUP_DIM), BF16) for _, d in DIL_PAIRS]
    in_specs = [pl.BlockSpec((1, tm, D_MODEL), row),
                pl.BlockSpec((1, 1, D_MODEL), lambda b, i: (b, 0, 0)),
                pl.BlockSpec((1, 1, D_MODEL), lambda b, i: (b, 0, 1)),
                _const_spec((1, D_MODEL)),
                _const_spec((D_MODEL, PROJ_DIM)),
                _const_spec((3, CONV_DIM)),
                _const_spec((1, CONV_DIM))]
    out_specs = [pl.BlockSpec((1, tm, CONV_DIM), row)] + stream_specs * 3 + [
        pl.BlockSpec((1, 2, CONV_DIM), lambda b, i: (b, 0, 0)),
        pl.BlockSpec((1, 2, GROUP_DIM, w1), lambda b, i: (b, 0, 0, 0)),
        pl.BlockSpec((1, 2, GROUP_DIM, w2), lambda b, i: (b, 0, 0, 0)),
        pl.BlockSpec((1, 2, GROUP_DIM, tm), lambda b, i: (b, 0, 0, jnp.maximum(i - (n - n3), 0)))]
    out_shapes = [jax.ShapeDtypeStruct((B, S, CONV_DIM), BF16)] + stream_shapes * 3 + [
        jax.ShapeDtypeStruct((B, 2, CONV_DIM), F32),
        jax.ShapeDtypeStruct((B, 2, GROUP_DIM, w1), F32),
        jax.ShapeDtypeStruct((B, 2, GROUP_DIM, w2), F32),
        jax.ShapeDtypeStruct((B, 2, GROUP_DIM, w3), F32)]
    operands = [x, mod, mod, g1, w_in, conv_w, gn_conv]
    assert len(operands) == N_INPROJ_INPUTS and len(out_specs) == N_INPROJ_OUTPUTS
    hosted = []
    for q8, k8, v8, kt, vt, cache, bc, bn, g, t_len in samples:
        nbb = cache.shape[0] // (B * n)
        assert nbb * B * n == cache.shape[0]
        s_in, s_out, s_shapes = _attn_sample_specs(cache.shape, bc, bn, g, t_len, nbb, lambda b, i: b * n + i)
        in_specs += s_in
        operands += [q8, k8, v8, kt, vt, cache, bc, bn]
        out_specs += s_out
        out_shapes += s_shapes
        hosted.append((cache.shape[-1], t_len))
    return pl.pallas_call(
        functools.partial(_inproj_prompt_kernel, tm=tm, hosted=tuple(hosted)),
        grid=(B, n),
        in_specs=in_specs, out_specs=out_specs, out_shape=out_shapes,
        scratch_shapes=[pltpu.VMEM((tm + 8, CONV_DIM), F32),
                        pltpu.VMEM((INPROJ_PARTS * 3 * STREAM_SLABS, tm // INPROJ_PARTS, LANES), F32)],
        compiler_params=pltpu.CompilerParams(dimension_semantics=("arbitrary", "arbitrary"),
                                             vmem_limit_bytes=HOST_VMEM_LIMIT if hosted else VMEM_LIMIT),
        name="inproj_prompt",
    )(*operands)


def _inproj_sample_kernel(x_ref, sh_ref, sc_ref, g1_ref, w_ref, cw_ref, gnc_ref, p0_ref, p1_ref,
                          convn_ref, u_ref, q_ref, k_ref, v_ref, kt_ref, vt_ref, ubuf_ref, *, tm, t_len):
    h = _rms(x_ref[...], g1_ref[...]) * (1.0 + sc_ref[...]) + sh_ref[...]
    hb = h.astype(BF16)

    pc = jnp.dot(hb, w_ref[:, 0:3 * CONV_DIM], preferred_element_type=F32)
    gb = pc[:, 0:CONV_DIM]
    u = pc[:, CONV_DIM:2 * CONV_DIM] * pc[:, 2 * CONV_DIM:3 * CONV_DIM]
    u_ref[...] = u

    ubuf_ref[0:8, :] = jnp.zeros((8, CONV_DIM), F32)
    ubuf_ref[8:tm + 8, :] = u
    t = lax.broadcasted_iota(jnp.int32, (tm, 1), 0) % t_len
    um1 = jnp.where(t == 0, p1_ref[...], ubuf_ref[7:tm + 7, :])
    um2 = jnp.where(t == 0, p0_ref[...], jnp.where(t == 1, p1_ref[...], ubuf_ref[6:tm + 6, :]))
    cw = cw_ref[...]
    z = cw[0:1, :] * um2 + cw[1:2, :] * um1 + cw[2:3, :] * u
    convn_ref[...] = _rms(gb * z, gnc_ref[...]).astype(BF16)

    c0 = 3 * CONV_DIM
    pq = jnp.dot(hb, w_ref[:, c0:c0 + ATTN_DIM], preferred_element_type=F32)
    q_ref[...] = pq * ATTN_SCALE
    pk = jnp.dot(hb, w_ref[:, c0 + ATTN_DIM:c0 + 2 * ATTN_DIM], preferred_element_type=F32)
    k_ref[...] = pk
    kt_ref[...] = pk.T
    pv = jnp.dot(hb, w_ref[:, c0 + 2 * ATTN_DIM:c0 + 3 * ATTN_DIM], preferred_element_type=F32)
    v_ref[...] = pv
    vt_ref[...] = pv.T


def _inproj_sample(x, mod, g1, w_in, conv_w, gn_conv, prev0, prev1, t_len):
    tm = x.shape[0]
    full = lambda shape: pl.BlockSpec(shape, lambda i: (0,) * len(shape))
    return pl.pallas_call(
        functools.partial(_inproj_sample_kernel, tm=tm, t_len=t_len),
        grid=(1,),
        in_specs=[full((tm, D_MODEL)),
                  pl.BlockSpec((tm, D_MODEL), lambda i: (0, 0)),
                  pl.BlockSpec((tm, D_MODEL), lambda i: (0, 1)),
                  full((1, D_MODEL)), full((D_MODEL, PROJ_DIM)), full((3, CONV_DIM)), full((1, CONV_DIM)),
                  full((tm, CONV_DIM)), full((tm, CONV_DIM))],
        out_specs=[full((tm, CONV_DIM)), full((tm, CONV_DIM)),
                   full((tm, ATTN_DIM)), full((tm, ATTN_DIM)), full((tm, ATTN_DIM)),
                   full((ATTN_DIM, tm)), full((ATTN_DIM, tm))],
        out_shape=[jax.ShapeDtypeStruct((tm, CONV_DIM), BF16),
                   jax.ShapeDtypeStruct((tm, CONV_DIM), F32),
                   jax.ShapeDtypeStruct((tm, ATTN_DIM), F32),
                   jax.ShapeDtypeStruct((tm, ATTN_DIM), F32),
                   jax.ShapeDtypeStruct((tm, ATTN_DIM), F32),
                   jax.ShapeDtypeStruct((ATTN_DIM, tm), F32),
                   jax.ShapeDtypeStruct((ATTN_DIM, tm), F32)],
        scratch_shapes=[pltpu.VMEM((tm + 8, CONV_DIM), F32)],
        compiler_params=_cparams(("arbitrary",)),
        name="inproj_sample",
    )(x, mod, mod, g1, w_in, conv_w, gn_conv, prev0, prev1)


def _t5_buckets(dist):
    dist = np.asarray(dist, np.int32)
    max_exact = N_BUCKETS // 2
    large = max_exact + (np.log(np.maximum(dist, 1).astype(np.float32) / max_exact)
                         / np.log(MAX_DISTANCE / max_exact) * (N_BUCKETS - max_exact)).astype(np.int32)
    large = np.minimum(large, N_BUCKETS - 1)
    return np.where(dist < max_exact, dist, large).astype(np.int32)


def _group_table(rel_bias, g):
    w, d = DIL_PAIRS[g]
    buckets = _t5_buckets(np.arange(w // d + 1) * d)
    return rel_bias[buckets][:, g * GROUP_HEADS:(g + 1) * GROUP_HEADS].T.astype(F32)


def _prompt_bias(table):
    period = 3 * QB - 1
    base = jnp.concatenate([table[:, ::-1], jnp.full((GROUP_HEADS, period - QB - 1), NEG_INF, F32)], axis=1)
    skew = jnp.tile(base, (1, QB))[:, :QB * (period - 1)].reshape(GROUP_HEADS, QB, period - 1)
    prev, cur = skew[:, :, 0:QB], skew[:, :, QB:2 * QB]
    tri = (np.arange(QB)[None, :] <= np.arange(QB)[:, None])[None]
    fold = jnp.stack([jnp.where(tri, cur, prev), jnp.where(tri, cur, NEG_INF)])
    fold = fold.transpose(0, 3, 1, 2).reshape(2, QB, GROUP_HEADS * QB)
    lone = jnp.broadcast_to(table[:, QB:QB + 1], (GROUP_HEADS, QB)).reshape(1, 1, GROUP_HEADS * QB)
    return fold, jnp.concatenate([lone, jnp.full_like(lone, NEG_INF)], axis=0)


def _sample_bias(table, buf_len, d, t_len):
    assert buf_len == QB * d
    asc = table[:, ::-1]
    inter = jnp.concatenate([asc[:, :, None], jnp.full((GROUP_HEADS, QB + 1, d - 1), NEG_INF, F32)],
                            axis=2).reshape(GROUP_HEADS, (QB + 1) * d)
    ext_len = buf_len + LANES
    rows = []
    for t in range(8):
        lead = t if t < t_len else 0
        rows.append(jnp.pad(inter, ((0, 0), (lead, ext_len - lead - (QB + 1) * d)), constant_values=NEG_INF))
    ext = jnp.stack(rows, axis=1)
    return (ext[:, :, :buf_len].reshape(GROUP_HEADS * 8, buf_len),
            ext[:, :, buf_len:].reshape(GROUP_HEADS * 8, LANES))


def _attn_prompt_kernel(q_ref, kc_ref, vc_ref, kp_ref, vp_ref, bias_ref, dbias_ref, o_ref, lse_ref):
    i = pl.program_id(1)
    sb, tq, _ = q_ref.shape
    hm_b = _head_masks(GROUP_DIM, BF16)
    first = jnp.where(i == 0, 1, 0)
    c_idx = lax.broadcasted_iota(jnp.int32, (QB, GROUP_HEADS * QB), 0)
    q_idx = lax.broadcasted_iota(jnp.int32, (QB, GROUP_HEADS * QB), 1) & (QB - 1)
    tri = c_idx <= q_idx
    tri_f = tri.astype(F32)
    diag_f = (c_idx == q_idx).astype(F32)
    work = [(s, j) for s in range(sb) for j in range(tq // QB)]

    def scores(s, j):
        q = q_ref[s, j * QB:(j + 1) * QB, :]
        q4 = jnp.concatenate([q * hm_b[h] for h in range(GROUP_HEADS)], axis=0)
        if j == 0:
            kk = jnp.concatenate([kp_ref[s], kc_ref[s, 0:QB, :]], axis=0)
        else:
            kk = kc_ref[s, (j - 1) * QB:(j + 1) * QB, :]
        return lax.dot_general(kk, q4, (((1,), (1,)), ((), ())), preferred_element_type=F32)

    def values_t(s, j):
        if j == 0:
            return jnp.concatenate([vp_ref[s].T, vc_ref[s, 0:QB, :].T], axis=1)
        return vc_ref[s, (j - 1) * QB:(j + 1) * QB, :].T

    st_next = scores(*work[0])
    for n, (s, j) in enumerate(work):
        st = st_next
        if n + 1 < len(work):
            st_next = scores(*work[n + 1])
        slot = first if j == 0 else 0
        sp = st[0:QB]
        sc = st[QB:2 * QB]
        w = jnp.where(tri, sc, sp) + bias_ref[slot]
        sd = jnp.sum(sp * diag_f, axis=0, keepdims=True) + dbias_ref[slot]
        m = jnp.maximum(jnp.max(w, axis=0, keepdims=True), sd)
        e = jnp.exp(w - m)
        ed = jnp.exp(sd - m)
        den = jnp.sum(e, axis=0, keepdims=True) + ed
        lse = m + jnp.log(den)
        rden = 1.0 / den
        p_cur = e * tri_f
        pt = jnp.concatenate([e - p_cur, p_cur], axis=0).astype(BF16)
        ed_r = ed.astype(BF16).astype(F32)
        vt = values_t(s, j)
        ots = []
        for h in range(GROUP_HEADS):
            cols = slice(h * QB, (h + 1) * QB)
            rows = slice(h * HEAD_DIM, (h + 1) * HEAD_DIM)
            ot = jnp.dot(vt[rows, :], pt[:, cols], preferred_element_type=F32)
            ot = ot + vt[rows, 0:QB].astype(F32) * ed_r[:, cols]
            ots.append(ot * rden[:, cols])
        o_ref[s, j * QB:(j + 1) * QB, :] = jnp.concatenate(ots, axis=0).T
        lrows = jnp.concatenate([lse[:, h * QB:(h + 1) * QB] for h in range(GROUP_HEADS)]
                                + [jnp.zeros((LANES - GROUP_HEADS, QB), F32)], axis=0)
        lse_ref[s, j * QB:(j + 1) * QB, :] = lrows.T


def _attn_prompt(q, k, v, bias, dbias, block_rows=2048):
    ns, M, _ = q.shape
    tq = min(block_rows, M)
    sb = block_rows // tq
    r = tq // QB
    cur = pl.BlockSpec((sb, tq, GROUP_DIM), lambda s, i: (s, i, 0))
    prev = pl.BlockSpec((sb, QB, GROUP_DIM), lambda s, i: (s, jnp.maximum(i * r - 1, 0), 0))
    return pl.pallas_call(
        _attn_prompt_kernel,
        grid=(ns // sb, M // tq),
        in_specs=[cur, cur, cur, prev, prev, _const_spec(bias.shape), _const_spec(dbias.shape)],
        out_specs=[pl.BlockSpec((sb, tq, GROUP_DIM), lambda s, i: (s, i, 0)),
                   pl.BlockSpec((sb, tq, LANES), lambda s, i: (s, i, 0))],
        out_shape=[jax.ShapeDtypeStruct((ns, M, GROUP_DIM), F32),
                   jax.ShapeDtypeStruct((ns, M, LANES), F32)],
        compiler_params=_cparams(("arbitrary", "arbitrary")),
        name="attn_prompt",
    )(q, k, v, k, v, bias, dbias)


def _attn_sample_kernel(q_ref, k8_ref, v8_ref, kt_ref, vt_ref, cache_ref, bc_ref, bn_ref,
                        o_ref, lse_ref, newc_ref, *, buf_len, t_len):
    _attn_sample_rows(pl.program_id(0), q_ref, k8_ref, v8_ref, kt_ref, vt_ref, cache_ref, bc_ref, bn_ref,
                      o_ref, lse_ref, newc_ref, buf_len, t_len)


def _attn_sample_rows(step, q_ref, k8_ref, v8_ref, kt_ref, vt_ref, cache_ref, bc_ref, bn_ref,
                      o_ref, lse_ref, newc_ref, buf_len, t_len):
    rows = range(q_ref.shape[0])
    hm_f = _head_masks(GROUP_DIM, F32)
    lm_f = _head_masks(LANES, F32)
    lane = lax.broadcasted_iota(jnp.int32, (1, LANES), 1)

    q4bs = [jnp.concatenate([q_ref[i] * hm_f[h] for h in range(GROUP_HEADS)], axis=0).astype(BF16)
            for i in rows]
    s_cs = [jnp.dot(q4bs[i], cache_ref[i, 0].astype(BF16), preferred_element_type=F32) + bc_ref[...]
            for i in rows]

    stats = []
    for i in rows:
        q4r = q4bs[i].astype(F32)
        k8r = k8_ref[i].astype(BF16).astype(F32)
        s_n = bn_ref[...]
        for t in range(t_len):
            col = jnp.sum(q4r * k8r[t:t + 1, :], axis=-1, keepdims=True)
            s_n = s_n + jnp.where(lane == t, col, 0.0)
        m = jnp.maximum(jnp.max(s_cs[i], axis=-1, keepdims=True), jnp.max(s_n, axis=-1, keepdims=True))
        e_c = jnp.exp(s_cs[i] - m)
        e_n = jnp.exp(s_n - m)
        den = jnp.sum(e_c, axis=-1, keepdims=True) + jnp.sum(e_n, axis=-1, keepdims=True)
        stats.append((m, e_c, e_n, den))

    pvs = [lax.dot_general(stats[i][1].astype(BF16), cache_ref[i, 1].astype(BF16), (((1,), (1,)), ((), ())),
                           preferred_element_type=F32) for i in rows]
    for i in rows:
        m, _, e_n, den = stats[i]
        pv = pvs[i]
        v8r = v8_ref[i].astype(BF16).astype(F32)
        e_nr = e_n.astype(BF16).astype(F32)
        for t in range(t_len):
            w_t = jnp.sum(jnp.where(lane == t, e_nr, 0.0), axis=-1, keepdims=True)
            pv = pv + w_t * v8r[t:t + 1, :]
        pv = pv / den
        lse = m + jnp.log(den)
        o = jnp.zeros((8, GROUP_DIM), F32)
        l = jnp.zeros((8, LANES), F32)
        for h in range(GROUP_HEADS):
            o = o + pv[h * 8:(h + 1) * 8, :] * hm_f[h]
            l = l + lse[h * 8:(h + 1) * 8, :] * lm_f[h]
        o_ref[i] = o
        lse_ref[i] = l

    keep = LANES - t_len
    per_block = LANES // t_len
    for i in rows:
        shift_new = keep - ((step * len(rows) + i) % per_block) * t_len
        for kv, new_ref in ((0, kt_ref), (1, vt_ref)):
            new_r = pltpu.roll(new_ref[...], shift_new, 1)
            prev_r = None
            for c in range(buf_len // LANES):
                r_c = pltpu.roll(cache_ref[i, kv, :, c * LANES:(c + 1) * LANES], keep, 1)
                if c > 0:
                    newc_ref[i, kv, :, (c - 1) * LANES:c * LANES] = jnp.where(lane < keep, prev_r, r_c)
                prev_r = r_c
            newc_ref[i, kv, :, buf_len - LANES:buf_len] = jnp.where(lane < keep, prev_r, new_r)


SAMPLE_STEP_POSITIONS = 2048
SAMPLE_STEP_ROWS = 8


def _attn_sample_specs(cache_shape, bc, bn, g, t_len, nbb, step_of):
    nb, _, _, buf_len = cache_shape
    per_block = LANES // t_len
    assert nb % nbb == 0 and per_block % nbb == 0
    new_spec = pl.BlockSpec((nbb, 8, GROUP_DIM), lambda *i: (step_of(*i), 0, g))
    newt_spec = pl.BlockSpec((GROUP_DIM, LANES), lambda *i: (g, (step_of(*i) * nbb) // per_block))
    cache_spec = pl.BlockSpec((nbb, 2, GROUP_DIM, buf_len), lambda *i: (step_of(*i), 0, 0, 0))
    in_specs = [new_spec, new_spec, new_spec, newt_spec, newt_spec, cache_spec,
                _const_spec(bc.shape), _const_spec(bn.shape)]
    out_specs = [pl.BlockSpec((nbb, 8, GROUP_DIM), lambda *i: (step_of(*i), 0, 0)),
                 pl.BlockSpec((nbb, 8, LANES), lambda *i: (step_of(*i), 0, 0)),
                 cache_spec]
    out_shapes = [jax.ShapeDtypeStruct((nb, 8, GROUP_DIM), F32),
                  jax.ShapeDtypeStruct((nb, 8, LANES), F32),
                  jax.ShapeDtypeStruct(cache_shape, F32)]
    return in_specs, out_specs, out_shapes


def _attn_sample(q8, k8, v8, kt, vt, cache, bc, bn, g, t_len):
    nb, _, _, buf_len = cache.shape
    nbb = max(1, min(LANES // t_len, SAMPLE_STEP_ROWS, SAMPLE_STEP_POSITIONS // buf_len))
    in_specs, out_specs, out_shapes = _attn_sample_specs(cache.shape, bc, bn, g, t_len, nbb, lambda b: b)
    return pl.pallas_call(
        functools.partial(_attn_sample_kernel, buf_len=buf_len, t_len=t_len),
        grid=(nb // nbb,),
        in_specs=in_specs, out_specs=out_specs, out_shape=out_shapes,
        compiler_params=_cparams(("arbitrary",)),
        name="attn_sample",
    )(q8, k8, v8, kt, vt, cache, bc, bn)


OUT_PARTS = 2
MERGE_SLABS = N_DIL * (GROUP_DIM + LANES) // LANES


def _load_positions(ref, slab_ref, slab0, r0, rows):
    _, d, _, width = ref.shape
    if d == 1:
        return ref[0, 0, r0:r0 + rows, :]
    for r in range(d):
        for s in range(width // LANES):
            slab_ref[slab0 + s, pl.ds(r, rows // d, stride=d), :] = (
                ref[0, r, r0 // d:(r0 + rows) // d, s * LANES:(s + 1) * LANES])
    return jnp.concatenate([slab_ref[slab0 + s] for s in range(width // LANES)], axis=-1)


N_OUT_INPUTS = 19


def _out_kernel(*refs, hosted):
    (x_ref, ga1_ref, sh2_ref, sc2_ref, ga2_ref, convn_ref, o1_ref, o2_ref, o3_ref, l1_ref, l2_ref, l3_ref,
     gna_ref, wout_ref, g2_ref, wg_ref, wu_ref, wd_ref, fg_ref) = refs[:N_OUT_INPUTS]
    if hosted is None:
        y_ref, slab_ref = refs[N_OUT_INPUTS:]
    else:
        sample_in = refs[N_OUT_INPUTS:N_OUT_INPUTS + N_SAMPLE_INPUTS]
        y_ref, *sample_out, slab_ref = refs[N_OUT_INPUTS + N_SAMPLE_INPUTS:]
    tm = x_ref.shape[1]
    pr = tm // OUT_PARTS
    parts = range(OUT_PARTS)
    hm_f = _head_masks(GROUP_DIM, F32)
    lm_f = _head_masks(LANES, F32)
    gna = gna_ref[...]

    def mod_rows(ref, a):
        return ref[0] if ref.shape[1] == 1 else ref[0, a * pr:(a + 1) * pr, :]

    def merge(a):
        r0 = a * pr
        slab0 = a * MERGE_SLABS
        ls = [_load_positions(l_ref, slab_ref, slab0 + g, r0, pr)
              for g, l_ref in enumerate((l1_ref, l2_ref, l3_ref))]
        m = jnp.maximum(jnp.maximum(ls[0], ls[1]), ls[2])
        es = [jnp.exp(l - m) for l in ls]
        tot = es[0] + es[1] + es[2]
        aos = []
        ssq = jnp.zeros((pr, 1), F32)
        for g, o_ref in enumerate((o1_ref, o2_ref, o3_ref)):
            alpha = es[g] / tot
            wide = jnp.zeros((pr, GROUP_DIM), F32)
            for h in range(GROUP_HEADS):
                a_h = jnp.sum(alpha * lm_f[h], axis=-1, keepdims=True)
                wide = wide + a_h * hm_f[h]
            ao = wide * _load_positions(o_ref, slab_ref, slab0 + N_DIL + g * (GROUP_DIM // LANES), r0, pr)
            ssq = ssq + jnp.sum(ao * ao, axis=-1, keepdims=True)
            aos.append(ao)
        rinv = lax.rsqrt(ssq / ATTN_DIM + EPS)
        return jnp.concatenate(
            [convn_ref[0, r0:r0 + pr, :]]
            + [(aos[g] * rinv * gna[:, g * GROUP_DIM:(g + 1) * GROUP_DIM]).astype(BF16) for g in range(N_DIL)],
            axis=-1)

    def out_proj(a, mixed):
        mix = jnp.dot(mixed, wout_ref[...], preferred_element_type=F32)
        x1 = x_ref[0, a * pr:(a + 1) * pr, :] + mod_rows(ga1_ref, a) * mix
        h2 = (_rms(x1, g2_ref[...]) * (1.0 + mod_rows(sc2_ref, a)) + mod_rows(sh2_ref, a)).astype(BF16)
        return x1, h2

    def swiglu(h2):
        gate = jnp.dot(h2, wg_ref[...], preferred_element_type=F32)
        up = jnp.dot(h2, wu_ref[...], preferred_element_type=F32)
        act = (gate * jax.nn.sigmoid(gate) * up).astype(BF16)
        return jnp.dot(act, wd_ref[...], preferred_element_type=F32)

    mixed = [merge(a) for a in parts]
    x1h2 = [out_proj(a, mixed[a]) for a in parts]
    if hosted is not None:
        step = pl.program_id(0) * pl.num_programs(1) + pl.program_id(1)
        _attn_sample_rows(step, *sample_in, *sample_out, *hosted)
    ffn = [swiglu(h2) for _, h2 in x1h2]
    for a in parts:
        x2 = x1h2[a][0] + mod_rows(ga2_ref, a) * ffn[a]
        y_ref[0, a * pr:(a + 1) * pr, :] = _rms(x2, fg_ref[...])


OUT_TILE = 512


def _out(x, mod, convn, os_, ls_, gn_attn, w_out, g2, w_gate, w_up, w_down, final_g, sample=None):
    nb, R, _ = x.shape
    tm = min(OUT_TILE, R)
    n_i = R // tm
    per_row = mod.shape[1] != 1
    mrows = tm if per_row else 1
    row = lambda b, i: (b, i, 0)

    def mod_spec(c):
        return pl.BlockSpec((1, mrows, D_MODEL), lambda b, i: (b, i if per_row else 0, c))

    def stream_spec(a):
        _, d, _, width = a.shape
        return pl.BlockSpec((1, d, tm // d, width), lambda b, i: (b, 0, i, 0))

    in_specs = ([pl.BlockSpec((1, tm, D_MODEL), row),
                 mod_spec(2), mod_spec(3), mod_spec(4), mod_spec(5),
                 pl.BlockSpec((1, tm, CONV_DIM), row)]
                + [stream_spec(a) for a in os_] + [stream_spec(a) for a in ls_]
                + [_const_spec((1, ATTN_DIM)), _const_spec((D_MODEL, D_MODEL)), _const_spec((1, D_MODEL)),
                   _const_spec((D_MODEL, D_FF)), _const_spec((D_MODEL, D_FF)), _const_spec((D_FF, D_MODEL)),
                   _const_spec((1, D_MODEL))])
    operands = [x, mod, mod, mod, mod, convn, *os_, *ls_, gn_attn, w_out, g2, w_gate, w_up, w_down, final_g]
    assert len(operands) == N_OUT_INPUTS
    out_specs = [pl.BlockSpec((1, tm, D_MODEL), row)]
    out_shapes = [jax.ShapeDtypeStruct(x.shape, F32)]
    hosted, limit = None, VMEM_LIMIT
    if sample is not None:
        q8, k8, v8, kt, vt, cache, bc, bn, g, t_len = sample
        assert cache.shape[0] == nb * n_i
        s_in, s_out, s_shapes = _attn_sample_specs(cache.shape, bc, bn, g, t_len, 1, lambda b, i: b * n_i + i)
        in_specs += s_in
        operands += [q8, k8, v8, kt, vt, cache, bc, bn]
        out_specs += s_out
        out_shapes += s_shapes
        hosted, limit = (cache.shape[-1], t_len), HOST_VMEM_LIMIT
    outs = pl.pallas_call(
        functools.partial(_out_kernel, hosted=hosted),
        grid=(nb, n_i),
        in_specs=in_specs, out_specs=out_specs, out_shape=out_shapes,
        scratch_shapes=[pltpu.VMEM((OUT_PARTS * MERGE_SLABS, tm // OUT_PARTS, LANES), F32)],
        compiler_params=pltpu.CompilerParams(dimension_semantics=("arbitrary", "arbitrary"),
                                             vmem_limit_bytes=limit),
        name="out",
    )(*operands)
    return outs[0] if sample is None else outs


def _buffer_to_channel_major(cache):
    B, L = cache.shape[:2]
    return cache.transpose(0, 2, 3, 4, 1).reshape(B, 2, GROUP_DIM, L)


def _buffer_from_channel_major(buf):
    B, _, _, L = buf.shape
    return buf.reshape(B, 2, GROUP_HEADS, HEAD_DIM, L).transpose(0, 4, 1, 2, 3)[None]


def kernel(x_prompt, x_sample, state_conv, cache_kv1, cache_kv2, cache_kv3, c_prompt, c_sample, w_ada, b_ada, norm1_g, norm2_g, w_in, conv_w, gn_conv, gn_attn, w_out, w_gate, w_up, w_down, rel_bias, final_g):
    B, S, _ = x_prompt.shape
    nb, t_len, _ = x_sample.shape
    rows_s = nb * t_len
    assert w_ada.shape[0] == 1 and LANES % t_len == 0 and 2 <= t_len <= 8 and rows_s % LANES == 0

    w_in_b = w_in[0].astype(BF16)
    w_out_b = w_out[0].astype(BF16)
    w_gate_b = w_gate[0].astype(BF16)
    w_up_b = w_up[0].astype(BF16)
    w_down_b = w_down[0].astype(BF16)
    final_g2 = final_g.reshape(1, D_MODEL)
    tables = [_group_table(rel_bias, g) for g in range(N_DIL)]

    c_all = jnp.concatenate([c_prompt, jnp.repeat(c_sample, t_len, axis=0)], axis=0)
    mod = _ada(c_all, w_ada[0], b_ada)
    mod_p = mod[:B].reshape(B, 1, 6 * D_MODEL)
    mod_s = mod[B:]

    xs = x_sample.reshape(rows_s, D_MODEL)
    prev0 = jnp.repeat(state_conv[0, :, 0], t_len, axis=0)
    prev1 = jnp.repeat(state_conv[0, :, 1], t_len, axis=0)
    convn_s, u_s, q_s, k_s, v_s, kt_s, vt_s = _inproj_sample(
        xs, mod_s, norm1_g, w_in_b, conv_w[0], gn_conv, prev0, prev1, t_len)
    pad8 = lambda a: jnp.pad(a.reshape(nb, t_len, ATTN_DIM), ((0, 0), (0, 8 - t_len), (0, 0)))
    q8, k8, v8 = pad8(q_s), pad8(k_s), pad8(v_s)
    sample_ops = []
    for g, (cache, (_, d)) in enumerate(zip((cache_kv1, cache_kv2, cache_kv3), DIL_PAIRS)):
        buf = _buffer_to_channel_major(cache[0])
        bc, bn = _sample_bias(tables[g], buf.shape[-1], d, t_len)
        sample_ops.append((q8, k8, v8, kt_s, vt_s, buf, bc, bn, g, t_len))

    host_out = N_DIL - 1 if B * (S // min(OUT_TILE, S)) == nb else None
    inproj_steps = B * (S // INPROJ_TILE)
    host_in = [g for g in range(N_DIL) if g != host_out and nb % inproj_steps == 0
               and (LANES // t_len) % (nb // inproj_steps) == 0]
    sample_res = {}
    (convn, q1, q2, q3, k1, k2, k3, v1, v2, v3, nconv_p, kv1_p, kv2_p, kv3_p, *hosted_res) = _inproj_prompt(
        x_prompt, mod_p, norm1_g, w_in_b, conv_w[0], gn_conv, samples=[sample_ops[g] for g in host_in])
    for j, g in enumerate(host_in):
        sample_res[g] = hosted_res[j * N_SAMPLE_OUTPUTS:(j + 1) * N_SAMPLE_OUTPUTS]
    os_, ls_ = [], []
    for g, (qg, kg, vg) in enumerate(((q1, k1, v1), (q2, k2, v2), (q3, k3, v3))):
        _, d, M, _ = qg.shape
        merge = lambda a: a.reshape(B * d, M, GROUP_DIM)
        o, l = _attn_prompt(merge(qg), merge(kg), merge(vg), *_prompt_bias(tables[g]))
        os_.append(o.reshape(B, d, M, GROUP_DIM))
        ls_.append(l.reshape(B, d, M, LANES))
    outs = _out(x_prompt, mod_p, convn, os_, ls_, gn_attn, w_out_b, norm2_g, w_gate_b, w_up_b, w_down_b,
                final_g2, sample=None if host_out is None else sample_ops[host_out])
    y_prompt = outs
    if host_out is not None:
        y_prompt, *sample_res[host_out] = outs

    os_s, ls_s, new_bufs = [], [], []
    for g in range(N_DIL):
        o, l, nbuf = sample_res[g] if g in sample_res else _attn_sample(*sample_ops[g])
        os_s.append(o[:, :t_len].reshape(1, 1, rows_s, GROUP_DIM))
        ls_s.append(l[:, :t_len].reshape(1, 1, rows_s, LANES))
        new_bufs.append(_buffer_from_channel_major(nbuf))
    y_sample = _out(xs[None], mod_s[None], convn_s[None], os_s, ls_s, gn_attn, w_out_b, norm2_g,
                    w_gate_b, w_up_b, w_down_b, final_g2)
    y_sample = y_sample.reshape(nb, t_len, D_MODEL)
    nconv_s = u_s.reshape(nb, t_len, CONV_DIM)[:, t_len - 2:]

    return (y_prompt, y_sample, nconv_p[None],
            _buffer_from_channel_major(kv1_p), _buffer_from_channel_major(kv2_p),
            _buffer_from_channel_major(kv3_p),
            nconv_s[None], new_bufs[0], new_bufs[1], new_bufs[2])
```

```python
import functools

import numpy as np
import jax
import jax.numpy as jnp
from jax import lax
from jax.experimental import pallas as pl
from jax.experimental.pallas import tpu as pltpu

F32 = jnp.float32
BF16 = jnp.bfloat16

D_MODEL = 1024
HEAD_DIM = 64
CONV_DIM = 256
ATTN_DIM = 768
GROUP_HEADS = 4
GROUP_DIM = GROUP_HEADS * HEAD_DIM
DIL_PAIRS = ((128, 1), (512, 4), (2048, 16))
N_DIL = len(DIL_PAIRS)
QB = 128
N_BUCKETS = 32
MAX_DISTANCE = 2048
D_FF = 2816
PROJ_DIM = 3072
EPS = 1e-6
NEG_INF = -1e30
ATTN_SCALE = HEAD_DIM ** -0.5

LANES = 128
VMEM_LIMIT = 56 * 1024 * 1024
HOST_VMEM_LIMIT = 62 * 1024 * 1024


def _cparams(sem):
    return pltpu.CompilerParams(dimension_semantics=sem, vmem_limit_bytes=VMEM_LIMIT)


def _const_spec(shape):
    nd = len(shape)
    return pl.BlockSpec(shape, lambda *_: (0,) * nd, pipeline_mode=pl.Buffered(1))


def _head_masks(width, dtype):
    lane = lax.broadcasted_iota(jnp.int32, (1, width), 1)
    if width == GROUP_DIM:
        lane = lane // HEAD_DIM
    return [(lane == h).astype(dtype) for h in range(GROUP_HEADS)]


def _ada_kernel(c_ref, w_ref, b_ref, op_ref, os_ref):
    c = c_ref[...]
    s = (c * jax.nn.sigmoid(c)).astype(BF16)
    mod = jnp.dot(s, w_ref[...].astype(BF16), preferred_element_type=F32) + b_ref[...]
    rows_p = op_ref.shape[0]
    op_ref[...] = mod[:rows_p]
    os_ref[...] = mod[rows_p:]


def _ada(c_all, w_ada, b_ada, rows_p):
    rows = c_all.shape[0]
    tn = 1536
    assert rows_p % 8 == 0
    return pl.pallas_call(
        _ada_kernel,
        grid=(6 * D_MODEL // tn,),
        in_specs=[pl.BlockSpec((rows, D_MODEL), lambda j: (0, 0)),
                  pl.BlockSpec((D_MODEL, tn), lambda j: (0, j)),
                  pl.BlockSpec((1, tn), lambda j: (0, j))],
        out_specs=[pl.BlockSpec((rows_p, tn), lambda j: (0, j)),
                   pl.BlockSpec((rows - rows_p, tn), lambda j: (0, j))],
        out_shape=[jax.ShapeDtypeStruct((rows_p, 6 * D_MODEL), F32),
                   jax.ShapeDtypeStruct((rows - rows_p, 6 * D_MODEL), F32)],
        compiler_params=_cparams(("arbitrary",)),
        name="ada",
    )(c_all, w_ada, b_ada)


def _rms(x, g):
    return x * lax.rsqrt(jnp.mean(x * x, axis=-1, keepdims=True) + EPS) * g


STREAM_SLABS = sum(GROUP_DIM // LANES for _, d in DIL_PAIRS if d > 1)
INPROJ_PARTS = 2


def _store_streams(p, out_refs, slab_ref, slab0, r0):
    rows = p.shape[0]
    for g, ((_, d), out_ref) in enumerate(zip(DIL_PAIRS, out_refs)):
        lo = g * GROUP_DIM
        if d == 1:
            out_ref[0, 0, r0:r0 + rows, :] = p[:, lo:lo + GROUP_DIM].astype(BF16)
            continue
        for s in range(GROUP_DIM // LANES):
            slab_ref[slab0 + s] = p[:, lo + s * LANES:lo + (s + 1) * LANES]
        for r in range(d):
            for s in range(GROUP_DIM // LANES):
                out_ref[0, r, r0 // d:(r0 + rows) // d, s * LANES:(s + 1) * LANES] = (
                    slab_ref[slab0 + s, pl.ds(r, rows // d, stride=d), :].astype(BF16))
        slab0 += GROUP_DIM // LANES


N_INPROJ_INPUTS = 7
N_INPROJ_OUTPUTS = 14
N_SAMPLE_INPUTS = 8
N_SAMPLE_OUTPUTS = 3


def _inproj_prompt_kernel(*refs, tm, hosted):
    n_in = N_INPROJ_INPUTS + N_SAMPLE_INPUTS * len(hosted)
    x_ref, sh_ref, sc_ref, g1_ref, w_ref, cw_ref, gnc_ref = refs[:N_INPROJ_INPUTS]
    (convn_ref, q1_ref, q2_ref, q3_ref, k1_ref, k2_ref, k3_ref, v1_ref, v2_ref, v3_ref,
     nconv_ref, kv1_ref, kv2_ref, kv3_ref) = refs[n_in:n_in + N_INPROJ_OUTPUTS]
    ubuf_ref, slab_ref = refs[-2:]
    i = pl.program_id(1)
    n = pl.num_programs(1)
    pr = tm // INPROJ_PARTS
    parts = range(INPROJ_PARTS)
    c0 = 3 * CONV_DIM
    cw = cw_ref[...]

    hbs = [(_rms(x_ref[0, a * pr:(a + 1) * pr, :], g1_ref[...]) * (1.0 + sc_ref[0]) + sh_ref[0]).astype(BF16)
           for a in parts]
    projs = [jnp.dot(hb, w_ref[...], preferred_element_type=F32) for hb in hbs]

    for j, dims in enumerate(hosted):
        s_in = refs[N_INPROJ_INPUTS + j * N_SAMPLE_INPUTS:N_INPROJ_INPUTS + (j + 1) * N_SAMPLE_INPUTS]
        s_out = refs[n_in + N_INPROJ_OUTPUTS + j * N_SAMPLE_OUTPUTS:
                     n_in + N_INPROJ_OUTPUTS + (j + 1) * N_SAMPLE_OUTPUTS]
        _attn_sample_rows(pl.program_id(0) * n + i, *s_in, *s_out, *dims)

    tail = ubuf_ref[tm:tm + 8, :]
    ubuf_ref[0:8, :] = jnp.where(i == 0, jnp.zeros_like(tail), tail)
    for a in parts:
        r0 = a * pr
        p = projs[a]
        gb = p[:, 0:CONV_DIM]
        u = p[:, CONV_DIM:2 * CONV_DIM] * p[:, 2 * CONV_DIM:c0]
        ubuf_ref[8 + r0:8 + r0 + pr, :] = u
        um1 = ubuf_ref[7 + r0:7 + r0 + pr, :]
        um2 = ubuf_ref[6 + r0:6 + r0 + pr, :]
        z = cw[0:1, :] * um2 + cw[1:2, :] * um1 + cw[2:3, :] * u
        convn_ref[0, r0:r0 + pr, :] = _rms(gb * z, gnc_ref[...]).astype(BF16)
        slab0 = a * 3 * STREAM_SLABS
        _store_streams(p[:, c0:c0 + ATTN_DIM] * ATTN_SCALE, (q1_ref, q2_ref, q3_ref), slab_ref, slab0, r0)
        _store_streams(p[:, c0 + ATTN_DIM:c0 + 2 * ATTN_DIM], (k1_ref, k2_ref, k3_ref), slab_ref,
                       slab0 + STREAM_SLABS, r0)
        _store_streams(p[:, c0 + 2 * ATTN_DIM:c0 + 3 * ATTN_DIM], (v1_ref, v2_ref, v3_ref), slab_ref,
                       slab0 + 2 * STREAM_SLABS, r0)
    nconv_ref[0] = ubuf_ref[tm + 6:tm + 8, :]

    def put(ref, g, rows):
        for a in parts:
            lo_r, hi_r = max(tm - rows, a * pr), (a + 1) * pr
            if hi_r <= lo_r:
                continue
            for kv in range(2):
                lo = c0 + (1 + kv) * ATTN_DIM + g * GROUP_DIM
                ref[0, kv, :, lo_r - (tm - rows):hi_r - (tm - rows)] = (
                    projs[a][lo_r - a * pr:hi_r - a * pr, lo:lo + GROUP_DIM].T)

    w3 = DIL_PAIRS[2][0]

    @pl.when(i >= n - w3 // tm)
    def _():
        put(kv3_ref, 2, tm)

    @pl.when(i == n - 1)
    def _():
        put(kv2_ref, 1, DIL_PAIRS[1][0])
        put(kv1_ref, 0, DIL_PAIRS[0][0])


INPROJ_TILE = 1024


def _inproj_prompt(x, mod, g1, w_in, conv_w, gn_conv, samples=()):
    B, S, _ = x.shape
    tm = INPROJ_TILE
    n = S // tm
    w1, w2, w3 = (w for w, _ in DIL_PAIRS)
    assert S % tm == 0 and w3 % tm == 0 and tm >= w2 and tm % (INPROJ_PARTS * LANES) == 0
    n3 = w3 // tm
    row = lambda b, i: (b, i, 0)
    stream_specs = [pl.BlockSpec((1, d, tm // d, GROUP_DIM), lambda b, i: (b, 0, i, 0)) for _, d in DIL_PAIRS]
    stream_shapes = [jax.ShapeDtypeStruct((B, d, S // d, GROUP_DIM), BF16) for _, d in DIL_PAIRS]
    in_specs = [pl.BlockSpec((1, tm, D_MODEL), row),
                pl.BlockSpec((1, 1, D_MODEL), lambda b, i: (b, 0, 0)),
                pl.BlockSpec((1, 1, D_MODEL), lambda b, i: (b, 0, 1)),
                _const_spec((1, D_MODEL)),
                _const_spec((D_MODEL, PROJ_DIM)),
                _const_spec((3, CONV_DIM)),
                _const_spec((1, CONV_DIM))]
    out_specs = [pl.BlockSpec((1, tm, CONV_DIM), row)] + stream_specs * 3 + [
        pl.BlockSpec((1, 2, CONV_DIM), lambda b, i: (b, 0, 0)),
        pl.BlockSpec((1, 2, GROUP_DIM, w1), lambda b, i: (b, 0, 0, 0)),
        pl.BlockSpec((1, 2, GROUP_DIM, w2), lambda b, i: (b, 0, 0, 0)),
        pl.BlockSpec((1, 2, GROUP_DIM, tm), lambda b, i: (b, 0, 0, jnp.maximum(i - (n - n3), 0)))]
    out_shapes = [jax.ShapeDtypeStruct((B, S, CONV_DIM), BF16)] + stream_shapes * 3 + [
        jax.ShapeDtypeStruct((B, 2, CONV_DIM), F32),
        jax.ShapeDtypeStruct((B, 2, GROUP_DIM, w1), F32),
        jax.ShapeDtypeStruct((B, 2, GROUP_DIM, w2), F32),
        jax.ShapeDtypeStruct((B, 2, GROUP_DIM, w3), F32)]
    operands = [x, mod, mod, g1, w_in, conv_w, gn_conv]
    assert len(operands) == N_INPROJ_INPUTS and len(out_specs) == N_INPROJ_OUTPUTS
    hosted = []
    for q8, k8, v8, kt, vt, cache, bc, bn, g, t_len in samples:
        nbb = cache.shape[0] // (B * n)
        assert nbb * B * n == cache.shape[0]
        s_in, s_out, s_shapes = _attn_sample_specs(cache.shape, bc, bn, g, t_len, nbb, lambda b, i: b * n + i)
        in_specs += s_in
        operands += [q8, k8, v8, kt, vt, cache, bc, bn]
        out_specs += s_out
        out_shapes += s_shapes
        hosted.append((cache.shape[-1], t_len))
    return pl.pallas_call(
        functools.partial(_inproj_prompt_kernel, tm=tm, hosted=tuple(hosted)),
        grid=(B, n),
        in_specs=in_specs, out_specs=out_specs, out_shape=out_shapes,
        scratch_shapes=[pltpu.VMEM((tm + 8, CONV_DIM), F32),
                        pltpu.VMEM((INPROJ_PARTS * 3 * STREAM_SLABS, tm // INPROJ_PARTS, LANES), F32)],
        compiler_params=pltpu.CompilerParams(dimension_semantics=("arbitrary", "arbitrary"),
                                             vmem_limit_bytes=HOST_VMEM_LIMIT if hosted else VMEM_LIMIT),
        name="inproj_prompt",
    )(*operands)


def _inproj_sample_kernel(x_ref, sh_ref, sc_ref, g1_ref, w_ref, cw_ref, gnc_ref, p0_ref, p1_ref,
                          convn_ref, u_ref, q_ref, k_ref, v_ref, kt_ref, vt_ref, ubuf_ref, *, tm, t_len):
    h = _rms(x_ref[...], g1_ref[...]) * (1.0 + sc_ref[...]) + sh_ref[...]
    hb = h.astype(BF16)

    pc = jnp.dot(hb, w_ref[:, 0:3 * CONV_DIM], preferred_element_type=F32)
    gb = pc[:, 0:CONV_DIM]
    u = pc[:, CONV_DIM:2 * CONV_DIM] * pc[:, 2 * CONV_DIM:3 * CONV_DIM]
    u_ref[...] = u

    ubuf_ref[0:8, :] = jnp.zeros((8, CONV_DIM), F32)
    ubuf_ref[8:tm + 8, :] = u
    t = lax.broadcasted_iota(jnp.int32, (tm, 1), 0) % t_len
    um1 = jnp.where(t == 0, p1_ref[...], ubuf_ref[7:tm + 7, :])
    um2 = jnp.where(t == 0, p0_ref[...], jnp.where(t == 1, p1_ref[...], ubuf_ref[6:tm + 6, :]))
    cw = cw_ref[...]
    z = cw[0:1, :] * um2 + cw[1:2, :] * um1 + cw[2:3, :] * u
    convn_ref[...] = _rms(gb * z, gnc_ref[...]).astype(BF16)

    c0 = 3 * CONV_DIM
    pq = jnp.dot(hb, w_ref[:, c0:c0 + ATTN_DIM], preferred_element_type=F32)
    q_ref[...] = pq * ATTN_SCALE
    pk = jnp.dot(hb, w_ref[:, c0 + ATTN_DIM:c0 + 2 * ATTN_DIM], preferred_element_type=F32)
    k_ref[...] = pk
    kt_ref[...] = pk.T
    pv = jnp.dot(hb, w_ref[:, c0 + 2 * ATTN_DIM:c0 + 3 * ATTN_DIM], preferred_element_type=F32)
    v_ref[...] = pv
    vt_ref[...] = pv.T


def _inproj_sample(x, mod, g1, w_in, conv_w, gn_conv, prev0, prev1, t_len):
    tm = x.shape[0]
    full = lambda shape: pl.BlockSpec(shape, lambda i: (0,) * len(shape))
    return pl.pallas_call(
        functools.partial(_inproj_sample_kernel, tm=tm, t_len=t_len),
        grid=(1,),
        in_specs=[full((tm, D_MODEL)),
                  pl.BlockSpec((tm, D_MODEL), lambda i: (0, 0)),
                  pl.BlockSpec((tm, D_MODEL), lambda i: (0, 1)),
                  full((1, D_MODEL)), full((D_MODEL, PROJ_DIM)), full((3, CONV_DIM)), full((1, CONV_DIM)),
                  full((tm, CONV_DIM)), full((tm, CONV_DIM))],
        out_specs=[full((tm, CONV_DIM)), full((tm, CONV_DIM)),
                   full((tm, ATTN_DIM)), full((tm, ATTN_DIM)), full((tm, ATTN_DIM)),
                   full((ATTN_DIM, tm)), full((ATTN_DIM, tm))],
        out_shape=[jax.ShapeDtypeStruct((tm, CONV_DIM), BF16),
                   jax.ShapeDtypeStruct((tm, CONV_DIM), F32),
                   jax.ShapeDtypeStruct((tm, ATTN_DIM), F32),
                   jax.ShapeDtypeStruct((tm, ATTN_DIM), F32),
                   jax.ShapeDtypeStruct((tm, ATTN_DIM), F32),
                   jax.ShapeDtypeStruct((ATTN_DIM, tm), F32),
                   jax.ShapeDtypeStruct((ATTN_DIM, tm), F32)],
        scratch_shapes=[pltpu.VMEM((tm + 8, CONV_DIM), F32)],
        compiler_params=_cparams(("arbitrary",)),
        name="inproj_sample",
    )(x, mod, mod, g1, w_in, conv_w, gn_conv, prev0, prev1)


def _t5_buckets(dist):
    dist = np.asarray(dist, np.int32)
    max_exact = N_BUCKETS // 2
    large = max_exact + (np.log(np.maximum(dist, 1).astype(np.float32) / max_exact)
                         / np.log(MAX_DISTANCE / max_exact) * (N_BUCKETS - max_exact)).astype(np.int32)
    large = np.minimum(large, N_BUCKETS - 1)
    return np.where(dist < max_exact, dist, large).astype(np.int32)


def _group_table(rel_bias, g):
    w, d = DIL_PAIRS[g]
    buckets = _t5_buckets(np.arange(w // d + 1) * d)
    return rel_bias[buckets][:, g * GROUP_HEADS:(g + 1) * GROUP_HEADS].T.astype(F32)


def _prompt_bias(table):
    period = 3 * QB - 1
    base = jnp.concatenate([table[:, ::-1], jnp.full((GROUP_HEADS, period - QB - 1), NEG_INF, F32)], axis=1)
    skew = jnp.tile(base, (1, QB))[:, :QB * (period - 1)].reshape(GROUP_HEADS, QB, period - 1)
    prev, cur = skew[:, :, 0:QB], skew[:, :, QB:2 * QB]
    tri = (np.arange(QB)[None, :] <= np.arange(QB)[:, None])[None]
    fold = jnp.stack([jnp.where(tri, cur, prev), jnp.where(tri, cur, NEG_INF)])
    fold = fold.transpose(0, 3, 1, 2).reshape(2, QB, GROUP_HEADS * QB)
    lone = jnp.broadcast_to(table[:, QB:QB + 1], (GROUP_HEADS, QB)).reshape(1, 1, GROUP_HEADS * QB)
    return fold, jnp.concatenate([lone, jnp.full_like(lone, NEG_INF)], axis=0)


def _sample_bias(table, buf_len, d, t_len):
    assert buf_len == QB * d
    asc = table[:, ::-1]
    inter = jnp.concatenate([asc[:, :, None], jnp.full((GROUP_HEADS, QB + 1, d - 1), NEG_INF, F32)],
                            axis=2).reshape(GROUP_HEADS, (QB + 1) * d)
    ext_len = buf_len + LANES
    rows = []
    for t in range(8):
        lead = t if t < t_len else 0
        rows.append(jnp.pad(inter, ((0, 0), (lead, ext_len - lead - (QB + 1) * d)), constant_values=NEG_INF))
    ext = jnp.stack(rows, axis=1)
    return (ext[:, :, :buf_len].reshape(GROUP_HEADS * 8, buf_len),
            ext[:, :, buf_len:].reshape(GROUP_HEADS * 8, LANES))


def _attn_prompt_kernel(q_ref, kc_ref, vc_ref, kp_ref, vp_ref, bias_ref, dbias_ref, o_ref, lse_ref):
    i = pl.program_id(1)
    sb, tq, _ = q_ref.shape
    hm_b = _head_masks(GROUP_DIM, BF16)
    first = jnp.where(i == 0, 1, 0)
    c_idx = lax.broadcasted_iota(jnp.int32, (QB, GROUP_HEADS * QB), 0)
    q_idx = lax.broadcasted_iota(jnp.int32, (QB, GROUP_HEADS * QB), 1) & (QB - 1)
    tri = c_idx <= q_idx
    tri_b = tri.astype(F32).astype(BF16)
    diag_f = (c_idx == q_idx).astype(F32)
    diag_b = diag_f.astype(BF16)
    work = [(s, j) for s in range(sb) for j in range(tq // QB)]

    def scores(s, j):
        q = q_ref[s, j * QB:(j + 1) * QB, :]
        q4 = jnp.concatenate([q * hm_b[h] for h in range(GROUP_HEADS)], axis=0)
        if j == 0:
            kk = jnp.concatenate([kp_ref[s], kc_ref[s, 0:QB, :]], axis=0)
        else:
            kk = kc_ref[s, (j - 1) * QB:(j + 1) * QB, :]
        return lax.dot_general(kk, q4, (((1,), (1,)), ((), ())), preferred_element_type=F32)

    def values_t(s, j):
        if j == 0:
            return jnp.concatenate([vp_ref[s].T, vc_ref[s, 0:QB, :].T], axis=1)
        return vc_ref[s, (j - 1) * QB:(j + 1) * QB, :].T

    st_next = scores(*work[0])
    for n, (s, j) in enumerate(work):
        st = st_next
        if n + 1 < len(work):
            st_next = scores(*work[n + 1])
        slot = first if j == 0 else 0
        sp = st[0:QB]
        sc = st[QB:2 * QB]
        w = jnp.where(tri, sc, sp) + bias_ref[slot]
        sd = jnp.sum(sp * diag_f, axis=0, keepdims=True) + dbias_ref[slot]
        m = jnp.maximum(jnp.max(w, axis=0, keepdims=True), sd)
        e = jnp.exp(w - m)
        ed = jnp.exp(sd - m)
        den = jnp.sum(e, axis=0, keepdims=True) + ed
        lse = m + jnp.log(den)
        rden = 1.0 / den
        e_b = e.astype(BF16)
        p_cur = e_b * tri_b
        p_prev = (e_b - p_cur) + jnp.broadcast_to(ed.astype(BF16), e_b.shape) * diag_b
        pt = jnp.concatenate([p_prev, p_cur], axis=0)
        vt = values_t(s, j)
        ots = []
        for h in range(GROUP_HEADS):
            cols = slice(h * QB, (h + 1) * QB)
            rows = slice(h * HEAD_DIM, (h + 1) * HEAD_DIM)
            ot = jnp.dot(vt[rows, :], pt[:, cols], preferred_element_type=F32)
            ots.append(ot * rden[:, cols])
        o_ref[s, j * QB:(j + 1) * QB, :] = jnp.concatenate(ots, axis=0).T
        lrows = jnp.concatenate([lse[:, h * QB:(h + 1) * QB] for h in range(GROUP_HEADS)]
                                + [jnp.zeros((LANES - GROUP_HEADS, QB), F32)], axis=0)
        lse_ref[s, j * QB:(j + 1) * QB, :] = lrows.T


def _attn_prompt(q, k, v, bias, dbias, block_rows=4096):
    ns, M, _ = q.shape
    tq = min(block_rows, M)
    sb = block_rows // tq
    r = tq // QB
    cur = pl.BlockSpec((sb, tq, GROUP_DIM), lambda s, i: (s, i, 0))
    prev = pl.BlockSpec((sb, QB, GROUP_DIM), lambda s, i: (s, jnp.maximum(i * r - 1, 0), 0))
    return pl.pallas_call(
        _attn_prompt_kernel,
        grid=(ns // sb, M // tq),
        in_specs=[cur, cur, cur, prev, prev, _const_spec(bias.shape), _const_spec(dbias.shape)],
        out_specs=[pl.BlockSpec((sb, tq, GROUP_DIM), lambda s, i: (s, i, 0)),
                   pl.BlockSpec((sb, tq, LANES), lambda s, i: (s, i, 0))],
        out_shape=[jax.ShapeDtypeStruct((ns, M, GROUP_DIM), F32),
                   jax.ShapeDtypeStruct((ns, M, LANES), F32)],
        compiler_params=_cparams(("arbitrary", "arbitrary")),
        name="attn_prompt",
    )(q, k, v, k, v, bias, dbias)


def _attn_sample_kernel(q_ref, k8_ref, v8_ref, kt_ref, vt_ref, cache_ref, bc_ref, bn_ref,
                        o_ref, lse_ref, newc_ref, *, buf_len, t_len):
    _attn_sample_rows(pl.program_id(0), q_ref, k8_ref, v8_ref, kt_ref, vt_ref, cache_ref, bc_ref, bn_ref,
                      o_ref, lse_ref, newc_ref, buf_len, t_len)


def _attn_sample_rows(step, q_ref, k8_ref, v8_ref, kt_ref, vt_ref, cache_ref, bc_ref, bn_ref,
                      o_ref, lse_ref, newc_ref, buf_len, t_len):
    rows = range(q_ref.shape[0])
    hm_f = _head_masks(GROUP_DIM, F32)
    lm_f = _head_masks(LANES, F32)
    lane = lax.broadcasted_iota(jnp.int32, (1, LANES), 1)

    q4bs = [jnp.concatenate([q_ref[i] * hm_f[h] for h in range(GROUP_HEADS)], axis=0).astype(BF16)
            for i in rows]
    s_cs = [jnp.dot(q4bs[i], cache_ref[i, 0].astype(BF16), preferred_element_type=F32) + bc_ref[...]
            for i in rows]

    stats = []
    for i in rows:
        q4r = q4bs[i].astype(F32)
        k8r = k8_ref[i].astype(BF16).astype(F32)
        s_n = bn_ref[...]
        for t in range(t_len):
            col = jnp.sum(q4r * k8r[t:t + 1, :], axis=-1, keepdims=True)
            s_n = s_n + jnp.where(lane == t, col, 0.0)
        m = jnp.maximum(jnp.max(s_cs[i], axis=-1, keepdims=True), jnp.max(s_n, axis=-1, keepdims=True))
        e_c = jnp.exp(s_cs[i] - m)
        e_n = jnp.exp(s_n - m)
        den = jnp.sum(e_c, axis=-1, keepdims=True) + jnp.sum(e_n, axis=-1, keepdims=True)
        stats.append((m, e_c, e_n, den))

    pvs = [lax.dot_general(stats[i][1].astype(BF16), cache_ref[i, 1].astype(BF16), (((1,), (1,)), ((), ())),
                           preferred_element_type=F32) for i in rows]
    for i in rows:
        m, _, e_n, den = stats[i]
        pv = pvs[i]
        v8r = v8_ref[i].astype(BF16).astype(F32)
        e_nr = e_n.astype(BF16).astype(F32)
        for t in range(t_len):
            w_t = jnp.sum(jnp.where(lane == t, e_nr, 0.0), axis=-1, keepdims=True)
            pv = pv + w_t * v8r[t:t + 1, :]
        pv = pv / den
        lse = m + jnp.log(den)
        o = jnp.zeros((8, GROUP_DIM), F32)
        l = jnp.zeros((8, LANES), F32)
        for h in range(GROUP_HEADS):
            o = o + pv[h * 8:(h + 1) * 8, :] * hm_f[h]
            l = l + lse[h * 8:(h + 1) * 8, :] * lm_f[h]
        o_ref[i] = o
        lse_ref[i] = l

    keep = LANES - t_len
    per_block = LANES // t_len
    for i in rows:
        shift_new = keep - ((step * len(rows) + i) % per_block) * t_len
        for kv, new_ref in ((0, kt_ref), (1, vt_ref)):
            new_r = pltpu.roll(new_ref[...], shift_new, 1)
            prev_r = None
            for c in range(buf_len // LANES):
                r_c = pltpu.roll(cache_ref[i, kv, :, c * LANES:(c + 1) * LANES], keep, 1)
                if c > 0:
                    newc_ref[i, kv, :, (c - 1) * LANES:c * LANES] = jnp.where(lane < keep, prev_r, r_c)
                prev_r = r_c
            newc_ref[i, kv, :, buf_len - LANES:buf_len] = jnp.where(lane < keep, prev_r, new_r)


SAMPLE_STEP_POSITIONS = 2048
SAMPLE_STEP_ROWS = 8


def _attn_sample_specs(cache_shape, bc, bn, g, t_len, nbb, step_of):
    nb, _, _, buf_len = cache_shape
    per_block = LANES // t_len
    assert nb % nbb == 0 and per_block % nbb == 0
    new_spec = pl.BlockSpec((nbb, 8, GROUP_DIM), lambda *i: (step_of(*i), 0, g))
    newt_spec = pl.BlockSpec((GROUP_DIM, LANES), lambda *i: (g, (step_of(*i) * nbb) // per_block))
    cache_spec = pl.BlockSpec((nbb, 2, GROUP_DIM, buf_len), lambda *i: (step_of(*i), 0, 0, 0))
    in_specs = [new_spec, new_spec, new_spec, newt_spec, newt_spec, cache_spec,
                _const_spec(bc.shape), _const_spec(bn.shape)]
    out_specs = [pl.BlockSpec((nbb, 8, GROUP_DIM), lambda *i: (step_of(*i), 0, 0)),
                 pl.BlockSpec((nbb, 8, LANES), lambda *i: (step_of(*i), 0, 0)),
                 cache_spec]
    out_shapes = [jax.ShapeDtypeStruct((nb, 8, GROUP_DIM), F32),
                  jax.ShapeDtypeStruct((nb, 8, LANES), F32),
                  jax.ShapeDtypeStruct(cache_shape, F32)]
    return in_specs, out_specs, out_shapes


def _attn_sample(q8, k8, v8, kt, vt, cache, bc, bn, g, t_len):
    nb, _, _, buf_len = cache.shape
    nbb = max(1, min(LANES // t_len, SAMPLE_STEP_ROWS, SAMPLE_STEP_POSITIONS // buf_len))
    in_specs, out_specs, out_shapes = _attn_sample_specs(cache.shape, bc, bn, g, t_len, nbb, lambda b: b)
    return pl.pallas_call(
        functools.partial(_attn_sample_kernel, buf_len=buf_len, t_len=t_len),
        grid=(nb // nbb,),
        in_specs=in_specs, out_specs=out_specs, out_shape=out_shapes,
        compiler_params=_cparams(("arbitrary",)),
        name="attn_sample",
    )(q8, k8, v8, kt, vt, cache, bc, bn)


OUT_PARTS = 2
MERGE_SLABS = N_DIL * (GROUP_DIM + LANES) // LANES


def _load_positions(ref, slab_ref, slab0, r0, rows):
    _, d, _, width = ref.shape
    if d == 1:
        return ref[0, 0, r0:r0 + rows, :]
    for r in range(d):
        for s in range(width // LANES):
            slab_ref[slab0 + s, pl.ds(r, rows // d, stride=d), :] = (
                ref[0, r, r0 // d:(r0 + rows) // d, s * LANES:(s + 1) * LANES])
    return jnp.concatenate([slab_ref[slab0 + s] for s in range(width // LANES)], axis=-1)


N_OUT_INPUTS = 19


def _out_kernel(*refs, hosted):
    (x_ref, ga1_ref, sh2_ref, sc2_ref, ga2_ref, convn_ref, o1_ref, o2_ref, o3_ref, l1_ref, l2_ref, l3_ref,
     gna_ref, wout_ref, g2_ref, wg_ref, wu_ref, wd_ref, fg_ref) = refs[:N_OUT_INPUTS]
    if hosted is None:
        y_ref, slab_ref = refs[N_OUT_INPUTS:]
    else:
        sample_in = refs[N_OUT_INPUTS:N_OUT_INPUTS + N_SAMPLE_INPUTS]
        y_ref, *sample_out, slab_ref = refs[N_OUT_INPUTS + N_SAMPLE_INPUTS:]
    tm = x_ref.shape[1]
    pr = tm // OUT_PARTS
    parts = range(OUT_PARTS)
    hm_f = _head_masks(GROUP_DIM, F32)
    lm_f = _head_masks(LANES, F32)
    gna = gna_ref[...]

    def mod_rows(ref, a):
        return ref[0] if ref.shape[1] == 1 else ref[0, a * pr:(a + 1) * pr, :]

    def merge(a):
        r0 = a * pr
        slab0 = a * MERGE_SLABS
        ls = [_load_positions(l_ref, slab_ref, slab0 + g, r0, pr)
              for g, l_ref in enumerate((l1_ref, l2_ref, l3_ref))]
        m = jnp.maximum(jnp.maximum(ls[0], ls[1]), ls[2])
        es = [jnp.exp(l - m) for l in ls]
        tot = es[0] + es[1] + es[2]
        aos = []
        ssq = jnp.zeros((pr, 1), F32)
        for g, o_ref in enumerate((o1_ref, o2_ref, o3_ref)):
            alpha = es[g] / tot
            wide = jnp.zeros((pr, GROUP_DIM), F32)
            for h in range(GROUP_HEADS):
                a_h = jnp.sum(alpha * lm_f[h], axis=-1, keepdims=True)
                wide = wide + a_h * hm_f[h]
            ao = wide * _load_positions(o_ref, slab_ref, slab0 + N_DIL + g * (GROUP_DIM // LANES), r0, pr)
            ssq = ssq + jnp.sum(ao * ao, axis=-1, keepdims=True)
            aos.append(ao)
        rinv = lax.rsqrt(ssq / ATTN_DIM + EPS)
        return jnp.concatenate(
            [convn_ref[0, r0:r0 + pr, :]]
            + [(aos[g] * rinv * gna[:, g * GROUP_DIM:(g + 1) * GROUP_DIM]).astype(BF16) for g in range(N_DIL)],
            axis=-1)

    def out_proj(a, mixed):
        mix = jnp.dot(mixed, wout_ref[...], preferred_element_type=F32)
        x1 = x_ref[0, a * pr:(a + 1) * pr, :] + mod_rows(ga1_ref, a) * mix
        h2 = (_rms(x1, g2_ref[...]) * (1.0 + mod_rows(sc2_ref, a)) + mod_rows(sh2_ref, a)).astype(BF16)
        return x1, h2

    def swiglu(h2):
        gate = jnp.dot(h2, wg_ref[...], preferred_element_type=F32)
        up = jnp.dot(h2, wu_ref[...], preferred_element_type=F32)
        act = (gate * jax.nn.sigmoid(gate) * up).astype(BF16)
        return jnp.dot(act, wd_ref[...], preferred_element_type=F32)

    mixed = [merge(a) for a in parts]
    x1h2 = [out_proj(a, mixed[a]) for a in parts]
    if hosted is not None:
        step = pl.program_id(0) * pl.num_programs(1) + pl.program_id(1)
        _attn_sample_rows(step, *sample_in, *sample_out, *hosted)
    ffn = [swiglu(h2) for _, h2 in x1h2]
    for a in parts:
        x2 = x1h2[a][0] + mod_rows(ga2_ref, a) * ffn[a]
        y_ref[0, a * pr:(a + 1) * pr, :] = _rms(x2, fg_ref[...])


OUT_TILE = 512


def _out(x, mod, convn, os_, ls_, gn_attn, w_out, g2, w_gate, w_up, w_down, final_g, sample=None):
    nb, R, _ = x.shape
    tm = min(OUT_TILE, R)
    n_i = R // tm
    per_row = mod.shape[1] != 1
    mrows = tm if per_row else 1
    row = lambda b, i: (b, i, 0)

    def mod_spec(c):
        return pl.BlockSpec((1, mrows, D_MODEL), lambda b, i: (b, i if per_row else 0, c))

    def stream_spec(a):
        _, d, _, width = a.shape
        return pl.BlockSpec((1, d, tm // d, width), lambda b, i: (b, 0, i, 0))

    in_specs = ([pl.BlockSpec((1, tm, D_MODEL), row),
                 mod_spec(2), mod_spec(3), mod_spec(4), mod_spec(5),
                 pl.BlockSpec((1, tm, CONV_DIM), row)]
                + [stream_spec(a) for a in os_] + [stream_spec(a) for a in ls_]
                + [_const_spec((1, ATTN_DIM)), _const_spec((D_MODEL, D_MODEL)), _const_spec((1, D_MODEL)),
                   _const_spec((D_MODEL, D_FF)), _const_spec((D_MODEL, D_FF)), _const_spec((D_FF, D_MODEL)),
                   _const_spec((1, D_MODEL))])
    operands = [x, mod, mod, mod, mod, convn, *os_, *ls_, gn_attn, w_out, g2, w_gate, w_up, w_down, final_g]
    assert len(operands) == N_OUT_INPUTS
    out_specs = [pl.BlockSpec((1, tm, D_MODEL), row)]
    out_shapes = [jax.ShapeDtypeStruct(x.shape, F32)]
    hosted, limit = None, VMEM_LIMIT
    if sample is not None:
        q8, k8, v8, kt, vt, cache, bc, bn, g, t_len = sample
        assert cache.shape[0] == nb * n_i
        s_in, s_out, s_shapes = _attn_sample_specs(cache.shape, bc, bn, g, t_len, 1, lambda b, i: b * n_i + i)
        in_specs += s_in
        operands += [q8, k8, v8, kt, vt, cache, bc, bn]
        out_specs += s_out
        out_shapes += s_shapes
        hosted, limit = (cache.shape[-1], t_len), HOST_VMEM_LIMIT
    outs = pl.pallas_call(
        functools.partial(_out_kernel, hosted=hosted),
        grid=(nb, n_i),
        in_specs=in_specs, out_specs=out_specs, out_shape=out_shapes,
        scratch_shapes=[pltpu.VMEM((OUT_PARTS * MERGE_SLABS, tm // OUT_PARTS, LANES), F32)],
        compiler_params=pltpu.CompilerParams(dimension_semantics=("arbitrary", "arbitrary"),
                                             vmem_limit_bytes=limit),
        name="out",
    )(*operands)
    return outs[0] if sample is None else outs


def _buffer_to_channel_major(cache):
    B, L = cache.shape[:2]
    return cache.transpose(0, 2, 3, 4, 1).reshape(B, 2, GROUP_DIM, L)


def _buffer_from_channel_major(buf):
    B, _, _, L = buf.shape
    return buf.reshape(B, 2, GROUP_HEADS, HEAD_DIM, L).transpose(0, 4, 1, 2, 3)[None]


def kernel(x_prompt, x_sample, state_conv, cache_kv1, cache_kv2, cache_kv3, c_prompt, c_sample, w_ada, b_ada, norm1_g, norm2_g, w_in, conv_w, gn_conv, gn_attn, w_out, w_gate, w_up, w_down, rel_bias, final_g):
    B, S, _ = x_prompt.shape
    nb, t_len, _ = x_sample.shape
    rows_s = nb * t_len
    assert w_ada.shape[0] == 1 and LANES % t_len == 0 and 2 <= t_len <= 8 and rows_s % LANES == 0

    w_in_b = w_in[0].astype(BF16)
    w_out_b = w_out[0].astype(BF16)
    w_gate_b = w_gate[0].astype(BF16)
    w_up_b = w_up[0].astype(BF16)
    w_down_b = w_down[0].astype(BF16)
    final_g2 = final_g.reshape(1, D_MODEL)
    tables = [_group_table(rel_bias, g) for g in range(N_DIL)]

    c_all = jnp.concatenate([c_prompt, jnp.repeat(c_sample, t_len, axis=0)], axis=0)
    mod_p, mod_s = _ada(c_all, w_ada[0], b_ada, B)
    mod_p = mod_p.reshape(B, 1, 6 * D_MODEL)

    xs = x_sample.reshape(rows_s, D_MODEL)
    prev0 = jnp.repeat(state_conv[0, :, 0], t_len, axis=0)
    prev1 = jnp.repeat(state_conv[0, :, 1], t_len, axis=0)
    convn_s, u_s, q_s, k_s, v_s, kt_s, vt_s = _inproj_sample(
        xs, mod_s, norm1_g, w_in_b, conv_w[0], gn_conv, prev0, prev1, t_len)
    pad8 = lambda a: jnp.pad(a.reshape(nb, t_len, ATTN_DIM), ((0, 0), (0, 8 - t_len), (0, 0)))
    q8, k8, v8 = pad8(q_s), pad8(k_s), pad8(v_s)
    sample_ops = []
    for g, (cache, (_, d)) in enumerate(zip((cache_kv1, cache_kv2, cache_kv3), DIL_PAIRS)):
        buf = _buffer_to_channel_major(cache[0])
        bc, bn = _sample_bias(tables[g], buf.shape[-1], d, t_len)
        sample_ops.append((q8, k8, v8, kt_s, vt_s, buf, bc, bn, g, t_len))

    host_out = N_DIL - 1 if B * (S // min(OUT_TILE, S)) == nb else None
    inproj_steps = B * (S // INPROJ_TILE)
    host_in = [g for g in range(N_DIL) if g != host_out and nb % inproj_steps == 0
               and (LANES // t_len) % (nb // inproj_steps) == 0]
    sample_res = {}
    (convn, q1, q2, q3, k1, k2, k3, v1, v2, v3, nconv_p, kv1_p, kv2_p, kv3_p, *hosted_res) = _inproj_prompt(
        x_prompt, mod_p, norm1_g, w_in_b, conv_w[0], gn_conv, samples=[sample_ops[g] for g in host_in])
    for j, g in enumerate(host_in):
        sample_res[g] = hosted_res[j * N_SAMPLE_OUTPUTS:(j + 1) * N_SAMPLE_OUTPUTS]
    os_, ls_ = [], []
    for g, (qg, kg, vg) in enumerate(((q1, k1, v1), (q2, k2, v2), (q3, k3, v3))):
        _, d, M, _ = qg.shape
        merge = lambda a: a.reshape(B * d, M, GROUP_DIM)
        o, l = _attn_prompt(merge(qg), merge(kg), merge(vg), *_prompt_bias(tables[g]))
        os_.append(o.reshape(B, d, M, GROUP_DIM))
        ls_.append(l.reshape(B, d, M, LANES))
    outs = _out(x_prompt, mod_p, convn, os_, ls_, gn_attn, w_out_b, norm2_g, w_gate_b, w_up_b, w_down_b,
                final_g2, sample=None if host_out is None else sample_ops[host_out])
    y_prompt = outs
    if host_out is not None:
        y_prompt, *sample_res[host_out] = outs

    os_s, ls_s, new_bufs = [], [], []
    for g in range(N_DIL):
        o, l, nbuf = sample_res[g] if g in sample_res else _attn_sample(*sample_ops[g])
        os_s.append(o[:, :t_len].reshape(1, 1, rows_s, GROUP_DIM))
        ls_s.append(l[:, :t_len].reshape(1, 1, rows_s, LANES))
        new_bufs.append(_buffer_from_channel_major(nbuf))
    y_sample = _out(xs[None], mod_s[None], convn_s[None], os_s, ls_s, gn_attn, w_out_b, norm2_g,
                    w_gate_b, w_up_b, w_down_b, final_g2)
    y_sample = y_sample.reshape(nb, t_len, D_MODEL)
    nconv_s = u_s.reshape(nb, t_len, CONV_DIM)[:, t_len - 2:]

    return (y_prompt, y_sample, nconv_p[None],
            _buffer_from_channel_major(kv1_p), _buffer_from_channel_major(kv2_p),
            _buffer_from_channel_major(kv3_p),
            nconv_s[None], new_bufs[0], new_bufs[1], new_bufs[2])
```

```python
import functools

import numpy as np
import jax
import jax.numpy as jnp
from jax import lax
from jax.experimental import pallas as pl
from jax.experimental.pallas import tpu as pltpu

F32 = jnp.float32
BF16 = jnp.bfloat16

D_MODEL = 1024
HEAD_DIM = 64
CONV_DIM = 256
ATTN_DIM = 768
GROUP_HEADS = 4
GROUP_DIM = GROUP_HEADS * HEAD_DIM
DIL_PAIRS = ((128, 1), (512, 4), (2048, 16))
N_DIL = len(DIL_PAIRS)
QB = 128
N_BUCKETS = 32
MAX_DISTANCE = 2048
D_FF = 2816
PROJ_DIM = 3072
EPS = 1e-6
NEG_INF = -1e30
ATTN_SCALE = HEAD_DIM ** -0.5

LANES = 128
VMEM_LIMIT = 56 * 1024 * 1024
HOST_VMEM_LIMIT = 62 * 1024 * 1024


def _cparams(sem):
    return pltpu.CompilerParams(dimension_semantics=sem, vmem_limit_bytes=VMEM_LIMIT)


def _const_spec(shape):
    nd = len(shape)
    return pl.BlockSpec(shape, lambda *_: (0,) * nd, pipeline_mode=pl.Buffered(1))


def _head_masks(width, dtype):
    lane = lax.broadcasted_iota(jnp.int32, (1, width), 1)
    if width == GROUP_DIM:
        lane = lane // HEAD_DIM
    return [(lane == h).astype(dtype) for h in range(GROUP_HEADS)]


def _ada_kernel(c_ref, w_ref, b_ref, op_ref, os_ref):
    c = c_ref[...]
    s = (c * jax.nn.sigmoid(c)).astype(BF16)
    mod = jnp.dot(s, w_ref[...].astype(BF16), preferred_element_type=F32) + b_ref[...]
    rows_p = op_ref.shape[0]
    op_ref[...] = mod[:rows_p]
    os_ref[...] = mod[rows_p:]


def _ada(c_all, w_ada, b_ada, rows_p):
    rows = c_all.shape[0]
    tn = 1536
    assert rows_p % 8 == 0
    return pl.pallas_call(
        _ada_kernel,
        grid=(6 * D_MODEL // tn,),
        in_specs=[pl.BlockSpec((rows, D_MODEL), lambda j: (0, 0)),
                  pl.BlockSpec((D_MODEL, tn), lambda j: (0, j)),
                  pl.BlockSpec((1, tn), lambda j: (0, j))],
        out_specs=[pl.BlockSpec((rows_p, tn), lambda j: (0, j)),
                   pl.BlockSpec((rows - rows_p, tn), lambda j: (0, j))],
        out_shape=[jax.ShapeDtypeStruct((rows_p, 6 * D_MODEL), F32),
                   jax.ShapeDtypeStruct((rows - rows_p, 6 * D_MODEL), F32)],
        compiler_params=_cparams(("arbitrary",)),
        name="ada",
    )(c_all, w_ada, b_ada)


def _rms(x, g):
    return x * lax.rsqrt(jnp.mean(x * x, axis=-1, keepdims=True) + EPS) * g


STREAM_SLABS = sum(GROUP_DIM // LANES for _, d in DIL_PAIRS if d > 1)
INPROJ_PARTS = 2


def _store_streams(p, out_refs, slab_ref, slab0, r0):
    rows = p.shape[0]
    for g, ((_, d), out_ref) in enumerate(zip(DIL_PAIRS, out_refs)):
        lo = g * GROUP_DIM
        if d == 1:
            out_ref[0, 0, r0:r0 + rows, :] = p[:, lo:lo + GROUP_DIM].astype(BF16)
            continue
        for s in range(GROUP_DIM // LANES):
            slab_ref[slab0 + s] = p[:, lo + s * LANES:lo + (s + 1) * LANES]
        for r in range(d):
            for s in range(GROUP_DIM // LANES):
                out_ref[0, r, r0 // d:(r0 + rows) // d, s * LANES:(s + 1) * LANES] = (
                    slab_ref[slab0 + s, pl.ds(r, rows // d, stride=d), :].astype(BF16))
        slab0 += GROUP_DIM // LANES


N_INPROJ_INPUTS = 7
N_INPROJ_OUTPUTS = 14
N_SAMPLE_INPUTS = 8
N_SAMPLE_OUTPUTS = 3


def _inproj_prompt_kernel(*refs, tm, hosted):
    n_in = N_INPROJ_INPUTS + N_SAMPLE_INPUTS * len(hosted)
    x_ref, sh_ref, sc_ref, g1_ref, w_ref, cw_ref, gnc_ref = refs[:N_INPROJ_INPUTS]
    (convn_ref, q1_ref, q2_ref, q3_ref, k1_ref, k2_ref, k3_ref, v1_ref, v2_ref, v3_ref,
     nconv_ref, kv1_ref, kv2_ref, kv3_ref) = refs[n_in:n_in + N_INPROJ_OUTPUTS]
    ubuf_ref, slab_ref = refs[-2:]
    i = pl.program_id(1)
    n = pl.num_programs(1)
    pr = tm // INPROJ_PARTS
    parts = range(INPROJ_PARTS)
    c0 = 3 * CONV_DIM
    cw = cw_ref[...]

    hbs = [(_rms(x_ref[0, a * pr:(a + 1) * pr, :], g1_ref[...]) * (1.0 + sc_ref[0]) + sh_ref[0]).astype(BF16)
           for a in parts]
    projs = [jnp.dot(hb, w_ref[...], preferred_element_type=F32) for hb in hbs]

    for j, dims in enumerate(hosted):
        s_in = refs[N_INPROJ_INPUTS + j * N_SAMPLE_INPUTS:N_INPROJ_INPUTS + (j + 1) * N_SAMPLE_INPUTS]
        s_out = refs[n_in + N_INPROJ_OUTPUTS + j * N_SAMPLE_OUTPUTS:
                     n_in + N_INPROJ_OUTPUTS + (j + 1) * N_SAMPLE_OUTPUTS]
        _attn_sample_rows(pl.program_id(0) * n + i, *s_in, *s_out, *dims)

    tail = ubuf_ref[tm:tm + 8, :]
    ubuf_ref[0:8, :] = jnp.where(i == 0, jnp.zeros_like(tail), tail)
    for a in parts:
        r0 = a * pr
        p = projs[a]
        gb = p[:, 0:CONV_DIM]
        u = p[:, CONV_DIM:2 * CONV_DIM] * p[:, 2 * CONV_DIM:c0]
        ubuf_ref[8 + r0:8 + r0 + pr, :] = u
        um1 = ubuf_ref[7 + r0:7 + r0 + pr, :]
        um2 = ubuf_ref[6 + r0:6 + r0 + pr, :]
        z = cw[0:1, :] * um2 + cw[1:2, :] * um1 + cw[2:3, :] * u
        convn_ref[0, r0:r0 + pr, :] = _rms(gb * z, gnc_ref[...]).astype(BF16)
        slab0 = a * 3 * STREAM_SLABS
        _store_streams(p[:, c0:c0 + ATTN_DIM] * ATTN_SCALE, (q1_ref, q2_ref, q3_ref), slab_ref, slab0, r0)
        _store_streams(p[:, c0 + ATTN_DIM:c0 + 2 * ATTN_DIM], (k1_ref, k2_ref, k3_ref), slab_ref,
                       slab0 + STREAM_SLABS, r0)
        _store_streams(p[:, c0 + 2 * ATTN_DIM:c0 + 3 * ATTN_DIM], (v1_ref, v2_ref, v3_ref), slab_ref,
                       slab0 + 2 * STREAM_SLABS, r0)
    nconv_ref[0] = ubuf_ref[tm + 6:tm + 8, :]

    def put(ref, g, rows):
        for a in parts:
            lo_r, hi_r = max(tm - rows, a * pr), (a + 1) * pr
            if hi_r <= lo_r:
                continue
            for kv in range(2):
                lo = c0 + (1 + kv) * ATTN_DIM + g * GROUP_DIM
                ref[0, kv, :, lo_r - (tm - rows):hi_r - (tm - rows)] = (
                    projs[a][lo_r - a * pr:hi_r - a * pr, lo:lo + GROUP_DIM].T)

    w3 = DIL_PAIRS[2][0]

    @pl.when(i >= n - w3 // tm)
    def _():
        put(kv3_ref, 2, tm)

    @pl.when(i == n - 1)
    def _():
        put(kv2_ref, 1, DIL_PAIRS[1][0])
        put(kv1_ref, 0, DIL_PAIRS[0][0])


INPROJ_TILE = 1024


def _inproj_prompt(x, mod, g1, w_in, conv_w, gn_conv, samples=()):
    B, S, _ = x.shape
    tm = INPROJ_TILE
    n = S // tm
    w1, w2, w3 = (w for w, _ in DIL_PAIRS)
    assert S % tm == 0 and w3 % tm == 0 and tm >= w2 and tm % (INPROJ_PARTS * LANES) == 0
    n3 = w3 // tm
    row = lambda b, i: (b, i, 0)
    stream_specs = [pl.BlockSpec((1, d, tm // d, GROUP_DIM), lambda b, i: (b, 0, i, 0)) for _, d in DIL_PAIRS]
    stream_shapes = [jax.ShapeDtypeStruct((B, d, S // d, GROUP_DIM), BF16) for _, d in DIL_PAIRS]
    in_specs = [pl.BlockSpec((1, tm, D_MODEL), row),
                pl.BlockSpec((1, 1, D_MODEL), lambda b, i: (b, 0, 0)),
                pl.BlockSpec((1, 1, D_MODEL), lambda b, i: (b, 0, 1)),
                _const_spec((1, D_MODEL)),
                _const_spec((D_MODEL, PROJ_DIM)),
                _const_spec((3, CONV_DIM)),
                _const_spec((1, CONV_DIM))]
    out_specs = [pl.BlockSpec((1, tm, CONV_DIM), row)] + stream_specs * 3 + [
        pl.BlockSpec((1, 2, CONV_DIM), lambda b, i: (b, 0, 0)),
        pl.BlockSpec((1, 2, GROUP_DIM, w1), lambda b, i: (b, 0, 0, 0)),
        pl.BlockSpec((1, 2, GROUP_DIM, w2), lambda b, i: (b, 0, 0, 0)),
        pl.BlockSpec((1, 2, GROUP_DIM, tm), lambda b, i: (b, 0, 0, jnp.maximum(i - (n - n3), 0)))]
    out_shapes = [jax.ShapeDtypeStruct((B, S, CONV_DIM), BF16)] + stream_shapes * 3 + [
        jax.ShapeDtypeStruct((B, 2, CONV_DIM), F32),
        jax.ShapeDtypeStruct((B, 2, GROUP_DIM, w1), F32),
        jax.ShapeDtypeStruct((B, 2, GROUP_DIM, w2), F32),
        jax.ShapeDtypeStruct((B, 2, GROUP_DIM, w3), F32)]
    operands = [x, mod, mod, g1, w_in, conv_w, gn_conv]
    assert len(operands) == N_INPROJ_INPUTS and len(out_specs) == N_INPROJ_OUTPUTS
    hosted = []
    for q8, k8, v8, kt, vt, cache, bc, bn, g, t_len in samples:
        nbb = cache.shape[0] // (B * n)
        assert nbb * B * n == cache.shape[0]
        s_in, s_out, s_shapes = _attn_sample_specs(cache.shape, bc, bn, g, t_len, nbb, lambda b, i: b * n + i)
        in_specs += s_in
        operands += [q8, k8, v8, kt, vt, cache, bc, bn]
        out_specs += s_out
        out_shapes += s_shapes
        hosted.append((cache.shape[-1], t_len))
    return pl.pallas_call(
        functools.partial(_inproj_prompt_kernel, tm=tm, hosted=tuple(hosted)),
        grid=(B, n),
        in_specs=in_specs, out_specs=out_specs, out_shape=out_shapes,
        scratch_shapes=[pltpu.VMEM((tm + 8, CONV_DIM), F32),
                        pltpu.VMEM((INPROJ_PARTS * 3 * STREAM_SLABS, tm // INPROJ_PARTS, LANES), F32)],
        compiler_params=pltpu.CompilerParams(dimension_semantics=("arbitrary", "arbitrary"),
                                             vmem_limit_bytes=HOST_VMEM_LIMIT if hosted else VMEM_LIMIT),
        name="inproj_prompt",
    )(*operands)


def _inproj_sample_kernel(x_ref, sh_ref, sc_ref, g1_ref, w_ref, cw_ref, gnc_ref, p0_ref, p1_ref,
                          convn_ref, u_ref, q_ref, k_ref, v_ref, kt_ref, vt_ref, ubuf_ref, *, tm, t_len):
    h = _rms(x_ref[...], g1_ref[...]) * (1.0 + sc_ref[...]) + sh_ref[...]
    hb = h.astype(BF16)

    pc = jnp.dot(hb, w_ref[:, 0:3 * CONV_DIM], preferred_element_type=F32)
    gb = pc[:, 0:CONV_DIM]
    u = pc[:, CONV_DIM:2 * CONV_DIM] * pc[:, 2 * CONV_DIM:3 * CONV_DIM]
    u_ref[...] = u

    ubuf_ref[0:8, :] = jnp.zeros((8, CONV_DIM), F32)
    ubuf_ref[8:tm + 8, :] = u
    t = lax.broadcasted_iota(jnp.int32, (tm, 1), 0) % t_len
    um1 = jnp.where(t == 0, p1_ref[...], ubuf_ref[7:tm + 7, :])
    um2 = jnp.where(t == 0, p0_ref[...], jnp.where(t == 1, p1_ref[...], ubuf_ref[6:tm + 6, :]))
    cw = cw_ref[...]
    z = cw[0:1, :] * um2 + cw[1:2, :] * um1 + cw[2:3, :] * u
    convn_ref[...] = _rms(gb * z, gnc_ref[...]).astype(BF16)

    c0 = 3 * CONV_DIM
    pq = jnp.dot(hb, w_ref[:, c0:c0 + ATTN_DIM], preferred_element_type=F32)
    q_ref[...] = pq * ATTN_SCALE
    pk = jnp.dot(hb, w_ref[:, c0 + ATTN_DIM:c0 + 2 * ATTN_DIM], preferred_element_type=F32)
    k_ref[...] = pk
    kt_ref[...] = pk.T
    pv = jnp.dot(hb, w_ref[:, c0 + 2 * ATTN_DIM:c0 + 3 * ATTN_DIM], preferred_element_type=F32)
    v_ref[...] = pv
    vt_ref[...] = pv.T


def _inproj_sample(x, mod, g1, w_in, conv_w, gn_conv, prev0, prev1, t_len):
    tm = x.shape[0]
    full = lambda shape: pl.BlockSpec(shape, lambda i: (0,) * len(shape))
    return pl.pallas_call(
        functools.partial(_inproj_sample_kernel, tm=tm, t_len=t_len),
        grid=(1,),
        in_specs=[full((tm, D_MODEL)),
                  pl.BlockSpec((tm, D_MODEL), lambda i: (0, 0)),
                  pl.BlockSpec((tm, D_MODEL), lambda i: (0, 1)),
                  full((1, D_MODEL)), full((D_MODEL, PROJ_DIM)), full((3, CONV_DIM)), full((1, CONV_DIM)),
                  full((tm, CONV_DIM)), full((tm, CONV_DIM))],
        out_specs=[full((tm, CONV_DIM)), full((tm, CONV_DIM)),
                   full((tm, ATTN_DIM)), full((tm, ATTN_DIM)), full((tm, ATTN_DIM)),
                   full((ATTN_DIM, tm)), full((ATTN_DIM, tm))],
        out_shape=[jax.ShapeDtypeStruct((tm, CONV_DIM), BF16),
                   jax.ShapeDtypeStruct((tm, CONV_DIM), F32),
                   jax.ShapeDtypeStruct((tm, ATTN_DIM), F32),
                   jax.ShapeDtypeStruct((tm, ATTN_DIM), F32),
                   jax.ShapeDtypeStruct((tm, ATTN_DIM), F32),
                   jax.ShapeDtypeStruct((ATTN_DIM, tm), F32),
                   jax.ShapeDtypeStruct((ATTN_DIM, tm), F32)],
        scratch_shapes=[pltpu.VMEM((tm + 8, CONV_DIM), F32)],
        compiler_params=_cparams(("arbitrary",)),
        name="inproj_sample",
    )(x, mod, mod, g1, w_in, conv_w, gn_conv, prev0, prev1)


def _t5_buckets(dist):
    dist = np.asarray(dist, np.int32)
    max_exact = N_BUCKETS // 2
    large = max_exact + (np.log(np.maximum(dist, 1).astype(np.float32) / max_exact)
                         / np.log(MAX_DISTANCE / max_exact) * (N_BUCKETS - max_exact)).astype(np.int32)
    large = np.minimum(large, N_BUCKETS - 1)
    return np.where(dist < max_exact, dist, large).astype(np.int32)


def _group_table(rel_bias, g):
    w, d = DIL_PAIRS[g]
    buckets = _t5_buckets(np.arange(w // d + 1) * d)
    return rel_bias[buckets][:, g * GROUP_HEADS:(g + 1) * GROUP_HEADS].T.astype(F32)


def _prompt_bias(table):
    period = 3 * QB - 1
    base = jnp.concatenate([table[:, ::-1], jnp.full((GROUP_HEADS, period - QB - 1), NEG_INF, F32)], axis=1)
    skew = jnp.tile(base, (1, QB))[:, :QB * (period - 1)].reshape(GROUP_HEADS, QB, period - 1)
    prev, cur = skew[:, :, 0:QB], skew[:, :, QB:2 * QB]
    tri = (np.arange(QB)[None, :] <= np.arange(QB)[:, None])[None]
    fold = jnp.stack([jnp.where(tri, cur, prev), jnp.where(tri, cur, NEG_INF)])
    fold = fold.transpose(0, 3, 1, 2).reshape(2, QB, GROUP_HEADS * QB)
    lone = jnp.broadcast_to(table[:, QB:QB + 1], (GROUP_HEADS, QB)).reshape(1, 1, GROUP_HEADS * QB)
    return fold, jnp.concatenate([lone, jnp.full_like(lone, NEG_INF)], axis=0)


def _sample_bias(table, buf_len, d, t_len):
    assert buf_len == QB * d
    asc = table[:, ::-1]
    inter = jnp.concatenate([asc[:, :, None], jnp.full((GROUP_HEADS, QB + 1, d - 1), NEG_INF, F32)],
                            axis=2).reshape(GROUP_HEADS, (QB + 1) * d)
    ext_len = buf_len + LANES
    rows = []
    for t in range(8):
        lead = t if t < t_len else 0
        rows.append(jnp.pad(inter, ((0, 0), (lead, ext_len - lead - (QB + 1) * d)), constant_values=NEG_INF))
    ext = jnp.stack(rows, axis=1)
    return (ext[:, :, :buf_len].reshape(GROUP_HEADS * 8, buf_len),
            ext[:, :, buf_len:].reshape(GROUP_HEADS * 8, LANES))


def _attn_prompt_kernel(q_ref, kc_ref, vc_ref, kp_ref, vp_ref, bias_ref, dbias_ref, o_ref, lse_ref):
    i = pl.program_id(1)
    sb, tq, _ = q_ref.shape
    hm_b = _head_masks(GROUP_DIM, BF16)
    first = jnp.where(i == 0, 1, 0)
    c_idx = lax.broadcasted_iota(jnp.int32, (QB, GROUP_HEADS * QB), 0)
    q_idx = lax.broadcasted_iota(jnp.int32, (QB, GROUP_HEADS * QB), 1) & (QB - 1)
    tri = c_idx <= q_idx
    tri_b = tri.astype(F32).astype(BF16)
    diag_f = (c_idx == q_idx).astype(F32)
    diag_b = diag_f.astype(BF16)
    work = [(s, j) for s in range(sb) for j in range(tq // QB)]

    def scores(s, j):
        q = q_ref[s, j * QB:(j + 1) * QB, :]
        q4 = jnp.concatenate([q * hm_b[h] for h in range(GROUP_HEADS)], axis=0)
        if j == 0:
            kk = jnp.concatenate([kp_ref[s], kc_ref[s, 0:QB, :]], axis=0)
        else:
            kk = kc_ref[s, (j - 1) * QB:(j + 1) * QB, :]
        return lax.dot_general(kk, q4, (((1,), (1,)), ((), ())), preferred_element_type=F32)

    def values_t(s, j):
        if j == 0:
            return jnp.concatenate([vp_ref[s].T, vc_ref[s, 0:QB, :].T], axis=1)
        return vc_ref[s, (j - 1) * QB:(j + 1) * QB, :].T

    st_next = scores(*work[0])
    for n, (s, j) in enumerate(work):
        st = st_next
        if n + 1 < len(work):
            st_next = scores(*work[n + 1])
        slot = first if j == 0 else 0
        sp = st[0:QB]
        sc = st[QB:2 * QB]
        w = jnp.where(tri, sc, sp) + bias_ref[slot]
        sd = jnp.sum(sp * diag_f, axis=0, keepdims=True) + dbias_ref[slot]
        m = jnp.maximum(jnp.max(w, axis=0, keepdims=True), sd)
        e = jnp.exp(w - m)
        ed = jnp.exp(sd - m)
        den = jnp.sum(e, axis=0, keepdims=True) + ed
        lse = m + jnp.log(den)
        rden = 1.0 / den
        e_b = e.astype(BF16)
        p_cur = e_b * tri_b
        p_prev = (e_b - p_cur) + jnp.broadcast_to(ed.astype(BF16), e_b.shape) * diag_b
        pt = jnp.concatenate([p_prev, p_cur], axis=0)
        vt = values_t(s, j)
        ots = []
        for h in range(GROUP_HEADS):
            cols = slice(h * QB, (h + 1) * QB)
            rows = slice(h * HEAD_DIM, (h + 1) * HEAD_DIM)
            ot = jnp.dot(vt[rows, :], pt[:, cols], preferred_element_type=F32)
            ots.append(ot * rden[:, cols])
        o_ref[s, j * QB:(j + 1) * QB, :] = jnp.concatenate(ots, axis=0).T.astype(o_ref.dtype)
        lrows = jnp.concatenate([lse[:, h * QB:(h + 1) * QB] for h in range(GROUP_HEADS)]
                                + [jnp.zeros((LANES - GROUP_HEADS, QB), F32)], axis=0)
        lse_ref[s, j * QB:(j + 1) * QB, :] = lrows.T


def _attn_prompt(q, k, v, bias, dbias, block_rows=4096):
    ns, M, _ = q.shape
    tq = min(block_rows, M)
    sb = block_rows // tq
    r = tq // QB
    cur = pl.BlockSpec((sb, tq, GROUP_DIM), lambda s, i: (s, i, 0))
    prev = pl.BlockSpec((sb, QB, GROUP_DIM), lambda s, i: (s, jnp.maximum(i * r - 1, 0), 0))
    return pl.pallas_call(
        _attn_prompt_kernel,
        grid=(ns // sb, M // tq),
        in_specs=[cur, cur, cur, prev, prev, _const_spec(bias.shape), _const_spec(dbias.shape)],
        out_specs=[pl.BlockSpec((sb, tq, GROUP_DIM), lambda s, i: (s, i, 0)),
                   pl.BlockSpec((sb, tq, LANES), lambda s, i: (s, i, 0))],
        out_shape=[jax.ShapeDtypeStruct((ns, M, GROUP_DIM), BF16),
                   jax.ShapeDtypeStruct((ns, M, LANES), F32)],
        compiler_params=_cparams(("arbitrary", "arbitrary")),
        name="attn_prompt",
    )(q, k, v, k, v, bias, dbias)


def _attn_sample_kernel(q_ref, k8_ref, v8_ref, kt_ref, vt_ref, cache_ref, bc_ref, bn_ref,
                        o_ref, lse_ref, newc_ref, *, buf_len, t_len):
    _attn_sample_rows(pl.program_id(0), q_ref, k8_ref, v8_ref, kt_ref, vt_ref, cache_ref, bc_ref, bn_ref,
                      o_ref, lse_ref, newc_ref, buf_len, t_len)


def _attn_sample_rows(step, q_ref, k8_ref, v8_ref, kt_ref, vt_ref, cache_ref, bc_ref, bn_ref,
                      o_ref, lse_ref, newc_ref, buf_len, t_len):
    rows = range(q_ref.shape[0])
    hm_f = _head_masks(GROUP_DIM, F32)
    lm_f = _head_masks(LANES, F32)
    lane = lax.broadcasted_iota(jnp.int32, (1, LANES), 1)

    q4bs = [jnp.concatenate([q_ref[i] * hm_f[h] for h in range(GROUP_HEADS)], axis=0).astype(BF16)
            for i in rows]
    s_cs = [jnp.dot(q4bs[i], cache_ref[i, 0].astype(BF16), preferred_element_type=F32) + bc_ref[...]
            for i in rows]

    stats = []
    for i in rows:
        q4r = q4bs[i].astype(F32)
        k8r = k8_ref[i].astype(BF16).astype(F32)
        s_n = bn_ref[...]
        for t in range(t_len):
            col = jnp.sum(q4r * k8r[t:t + 1, :], axis=-1, keepdims=True)
            s_n = s_n + jnp.where(lane == t, col, 0.0)
        m = jnp.maximum(jnp.max(s_cs[i], axis=-1, keepdims=True), jnp.max(s_n, axis=-1, keepdims=True))
        e_c = jnp.exp(s_cs[i] - m)
        e_n = jnp.exp(s_n - m)
        den = jnp.sum(e_c, axis=-1, keepdims=True) + jnp.sum(e_n, axis=-1, keepdims=True)
        stats.append((m, e_c, e_n, den))

    pvs = [lax.dot_general(stats[i][1].astype(BF16), cache_ref[i, 1].astype(BF16), (((1,), (1,)), ((), ())),
                           preferred_element_type=F32) for i in rows]
    for i in rows:
        m, _, e_n, den = stats[i]
        pv = pvs[i]
        v8r = v8_ref[i].astype(BF16).astype(F32)
        e_nr = e_n.astype(BF16).astype(F32)
        for t in range(t_len):
            w_t = jnp.sum(jnp.where(lane == t, e_nr, 0.0), axis=-1, keepdims=True)
            pv = pv + w_t * v8r[t:t + 1, :]
        pv = pv / den
        lse = m + jnp.log(den)
        o = jnp.zeros((8, GROUP_DIM), F32)
        l = jnp.zeros((8, LANES), F32)
        for h in range(GROUP_HEADS):
            o = o + pv[h * 8:(h + 1) * 8, :] * hm_f[h]
            l = l + lse[h * 8:(h + 1) * 8, :] * lm_f[h]
        o_ref[i] = o
        lse_ref[i] = l

    keep = LANES - t_len
    per_block = LANES // t_len
    for i in rows:
        shift_new = keep - ((step * len(rows) + i) % per_block) * t_len
        for kv, new_ref in ((0, kt_ref), (1, vt_ref)):
            new_r = pltpu.roll(new_ref[...], shift_new, 1)
            prev_r = None
            for c in range(buf_len // LANES):
                r_c = pltpu.roll(cache_ref[i, kv, :, c * LANES:(c + 1) * LANES], keep, 1)
                if c > 0:
                    newc_ref[i, kv, :, (c - 1) * LANES:c * LANES] = jnp.where(lane < keep, prev_r, r_c)
                prev_r = r_c
            newc_ref[i, kv, :, buf_len - LANES:buf_len] = jnp.where(lane < keep, prev_r, new_r)


SAMPLE_STEP_POSITIONS = 2048
SAMPLE_STEP_ROWS = 8


def _attn_sample_specs(cache_shape, bc, bn, g, t_len, nbb, step_of):
    nb, _, _, buf_len = cache_shape
    per_block = LANES // t_len
    assert nb % nbb == 0 and per_block % nbb == 0
    new_spec = pl.BlockSpec((nbb, 8, GROUP_DIM), lambda *i: (step_of(*i), 0, g))
    newt_spec = pl.BlockSpec((GROUP_DIM, LANES), lambda *i: (g, (step_of(*i) * nbb) // per_block))
    cache_spec = pl.BlockSpec((nbb, 2, GROUP_DIM, buf_len), lambda *i: (step_of(*i), 0, 0, 0))
    in_specs = [new_spec, new_spec, new_spec, newt_spec, newt_spec, cache_spec,
                _const_spec(bc.shape), _const_spec(bn.shape)]
    out_specs = [pl.BlockSpec((nbb, 8, GROUP_DIM), lambda *i: (step_of(*i), 0, 0)),
                 pl.BlockSpec((nbb, 8, LANES), lambda *i: (step_of(*i), 0, 0)),
                 cache_spec]
    out_shapes = [jax.ShapeDtypeStruct((nb, 8, GROUP_DIM), F32),
                  jax.ShapeDtypeStruct((nb, 8, LANES), F32),
                  jax.ShapeDtypeStruct(cache_shape, F32)]
    return in_specs, out_specs, out_shapes


def _attn_sample(q8, k8, v8, kt, vt, cache, bc, bn, g, t_len):
    nb, _, _, buf_len = cache.shape
    nbb = max(1, min(LANES // t_len, SAMPLE_STEP_ROWS, SAMPLE_STEP_POSITIONS // buf_len))
    in_specs, out_specs, out_shapes = _attn_sample_specs(cache.shape, bc, bn, g, t_len, nbb, lambda b: b)
    return pl.pallas_call(
        functools.partial(_attn_sample_kernel, buf_len=buf_len, t_len=t_len),
        grid=(nb // nbb,),
        in_specs=in_specs, out_specs=out_specs, out_shape=out_shapes,
        compiler_params=_cparams(("arbitrary",)),
        name="attn_sample",
    )(q8, k8, v8, kt, vt, cache, bc, bn)


OUT_PARTS = 2
MERGE_SLABS = N_DIL * (GROUP_DIM + LANES) // LANES


def _load_positions(ref, slab_ref, slab0, r0, rows):
    _, d, _, width = ref.shape
    if d == 1:
        return ref[0, 0, r0:r0 + rows, :].astype(F32)
    for r in range(d):
        for s in range(width // LANES):
            slab_ref[slab0 + s, pl.ds(r, rows // d, stride=d), :] = (
                ref[0, r, r0 // d:(r0 + rows) // d, s * LANES:(s + 1) * LANES].astype(F32))
    return jnp.concatenate([slab_ref[slab0 + s] for s in range(width // LANES)], axis=-1)


N_OUT_INPUTS = 19


def _out_kernel(*refs, hosted):
    (x_ref, ga1_ref, sh2_ref, sc2_ref, ga2_ref, convn_ref, o1_ref, o2_ref, o3_ref, l1_ref, l2_ref, l3_ref,
     gna_ref, wout_ref, g2_ref, wg_ref, wu_ref, wd_ref, fg_ref) = refs[:N_OUT_INPUTS]
    if hosted is None:
        y_ref, slab_ref = refs[N_OUT_INPUTS:]
    else:
        sample_in = refs[N_OUT_INPUTS:N_OUT_INPUTS + N_SAMPLE_INPUTS]
        y_ref, *sample_out, slab_ref = refs[N_OUT_INPUTS + N_SAMPLE_INPUTS:]
    tm = x_ref.shape[1]
    pr = tm // OUT_PARTS
    parts = range(OUT_PARTS)
    hm_f = _head_masks(GROUP_DIM, F32)
    lm_f = _head_masks(LANES, F32)
    gna = gna_ref[...]

    def mod_rows(ref, a):
        return ref[0] if ref.shape[1] == 1 else ref[0, a * pr:(a + 1) * pr, :]

    def merge(a):
        r0 = a * pr
        slab0 = a * MERGE_SLABS
        ls = [_load_positions(l_ref, slab_ref, slab0 + g, r0, pr)
              for g, l_ref in enumerate((l1_ref, l2_ref, l3_ref))]
        m = jnp.maximum(jnp.maximum(ls[0], ls[1]), ls[2])
        es = [jnp.exp(l - m) for l in ls]
        tot = es[0] + es[1] + es[2]
        aos = []
        ssq = jnp.zeros((pr, 1), F32)
        for g, o_ref in enumerate((o1_ref, o2_ref, o3_ref)):
            alpha = es[g] / tot
            wide = jnp.zeros((pr, GROUP_DIM), F32)
            for h in range(GROUP_HEADS):
                a_h = jnp.sum(alpha * lm_f[h], axis=-1, keepdims=True)
                wide = wide + a_h * hm_f[h]
            ao = wide * _load_positions(o_ref, slab_ref, slab0 + N_DIL + g * (GROUP_DIM // LANES), r0, pr)
            ssq = ssq + jnp.sum(ao * ao, axis=-1, keepdims=True)
            aos.append(ao)
        rinv = lax.rsqrt(ssq / ATTN_DIM + EPS)
        return jnp.concatenate(
            [convn_ref[0, r0:r0 + pr, :]]
            + [(aos[g] * rinv * gna[:, g * GROUP_DIM:(g + 1) * GROUP_DIM]).astype(BF16) for g in range(N_DIL)],
            axis=-1)

    def out_proj(a, mixed):
        mix = jnp.dot(mixed, wout_ref[...], preferred_element_type=F32)
        x1 = x_ref[0, a * pr:(a + 1) * pr, :] + mod_rows(ga1_ref, a) * mix
        h2 = (_rms(x1, g2_ref[...]) * (1.0 + mod_rows(sc2_ref, a)) + mod_rows(sh2_ref, a)).astype(BF16)
        return x1, h2

    def swiglu(h2):
        gate = jnp.dot(h2, wg_ref[...], preferred_element_type=F32)
        up = jnp.dot(h2, wu_ref[...], preferred_element_type=F32)
        act = (gate * jax.nn.sigmoid(gate) * up).astype(BF16)
        return jnp.dot(act, wd_ref[...], preferred_element_type=F32)

    if hosted is not None:
        step = pl.program_id(0) * pl.num_programs(1) + pl.program_id(1)
        _attn_sample_rows(step, *sample_in, *sample_out, *hosted)
    mixed = [merge(a) for a in parts]
    x1h2 = [out_proj(a, mixed[a]) for a in parts]
    ffn = [swiglu(h2) for _, h2 in x1h2]
    for a in parts:
        x2 = x1h2[a][0] + mod_rows(ga2_ref, a) * ffn[a]
        y_ref[0, a * pr:(a + 1) * pr, :] = _rms(x2, fg_ref[...])


OUT_TILE = 512


def _out(x, mod, convn, os_, ls_, gn_attn, w_out, g2, w_gate, w_up, w_down, final_g, sample=None):
    nb, R, _ = x.shape
    tm = min(OUT_TILE, R)
    n_i = R // tm
    per_row = mod.shape[1] != 1
    mrows = tm if per_row else 1
    row = lambda b, i: (b, i, 0)

    def mod_spec(c):
        return pl.BlockSpec((1, mrows, D_MODEL), lambda b, i: (b, i if per_row else 0, c))

    def stream_spec(a):
        _, d, _, width = a.shape
        return pl.BlockSpec((1, d, tm // d, width), lambda b, i: (b, 0, i, 0))

    in_specs = ([pl.BlockSpec((1, tm, D_MODEL), row),
                 mod_spec(2), mod_spec(3), mod_spec(4), mod_spec(5),
                 pl.BlockSpec((1, tm, CONV_DIM), row)]
                + [stream_spec(a) for a in os_] + [stream_spec(a) for a in ls_]
                + [_const_spec((1, ATTN_DIM)), _const_spec((D_MODEL, D_MODEL)), _const_spec((1, D_MODEL)),
                   _const_spec((D_MODEL, D_FF)), _const_spec((D_MODEL, D_FF)), _const_spec((D_FF, D_MODEL)),
                   _const_spec((1, D_MODEL))])
    operands = [x, mod, mod, mod, mod, convn, *os_, *ls_, gn_attn, w_out, g2, w_gate, w_up, w_down, final_g]
    assert len(operands) == N_OUT_INPUTS
    out_specs = [pl.BlockSpec((1, tm, D_MODEL), row)]
    out_shapes = [jax.ShapeDtypeStruct(x.shape, F32)]
    hosted, limit = None, VMEM_LIMIT
    if sample is not None:
        q8, k8, v8, kt, vt, cache, bc, bn, g, t_len = sample
        assert cache.shape[0] == nb * n_i
        s_in, s_out, s_shapes = _attn_sample_specs(cache.shape, bc, bn, g, t_len, 1, lambda b, i: b * n_i + i)
        in_specs += s_in
        operands += [q8, k8, v8, kt, vt, cache, bc, bn]
        out_specs += s_out
        out_shapes += s_shapes
        hosted, limit = (cache.shape[-1], t_len), HOST_VMEM_LIMIT
    outs = pl.pallas_call(
        functools.partial(_out_kernel, hosted=hosted),
        grid=(nb, n_i),
        in_specs=in_specs, out_specs=out_specs, out_shape=out_shapes,
        scratch_shapes=[pltpu.VMEM((OUT_PARTS * MERGE_SLABS, tm // OUT_PARTS, LANES), F32)],
        compiler_params=pltpu.CompilerParams(dimension_semantics=("arbitrary", "arbitrary"),
                                             vmem_limit_bytes=limit),
        name="out",
    )(*operands)
    return outs[0] if sample is None else outs


def _buffer_to_channel_major(cache):
    B, L = cache.shape[:2]
    return cache.transpose(0, 2, 3, 4, 1).reshape(B, 2, GROUP_DIM, L)


def _buffer_from_channel_major(buf):
    B, _, _, L = buf.shape
    return buf.reshape(B, 2, GROUP_HEADS, HEAD_DIM, L).transpose(0, 4, 1, 2, 3)[None]


def kernel(x_prompt, x_sample, state_conv, cache_kv1, cache_kv2, cache_kv3, c_prompt, c_sample, w_ada, b_ada, norm1_g, norm2_g, w_in, conv_w, gn_conv, gn_attn, w_out, w_gate, w_up, w_down, rel_bias, final_g):
    B, S, _ = x_prompt.shape
    nb, t_len, _ = x_sample.shape
    rows_s = nb * t_len
    assert w_ada.shape[0] == 1 and LANES % t_len == 0 and 2 <= t_len <= 8 and rows_s % LANES == 0

    w_in_b = w_in[0].astype(BF16)
    w_out_b = w_out[0].astype(BF16)
    w_gate_b = w_gate[0].astype(BF16)
    w_up_b = w_up[0].astype(BF16)
    w_down_b = w_down[0].astype(BF16)
    final_g2 = final_g.reshape(1, D_MODEL)
    tables = [_group_table(rel_bias, g) for g in range(N_DIL)]

    c_all = jnp.concatenate([c_prompt, jnp.repeat(c_sample, t_len, axis=0)], axis=0)
    mod_p, mod_s = _ada(c_all, w_ada[0], b_ada, B)
    mod_p = mod_p.reshape(B, 1, 6 * D_MODEL)

    xs = x_sample.reshape(rows_s, D_MODEL)
    prev0 = jnp.repeat(state_conv[0, :, 0], t_len, axis=0)
    prev1 = jnp.repeat(state_conv[0, :, 1], t_len, axis=0)
    convn_s, u_s, q_s, k_s, v_s, kt_s, vt_s = _inproj_sample(
        xs, mod_s, norm1_g, w_in_b, conv_w[0], gn_conv, prev0, prev1, t_len)
    pad8 = lambda a: jnp.pad(a.reshape(nb, t_len, ATTN_DIM), ((0, 0), (0, 8 - t_len), (0, 0)))
    q8, k8, v8 = pad8(q_s), pad8(k_s), pad8(v_s)
    sample_ops = []
    for g, (cache, (_, d)) in enumerate(zip((cache_kv1, cache_kv2, cache_kv3), DIL_PAIRS)):
        buf = _buffer_to_channel_major(cache[0])
        bc, bn = _sample_bias(tables[g], buf.shape[-1], d, t_len)
        sample_ops.append((q8, k8, v8, kt_s, vt_s, buf, bc, bn, g, t_len))

    host_out = N_DIL - 1 if B * (S // min(OUT_TILE, S)) == nb else None
    inproj_steps = B * (S // INPROJ_TILE)
    host_in = [g for g in range(N_DIL) if g != host_out and nb % inproj_steps == 0
               and (LANES // t_len) % (nb // inproj_steps) == 0]
    sample_res = {}
    (convn, q1, q2, q3, k1, k2, k3, v1, v2, v3, nconv_p, kv1_p, kv2_p, kv3_p, *hosted_res) = _inproj_prompt(
        x_prompt, mod_p, norm1_g, w_in_b, conv_w[0], gn_conv, samples=[sample_ops[g] for g in host_in])
    for j, g in enumerate(host_in):
        sample_res[g] = hosted_res[j * N_SAMPLE_OUTPUTS:(j + 1) * N_SAMPLE_OUTPUTS]
    os_, ls_ = [], []
    for g, (qg, kg, vg) in enumerate(((q1, k1, v1), (q2, k2, v2), (q3, k3, v3))):
        _, d, M, _ = qg.shape
        merge = lambda a: a.reshape(B * d, M, GROUP_DIM)
        o, l = _attn_prompt(merge(qg), merge(kg), merge(vg), *_prompt_bias(tables[g]))
        os_.append(o.reshape(B, d, M, GROUP_DIM))
        ls_.append(l.reshape(B, d, M, LANES))
    outs = _out(x_prompt, mod_p, convn, os_, ls_, gn_attn, w_out_b, norm2_g, w_gate_b, w_up_b, w_down_b,
                final_g2, sample=None if host_out is None else sample_ops[host_out])
    y_prompt = outs
    if host_out is not None:
        y_prompt, *sample_res[host_out] = outs

    os_s, ls_s, new_bufs = [], [], []
    for g in range(N_DIL):
        o, l, nbuf = sample_res[g] if g in sample_res else _attn_sample(*sample_ops[g])
        os_s.append(o[:, :t_len].reshape(1, 1, rows_s, GROUP_DIM))
        ls_s.append(l[:, :t_len].reshape(1, 1, rows_s, LANES))
        new_bufs.append(_buffer_from_channel_major(nbuf))
    y_sample = _out(xs[None], mod_s[None], convn_s[None], os_s, ls_s, gn_attn, w_out_b, norm2_g,
                    w_gate_b, w_up_b, w_down_b, final_g2)
    y_sample = y_sample.reshape(nb, t_len, D_MODEL)
    nconv_s = u_s.reshape(nb, t_len, CONV_DIM)[:, t_len - 2:]

    return (y_prompt, y_sample, nconv_p[None],
            _buffer_from_channel_major(kv1_p), _buffer_from_channel_major(kv2_p),
            _buffer_from_channel_major(kv3_p),
            nconv_s[None], new_bufs[0], new_bufs[1], new_bufs[2])
```

```python
import functools

import numpy as np
import jax
import jax.numpy as jnp
from jax import lax
from jax.experimental import pallas as pl
from jax.experimental.pallas import tpu as pltpu

F32 = jnp.float32
BF16 = jnp.bfloat16

D_MODEL = 1024
HEAD_DIM = 64
CONV_DIM = 256
ATTN_DIM = 768
GROUP_HEADS = 4
GROUP_DIM = GROUP_HEADS * HEAD_DIM
DIL_PAIRS = ((128, 1), (512, 4), (2048, 16))
N_DIL = len(DIL_PAIRS)
QB = 128
N_BUCKETS = 32
MAX_DISTANCE = 2048
D_FF = 2816
PROJ_DIM = 3072
EPS = 1e-6
NEG_INF = -1e30
ATTN_SCALE = HEAD_DIM ** -0.5

LANES = 128
SUBLANES = 8
VMEM_LIMIT = 56 * 1024 * 1024
HOST_VMEM_LIMIT = 62 * 1024 * 1024


def _cparams(sem):
    return pltpu.CompilerParams(dimension_semantics=sem, vmem_limit_bytes=VMEM_LIMIT)


def _const_spec(shape):
    nd = len(shape)
    return pl.BlockSpec(shape, lambda *_: (0,) * nd, pipeline_mode=pl.Buffered(1))


def _head_masks(width, dtype):
    lane = lax.broadcasted_iota(jnp.int32, (1, width), 1)
    if width == GROUP_DIM:
        lane = lane // HEAD_DIM
    return [(lane == h).astype(dtype) for h in range(GROUP_HEADS)]


ADA_TILE = 1536


def _ada_kernel(c_ref, w_ref, b_ref, op_ref, os_ref):
    c = c_ref[...]
    s = (c * jax.nn.sigmoid(c)).astype(BF16)
    mod = jnp.dot(s, w_ref[...].astype(BF16), preferred_element_type=F32) + b_ref[...]
    rows_p = op_ref.shape[0]
    op_ref[...] = mod[:rows_p]
    os_ref[...] = mod[rows_p:]


def _ada(c_all, w_ada, b_ada, rows_p):
    rows = c_all.shape[0]
    tn = ADA_TILE
    assert rows_p % SUBLANES == 0 and (6 * D_MODEL) % tn == 0
    return pl.pallas_call(
        _ada_kernel,
        grid=(6 * D_MODEL // tn,),
        in_specs=[pl.BlockSpec((rows, D_MODEL), lambda j: (0, 0)),
                  pl.BlockSpec((D_MODEL, tn), lambda j: (0, j)),
                  pl.BlockSpec((1, tn), lambda j: (0, j))],
        out_specs=[pl.BlockSpec((rows_p, tn), lambda j: (0, j)),
                   pl.BlockSpec((rows - rows_p, tn), lambda j: (0, j))],
        out_shape=[jax.ShapeDtypeStruct((rows_p, 6 * D_MODEL), F32),
                   jax.ShapeDtypeStruct((rows - rows_p, 6 * D_MODEL), F32)],
        compiler_params=_cparams(("arbitrary",)),
        name="ada",
    )(c_all, w_ada, b_ada)


def _rms(x, g):
    return x * lax.rsqrt(jnp.mean(x * x, axis=-1, keepdims=True) + EPS) * g


STREAM_SLABS = sum(GROUP_DIM // LANES for _, d in DIL_PAIRS if d > 1)
INPROJ_PARTS = 2


def _store_streams(p, out_refs, slab_ref, slab0, r0):
    rows = p.shape[0]
    for g, ((_, d), out_ref) in enumerate(zip(DIL_PAIRS, out_refs)):
        lo = g * GROUP_DIM
        if d == 1:
            out_ref[0, 0, r0:r0 + rows, :] = p[:, lo:lo + GROUP_DIM].astype(BF16)
            continue
        for s in range(GROUP_DIM // LANES):
            slab_ref[slab0 + s] = p[:, lo + s * LANES:lo + (s + 1) * LANES]
        for r in range(d):
            for s in range(GROUP_DIM // LANES):
                out_ref[0, r, r0 // d:(r0 + rows) // d, s * LANES:(s + 1) * LANES] = (
                    slab_ref[slab0 + s, pl.ds(r, rows // d, stride=d), :].astype(BF16))
        slab0 += GROUP_DIM // LANES


N_INPROJ_INPUTS = 7
N_INPROJ_OUTPUTS = 14
N_SAMPLE_INPUTS = 8
N_SAMPLE_OUTPUTS = 3


def _inproj_prompt_kernel(*refs, tm, hosted):
    n_in = N_INPROJ_INPUTS + N_SAMPLE_INPUTS * len(hosted)
    x_ref, sh_ref, sc_ref, g1_ref, w_ref, cw_ref, gnc_ref = refs[:N_INPROJ_INPUTS]
    (convn_ref, q1_ref, q2_ref, q3_ref, k1_ref, k2_ref, k3_ref, v1_ref, v2_ref, v3_ref,
     nconv_ref, kv1_ref, kv2_ref, kv3_ref) = refs[n_in:n_in + N_INPROJ_OUTPUTS]
    ubuf_ref, slab_ref = refs[-2:]
    i = pl.program_id(1)
    n = pl.num_programs(1)
    pr = tm // INPROJ_PARTS
    parts = range(INPROJ_PARTS)
    c0 = 3 * CONV_DIM
    cw = cw_ref[...]

    hbs = [(_rms(x_ref[0, a * pr:(a + 1) * pr, :], g1_ref[...]) * (1.0 + sc_ref[0]) + sh_ref[0]).astype(BF16)
           for a in parts]
    projs = [jnp.dot(hb, w_ref[...], preferred_element_type=F32) for hb in hbs]

    for j, dims in enumerate(hosted):
        s_in = refs[N_INPROJ_INPUTS + j * N_SAMPLE_INPUTS:N_INPROJ_INPUTS + (j + 1) * N_SAMPLE_INPUTS]
        s_out = refs[n_in + N_INPROJ_OUTPUTS + j * N_SAMPLE_OUTPUTS:
                     n_in + N_INPROJ_OUTPUTS + (j + 1) * N_SAMPLE_OUTPUTS]
        _attn_sample_rows(pl.program_id(0) * n + i, *s_in, *s_out, *dims)

    tail = ubuf_ref[tm:tm + SUBLANES, :]
    ubuf_ref[0:SUBLANES, :] = jnp.where(i == 0, jnp.zeros_like(tail), tail)
    for a in parts:
        r0 = a * pr
        p = projs[a]
        gb = p[:, 0:CONV_DIM]
        u = p[:, CONV_DIM:2 * CONV_DIM] * p[:, 2 * CONV_DIM:c0]
        ubuf_ref[SUBLANES + r0:SUBLANES + r0 + pr, :] = u
        um1 = ubuf_ref[SUBLANES - 1 + r0:SUBLANES - 1 + r0 + pr, :]
        um2 = ubuf_ref[SUBLANES - 2 + r0:SUBLANES - 2 + r0 + pr, :]
        z = cw[0:1, :] * um2 + cw[1:2, :] * um1 + cw[2:3, :] * u
        convn_ref[0, r0:r0 + pr, :] = _rms(gb * z, gnc_ref[...]).astype(BF16)
        slab0 = a * 3 * STREAM_SLABS
        _store_streams(p[:, c0:c0 + ATTN_DIM] * ATTN_SCALE, (q1_ref, q2_ref, q3_ref), slab_ref, slab0, r0)
        _store_streams(p[:, c0 + ATTN_DIM:c0 + 2 * ATTN_DIM], (k1_ref, k2_ref, k3_ref), slab_ref,
                       slab0 + STREAM_SLABS, r0)
        _store_streams(p[:, c0 + 2 * ATTN_DIM:c0 + 3 * ATTN_DIM], (v1_ref, v2_ref, v3_ref), slab_ref,
                       slab0 + 2 * STREAM_SLABS, r0)
    nconv_ref[0] = ubuf_ref[tm + SUBLANES - 2:tm + SUBLANES, :]

    def put(ref, g, rows):
        for a in parts:
            lo_r, hi_r = max(tm - rows, a * pr), (a + 1) * pr
            if hi_r <= lo_r:
                continue
            for kv in range(2):
                lo = c0 + (1 + kv) * ATTN_DIM + g * GROUP_DIM
                ref[0, kv, :, lo_r - (tm - rows):hi_r - (tm - rows)] = (
                    projs[a][lo_r - a * pr:hi_r - a * pr, lo:lo + GROUP_DIM].T)

    w3 = DIL_PAIRS[2][0]

    @pl.when(i >= n - w3 // tm)
    def _():
        put(kv3_ref, 2, tm)

    @pl.when(i == n - 1)
    def _():
        put(kv2_ref, 1, DIL_PAIRS[1][0])
        put(kv1_ref, 0, DIL_PAIRS[0][0])


INPROJ_TILE = 1024


def _inproj_prompt(x, mod, g1, w_in, conv_w, gn_conv, samples=()):
    B, S, _ = x.shape
    tm = INPROJ_TILE
    n = S // tm
    w1, w2, w3 = (w for w, _ in DIL_PAIRS)
    assert S % tm == 0 and w3 % tm == 0 and tm >= w2 and tm % (INPROJ_PARTS * LANES) == 0
    n3 = w3 // tm
    row = lambda b, i: (b, i, 0)
    stream_specs = [pl.BlockSpec((1, d, tm // d, GROUP_DIM), lambda b, i: (b, 0, i, 0)) for _, d in DIL_PAIRS]
    stream_shapes = [jax.ShapeDtypeStruct((B, d, S // d, GROUP_DIM), BF16) for _, d in DIL_PAIRS]
    in_specs = [pl.BlockSpec((1, tm, D_MODEL), row),
                pl.BlockSpec((1, 1, D_MODEL), lambda b, i: (b, 0, 0)),
                pl.BlockSpec((1, 1, D_MODEL), lambda b, i: (b, 0, 1)),
                _const_spec((1, D_MODEL)),
                _const_spec((D_MODEL, PROJ_DIM)),
                _const_spec((3, CONV_DIM)),
                _const_spec((1, CONV_DIM))]
    out_specs = [pl.BlockSpec((1, tm, CONV_DIM), row)] + stream_specs * 3 + [
        pl.BlockSpec((1, 2, CONV_DIM), lambda b, i: (b, 0, 0)),
        pl.BlockSpec((1, 2, GROUP_DIM, w1), lambda b, i: (b, 0, 0, 0)),
        pl.BlockSpec((1, 2, GROUP_DIM, w2), lambda b, i: (b, 0, 0, 0)),
        pl.BlockSpec((1, 2, GROUP_DIM, tm), lambda b, i: (b, 0, 0, jnp.maximum(i - (n - n3), 0)))]
    out_shapes = [jax.ShapeDtypeStruct((B, S, CONV_DIM), BF16)] + stream_shapes * 3 + [
        jax.ShapeDtypeStruct((B, 2, CONV_DIM), F32),
        jax.ShapeDtypeStruct((B, 2, GROUP_DIM, w1), F32),
        jax.ShapeDtypeStruct((B, 2, GROUP_DIM, w2), F32),
        jax.ShapeDtypeStruct((B, 2, GROUP_DIM, w3), F32)]
    operands = [x, mod, mod, g1, w_in, conv_w, gn_conv]
    assert len(operands) == N_INPROJ_INPUTS and len(out_specs) == N_INPROJ_OUTPUTS
    hosted = []
    for q8, k8, v8, kt, vt, cache, bc, bn, g, t_len in samples:
        nbb = cache.shape[0] // (B * n)
        assert nbb * B * n == cache.shape[0]
        s_in, s_out, s_shapes = _attn_sample_specs(cache.shape, bc, bn, g, t_len, nbb, lambda b, i: b * n + i)
        in_specs += s_in
        operands += [q8, k8, v8, kt, vt, cache, bc, bn]
        out_specs += s_out
        out_shapes += s_shapes
        hosted.append((cache.shape[-1], t_len))
    return pl.pallas_call(
        functools.partial(_inproj_prompt_kernel, tm=tm, hosted=tuple(hosted)),
        grid=(B, n),
        in_specs=in_specs, out_specs=out_specs, out_shape=out_shapes,
        scratch_shapes=[pltpu.VMEM((tm + SUBLANES, CONV_DIM), F32),
                        pltpu.VMEM((INPROJ_PARTS * 3 * STREAM_SLABS, tm // INPROJ_PARTS, LANES), F32)],
        compiler_params=pltpu.CompilerParams(dimension_semantics=("arbitrary", "arbitrary"),
                                             vmem_limit_bytes=HOST_VMEM_LIMIT if hosted else VMEM_LIMIT),
        name="inproj_prompt",
    )(*operands)


def _inproj_sample_kernel(x_ref, sh_ref, sc_ref, g1_ref, w_ref, cw_ref, gnc_ref, p0_ref, p1_ref,
                          convn_ref, u_ref, q_ref, k_ref, v_ref, kt_ref, vt_ref, ubuf_ref, *, tm, t_len):
    h = _rms(x_ref[...], g1_ref[...]) * (1.0 + sc_ref[...]) + sh_ref[...]
    hb = h.astype(BF16)

    pc = jnp.dot(hb, w_ref[:, 0:3 * CONV_DIM], preferred_element_type=F32)
    gb = pc[:, 0:CONV_DIM]
    u = pc[:, CONV_DIM:2 * CONV_DIM] * pc[:, 2 * CONV_DIM:3 * CONV_DIM]
    u_ref[...] = u

    ubuf_ref[0:SUBLANES, :] = jnp.zeros((SUBLANES, CONV_DIM), F32)
    ubuf_ref[SUBLANES:tm + SUBLANES, :] = u
    t = lax.broadcasted_iota(jnp.int32, (tm, 1), 0) % t_len
    um1 = jnp.where(t == 0, p1_ref[...], ubuf_ref[SUBLANES - 1:tm + SUBLANES - 1, :])
    um2 = jnp.where(t == 0, p0_ref[...], jnp.where(t == 1, p1_ref[...], ubuf_ref[SUBLANES - 2:tm + SUBLANES - 2, :]))
    cw = cw_ref[...]
    z = cw[0:1, :] * um2 + cw[1:2, :] * um1 + cw[2:3, :] * u
    convn_ref[...] = _rms(gb * z, gnc_ref[...]).astype(BF16)

    c0 = 3 * CONV_DIM
    pq = jnp.dot(hb, w_ref[:, c0:c0 + ATTN_DIM], preferred_element_type=F32)
    q_ref[...] = pq * ATTN_SCALE
    pk = jnp.dot(hb, w_ref[:, c0 + ATTN_DIM:c0 + 2 * ATTN_DIM], preferred_element_type=F32)
    k_ref[...] = pk
    kt_ref[...] = pk.T
    pv = jnp.dot(hb, w_ref[:, c0 + 2 * ATTN_DIM:c0 + 3 * ATTN_DIM], preferred_element_type=F32)
    v_ref[...] = pv
    vt_ref[...] = pv.T


def _inproj_sample(x, mod, g1, w_in, conv_w, gn_conv, prev0, prev1, t_len):
    tm = x.shape[0]
    full = lambda shape: pl.BlockSpec(shape, lambda i: (0,) * len(shape))
    return pl.pallas_call(
        functools.partial(_inproj_sample_kernel, tm=tm, t_len=t_len),
        grid=(1,),
        in_specs=[full((tm, D_MODEL)),
                  pl.BlockSpec((tm, D_MODEL), lambda i: (0, 0)),
                  pl.BlockSpec((tm, D_MODEL), lambda i: (0, 1)),
                  full((1, D_MODEL)), full((D_MODEL, PROJ_DIM)), full((3, CONV_DIM)), full((1, CONV_DIM)),
                  full((tm, CONV_DIM)), full((tm, CONV_DIM))],
        out_specs=[full((tm, CONV_DIM)), full((tm, CONV_DIM)),
                   full((tm, ATTN_DIM)), full((tm, ATTN_DIM)), full((tm, ATTN_DIM)),
                   full((ATTN_DIM, tm)), full((ATTN_DIM, tm))],
        out_shape=[jax.ShapeDtypeStruct((tm, CONV_DIM), BF16),
                   jax.ShapeDtypeStruct((tm, CONV_DIM), F32),
                   jax.ShapeDtypeStruct((tm, ATTN_DIM), F32),
                   jax.ShapeDtypeStruct((tm, ATTN_DIM), F32),
                   jax.ShapeDtypeStruct((tm, ATTN_DIM), F32),
                   jax.ShapeDtypeStruct((ATTN_DIM, tm), F32),
                   jax.ShapeDtypeStruct((ATTN_DIM, tm), F32)],
        scratch_shapes=[pltpu.VMEM((tm + SUBLANES, CONV_DIM), F32)],
        compiler_params=_cparams(("arbitrary",)),
        name="inproj_sample",
    )(x, mod, mod, g1, w_in, conv_w, gn_conv, prev0, prev1)


def _t5_buckets(dist):
    dist = np.asarray(dist, np.int32)
    max_exact = N_BUCKETS // 2
    large = max_exact + (np.log(np.maximum(dist, 1).astype(np.float32) / max_exact)
                         / np.log(MAX_DISTANCE / max_exact) * (N_BUCKETS - max_exact)).astype(np.int32)
    large = np.minimum(large, N_BUCKETS - 1)
    return np.where(dist < max_exact, dist, large).astype(np.int32)


def _group_table(rel_bias, g):
    w, d = DIL_PAIRS[g]
    buckets = _t5_buckets(np.arange(w // d + 1) * d)
    return rel_bias[buckets][:, g * GROUP_HEADS:(g + 1) * GROUP_HEADS].T.astype(F32)


def _prompt_bias(table):
    period = 3 * QB - 1
    base = jnp.concatenate([table[:, ::-1], jnp.full((GROUP_HEADS, period - QB - 1), NEG_INF, F32)], axis=1)
    skew = jnp.tile(base, (1, QB))[:, :QB * (period - 1)].reshape(GROUP_HEADS, QB, period - 1)
    prev, cur = skew[:, :, 0:QB], skew[:, :, QB:2 * QB]
    tri = (np.arange(QB)[None, :] <= np.arange(QB)[:, None])[None]
    fold = jnp.stack([jnp.where(tri, cur, prev), jnp.where(tri, cur, NEG_INF)])
    fold = fold.transpose(0, 3, 1, 2).reshape(2, QB, GROUP_HEADS * QB)
    lone = jnp.broadcast_to(table[:, QB:QB + 1], (GROUP_HEADS, QB)).reshape(1, 1, GROUP_HEADS * QB)
    return fold, jnp.concatenate([lone, jnp.full_like(lone, NEG_INF)], axis=0)


def _sample_bias(table, buf_len, d, t_len):
    assert buf_len == QB * d
    asc = table[:, ::-1]
    inter = jnp.concatenate([asc[:, :, None], jnp.full((GROUP_HEADS, QB + 1, d - 1), NEG_INF, F32)],
                            axis=2).reshape(GROUP_HEADS, (QB + 1) * d)
    ext_len = buf_len + LANES
    rows = []
    for t in range(SUBLANES):
        lead = t if t < t_len else 0
        rows.append(jnp.pad(inter, ((0, 0), (lead, ext_len - lead - (QB + 1) * d)), constant_values=NEG_INF))
    ext = jnp.stack(rows, axis=1)
    return (ext[:, :, :buf_len].reshape(GROUP_HEADS * SUBLANES, buf_len),
            ext[:, :, buf_len:].reshape(GROUP_HEADS * SUBLANES, LANES))


def _attn_prompt_kernel(q_ref, kc_ref, vc_ref, kp_ref, vp_ref, bias_ref, dbias_ref, o_ref, lse_ref):
    i = pl.program_id(1)
    sb, tq, _ = q_ref.shape
    hm_b = _head_masks(GROUP_DIM, BF16)
    first = jnp.where(i == 0, 1, 0)
    c_idx = lax.broadcasted_iota(jnp.int32, (QB, GROUP_HEADS * QB), 0)
    q_idx = lax.broadcasted_iota(jnp.int32, (QB, GROUP_HEADS * QB), 1) & (QB - 1)
    tri = c_idx <= q_idx
    tri_b = tri.astype(F32).astype(BF16)
    diag_f = (c_idx == q_idx).astype(F32)
    diag_b = diag_f.astype(BF16)
    work = [(s, j) for s in range(sb) for j in range(tq // QB)]

    def scores(s, j):
        q = q_ref[s, j * QB:(j + 1) * QB, :]
        q4 = jnp.concatenate([q * hm_b[h] for h in range(GROUP_HEADS)], axis=0)
        if j == 0:
            kk = jnp.concatenate([kp_ref[s], kc_ref[s, 0:QB, :]], axis=0)
        else:
            kk = kc_ref[s, (j - 1) * QB:(j + 1) * QB, :]
        return lax.dot_general(kk, q4, (((1,), (1,)), ((), ())), preferred_element_type=F32)

    def values_t(s, j):
        if j == 0:
            return jnp.concatenate([vp_ref[s].T, vc_ref[s, 0:QB, :].T], axis=1)
        return vc_ref[s, (j - 1) * QB:(j + 1) * QB, :].T

    st_next = scores(*work[0])
    for n, (s, j) in enumerate(work):
        st = st_next
        if n + 1 < len(work):
            st_next = scores(*work[n + 1])
        slot = first if j == 0 else 0
        sp = st[0:QB]
        sc = st[QB:2 * QB]
        w = jnp.where(tri, sc, sp) + bias_ref[slot]
        sd = jnp.sum(sp * diag_f, axis=0, keepdims=True) + dbias_ref[slot]
        m = jnp.maximum(jnp.max(w, axis=0, keepdims=True), sd)
        e = jnp.exp(w - m)
        ed = jnp.exp(sd - m)
        den = jnp.sum(e, axis=0, keepdims=True) + ed
        lse = m + jnp.log(den)
        rden = 1.0 / den
        e_b = e.astype(BF16)
        p_cur = e_b * tri_b
        p_prev = (e_b - p_cur) + jnp.broadcast_to(ed.astype(BF16), e_b.shape) * diag_b
        pt = jnp.concatenate([p_prev, p_cur], axis=0)
        vt = values_t(s, j)
        ots = []
        for h in range(GROUP_HEADS):
            cols = slice(h * QB, (h + 1) * QB)
            rows = slice(h * HEAD_DIM, (h + 1) * HEAD_DIM)
            ot = jnp.dot(vt[rows, :], pt[:, cols], preferred_element_type=F32)
            ots.append(ot * rden[:, cols])
        o_ref[s, j * QB:(j + 1) * QB, :] = jnp.concatenate(ots, axis=0).T.astype(o_ref.dtype)
        lrows = jnp.concatenate([lse[:, h * QB:(h + 1) * QB] for h in range(GROUP_HEADS)]
                                + [jnp.zeros((LANES - GROUP_HEADS, QB), F32)], axis=0)
        lse_ref[s, j * QB:(j + 1) * QB, :] = lrows.T


def _attn_prompt(q, k, v, bias, dbias, block_rows=8192):
    ns, M, _ = q.shape
    tq = min(block_rows, M)
    sb = block_rows // tq
    r = tq // QB
    cur = pl.BlockSpec((sb, tq, GROUP_DIM), lambda s, i: (s, i, 0))
    prev = pl.BlockSpec((sb, QB, GROUP_DIM), lambda s, i: (s, jnp.maximum(i * r - 1, 0), 0))
    return pl.pallas_call(
        _attn_prompt_kernel,
        grid=(ns // sb, M // tq),
        in_specs=[cur, cur, cur, prev, prev, _const_spec(bias.shape), _const_spec(dbias.shape)],
        out_specs=[pl.BlockSpec((sb, tq, GROUP_DIM), lambda s, i: (s, i, 0)),
                   pl.BlockSpec((sb, tq, LANES), lambda s, i: (s, i, 0))],
        out_shape=[jax.ShapeDtypeStruct((ns, M, GROUP_DIM), BF16),
                   jax.ShapeDtypeStruct((ns, M, LANES), F32)],
        compiler_params=_cparams(("arbitrary", "arbitrary")),
        name="attn_prompt",
    )(q, k, v, k, v, bias, dbias)


def _attn_sample_kernel(q_ref, k8_ref, v8_ref, kt_ref, vt_ref, cache_ref, bc_ref, bn_ref,
                        o_ref, lse_ref, newc_ref, *, buf_len, t_len):
    _attn_sample_rows(pl.program_id(0), q_ref, k8_ref, v8_ref, kt_ref, vt_ref, cache_ref, bc_ref, bn_ref,
                      o_ref, lse_ref, newc_ref, buf_len, t_len)


def _attn_sample_rows(step, q_ref, k8_ref, v8_ref, kt_ref, vt_ref, cache_ref, bc_ref, bn_ref,
                      o_ref, lse_ref, newc_ref, buf_len, t_len):
    rows = range(q_ref.shape[0])
    hm_f = _head_masks(GROUP_DIM, F32)
    lm_f = _head_masks(LANES, F32)
    lane = lax.broadcasted_iota(jnp.int32, (1, LANES), 1)

    q4bs = [jnp.concatenate([q_ref[i] * hm_f[h] for h in range(GROUP_HEADS)], axis=0).astype(BF16)
            for i in rows]
    s_cs = [jnp.dot(q4bs[i], cache_ref[i, 0].astype(BF16), preferred_element_type=F32) + bc_ref[...]
            for i in rows]

    stats = []
    for i in rows:
        q4r = q4bs[i].astype(F32)
        k8r = k8_ref[i].astype(BF16).astype(F32)
        s_n = bn_ref[...]
        for t in range(t_len):
            col = jnp.sum(q4r * k8r[t:t + 1, :], axis=-1, keepdims=True)
            s_n = s_n + jnp.where(lane == t, col, 0.0)
        m = jnp.maximum(jnp.max(s_cs[i], axis=-1, keepdims=True), jnp.max(s_n, axis=-1, keepdims=True))
        e_c = jnp.exp(s_cs[i] - m)
        e_n = jnp.exp(s_n - m)
        den = jnp.sum(e_c, axis=-1, keepdims=True) + jnp.sum(e_n, axis=-1, keepdims=True)
        stats.append((m, e_c, e_n, den))

    pvs = [lax.dot_general(stats[i][1].astype(BF16), cache_ref[i, 1].astype(BF16), (((1,), (1,)), ((), ())),
                           preferred_element_type=F32) for i in rows]
    for i in rows:
        m, _, e_n, den = stats[i]
        pv = pvs[i]
        v8r = v8_ref[i].astype(BF16).astype(F32)
        e_nr = e_n.astype(BF16).astype(F32)
        for t in range(t_len):
            w_t = jnp.sum(jnp.where(lane == t, e_nr, 0.0), axis=-1, keepdims=True)
            pv = pv + w_t * v8r[t:t + 1, :]
        pv = pv / den
        lse = m + jnp.log(den)
        o = jnp.zeros((SUBLANES, GROUP_DIM), F32)
        l = jnp.zeros((SUBLANES, LANES), F32)
        for h in range(GROUP_HEADS):
            o = o + pv[h * SUBLANES:(h + 1) * SUBLANES, :] * hm_f[h]
            l = l + lse[h * SUBLANES:(h + 1) * SUBLANES, :] * lm_f[h]
        o_ref[i] = o
        lse_ref[i] = l

    keep = LANES - t_len
    per_block = LANES // t_len
    for i in rows:
        shift_new = keep - ((step * len(rows) + i) % per_block) * t_len
        for kv, new_ref in ((0, kt_ref), (1, vt_ref)):
            new_r = pltpu.roll(new_ref[...], shift_new, 1)
            prev_r = None
            for c in range(buf_len // LANES):
                r_c = pltpu.roll(cache_ref[i, kv, :, c * LANES:(c + 1) * LANES], keep, 1)
                if c > 0:
                    newc_ref[i, kv, :, (c - 1) * LANES:c * LANES] = jnp.where(lane < keep, prev_r, r_c)
                prev_r = r_c
            newc_ref[i, kv, :, buf_len - LANES:buf_len] = jnp.where(lane < keep, prev_r, new_r)


SAMPLE_STEP_POSITIONS = 2048
SAMPLE_STEP_ROWS = 8


def _attn_sample_specs(cache_shape, bc, bn, g, t_len, nbb, step_of):
    nb, _, _, buf_len = cache_shape
    per_block = LANES // t_len
    assert nb % nbb == 0 and per_block % nbb == 0
    new_spec = pl.BlockSpec((nbb, SUBLANES,GROUP_DIM), lambda *i: (step_of(*i), 0, g))
    newt_spec = pl.BlockSpec((GROUP_DIM, LANES), lambda *i: (g, (step_of(*i) * nbb) // per_block))
    cache_spec = pl.BlockSpec((nbb, 2, GROUP_DIM, buf_len), lambda *i: (step_of(*i), 0, 0, 0))
    in_specs = [new_spec, new_spec, new_spec, newt_spec, newt_spec, cache_spec,
                _const_spec(bc.shape), _const_spec(bn.shape)]
    out_specs = [pl.BlockSpec((nbb, SUBLANES,GROUP_DIM), lambda *i: (step_of(*i), 0, 0)),
                 pl.BlockSpec((nbb, SUBLANES,LANES), lambda *i: (step_of(*i), 0, 0)),
                 cache_spec]
    out_shapes = [jax.ShapeDtypeStruct((nb, SUBLANES,GROUP_DIM), F32),
                  jax.ShapeDtypeStruct((nb, SUBLANES,LANES), F32),
                  jax.ShapeDtypeStruct(cache_shape, F32)]
    return in_specs, out_specs, out_shapes


def _attn_sample(q8, k8, v8, kt, vt, cache, bc, bn, g, t_len):
    nb, _, _, buf_len = cache.shape
    nbb = max(1, min(LANES // t_len, SAMPLE_STEP_ROWS, SAMPLE_STEP_POSITIONS // buf_len))
    in_specs, out_specs, out_shapes = _attn_sample_specs(cache.shape, bc, bn, g, t_len, nbb, lambda b: b)
    return pl.pallas_call(
        functools.partial(_attn_sample_kernel, buf_len=buf_len, t_len=t_len),
        grid=(nb // nbb,),
        in_specs=in_specs, out_specs=out_specs, out_shape=out_shapes,
        compiler_params=_cparams(("arbitrary",)),
        name="attn_sample",
    )(q8, k8, v8, kt, vt, cache, bc, bn)


OUT_PARTS = 2
MERGE_SLABS = N_DIL * (GROUP_DIM + LANES) // LANES


def _load_positions(ref, slab_ref, slab0, r0, rows):
    _, d, _, width = ref.shape
    if d == 1:
        return ref[0, 0, r0:r0 + rows, :].astype(F32)
    for r in range(d):
        for s in range(width // LANES):
            slab_ref[slab0 + s, pl.ds(r, rows // d, stride=d), :] = (
                ref[0, r, r0 // d:(r0 + rows) // d, s * LANES:(s + 1) * LANES].astype(F32))
    return jnp.concatenate([slab_ref[slab0 + s] for s in range(width // LANES)], axis=-1)


N_OUT_INPUTS = 19


def _out_kernel(*refs, hosted):
    (x_ref, ga1_ref, sh2_ref, sc2_ref, ga2_ref, convn_ref, o1_ref, o2_ref, o3_ref, l1_ref, l2_ref, l3_ref,
     gna_ref, wout_ref, g2_ref, wg_ref, wu_ref, wd_ref, fg_ref) = refs[:N_OUT_INPUTS]
    if hosted is None:
        y_ref, slab_ref = refs[N_OUT_INPUTS:]
    else:
        sample_in = refs[N_OUT_INPUTS:N_OUT_INPUTS + N_SAMPLE_INPUTS]
        y_ref, *sample_out, slab_ref = refs[N_OUT_INPUTS + N_SAMPLE_INPUTS:]
    tm = x_ref.shape[1]
    pr = tm // OUT_PARTS
    parts = range(OUT_PARTS)
    hm_f = _head_masks(GROUP_DIM, F32)
    lm_f = _head_masks(LANES, F32)
    gna = gna_ref[...]

    def mod_rows(ref, a):
        return ref[0] if ref.shape[1] == 1 else ref[0, a * pr:(a + 1) * pr, :]

    def merge(a):
        r0 = a * pr
        slab0 = a * MERGE_SLABS
        ls = [_load_positions(l_ref, slab_ref, slab0 + g, r0, pr)
              for g, l_ref in enumerate((l1_ref, l2_ref, l3_ref))]
        m = jnp.maximum(jnp.maximum(ls[0], ls[1]), ls[2])
        es = [jnp.exp(l - m) for l in ls]
        tot = es[0] + es[1] + es[2]
        aos = []
        ssq = jnp.zeros((pr, 1), F32)
        for g, o_ref in enumerate((o1_ref, o2_ref, o3_ref)):
            alpha = es[g] / tot
            wide = jnp.zeros((pr, GROUP_DIM), F32)
            for h in range(GROUP_HEADS):
                a_h = jnp.sum(alpha * lm_f[h], axis=-1, keepdims=True)
                wide = wide + a_h * hm_f[h]
            ao = wide * _load_positions(o_ref, slab_ref, slab0 + N_DIL + g * (GROUP_DIM // LANES), r0, pr)
            ssq = ssq + jnp.sum(ao * ao, axis=-1, keepdims=True)
            aos.append(ao)
        rinv = lax.rsqrt(ssq / ATTN_DIM + EPS)
        return jnp.concatenate(
            [convn_ref[0, r0:r0 + pr, :]]
            + [(aos[g] * rinv * gna[:, g * GROUP_DIM:(g + 1) * GROUP_DIM]).astype(BF16) for g in range(N_DIL)],
            axis=-1)

    def out_proj(a, mixed):
        mix = jnp.dot(mixed, wout_ref[...], preferred_element_type=F32)
        x1 = x_ref[0, a * pr:(a + 1) * pr, :] + mod_rows(ga1_ref, a) * mix
        h2 = (_rms(x1, g2_ref[...]) * (1.0 + mod_rows(sc2_ref, a)) + mod_rows(sh2_ref, a)).astype(BF16)
        return x1, h2

    def swiglu(h2):
        gate = jnp.dot(h2, wg_ref[...], preferred_element_type=F32)
        up = jnp.dot(h2, wu_ref[...], preferred_element_type=F32)
        act = (gate * jax.nn.sigmoid(gate) * up).astype(BF16)
        return jnp.dot(act, wd_ref[...], preferred_element_type=F32)

    if hosted is not None:
        step = pl.program_id(0) * pl.num_programs(1) + pl.program_id(1)
        _attn_sample_rows(step, *sample_in, *sample_out, *hosted)
    mixed = [merge(a) for a in parts]
    x1h2 = [out_proj(a, mixed[a]) for a in parts]
    ffn = [swiglu(h2) for _, h2 in x1h2]
    for a in parts:
        x2 = x1h2[a][0] + mod_rows(ga2_ref, a) * ffn[a]
        y_ref[0, a * pr:(a + 1) * pr, :] = _rms(x2, fg_ref[...])


OUT_TILE = 512


def _out(x, mod, convn, os_, ls_, gn_attn, w_out, g2, w_gate, w_up, w_down, final_g, sample=None):
    nb, R, _ = x.shape
    tm = min(OUT_TILE, R)
    n_i = R // tm
    per_row = mod.shape[1] != 1
    mrows = tm if per_row else 1
    row = lambda b, i: (b, i, 0)

    def mod_spec(c):
        return pl.BlockSpec((1, mrows, D_MODEL), lambda b, i: (b, i if per_row else 0, c))

    def stream_spec(a):
        _, d, _, width = a.shape
        return pl.BlockSpec((1, d, tm // d, width), lambda b, i: (b, 0, i, 0))

    in_specs = ([pl.BlockSpec((1, tm, D_MODEL), row),
                 mod_spec(2), mod_spec(3), mod_spec(4), mod_spec(5),
                 pl.BlockSpec((1, tm, CONV_DIM), row)]
                + [stream_spec(a) for a in os_] + [stream_spec(a) for a in ls_]
                + [_const_spec((1, ATTN_DIM)), _const_spec((D_MODEL, D_MODEL)), _const_spec((1, D_MODEL)),
                   _const_spec((D_MODEL, D_FF)), _const_spec((D_MODEL, D_FF)), _const_spec((D_FF, D_MODEL)),
                   _const_spec((1, D_MODEL))])
    operands = [x, mod, mod, mod, mod, convn, *os_, *ls_, gn_attn, w_out, g2, w_gate, w_up, w_down, final_g]
    assert len(operands) == N_OUT_INPUTS
    out_specs = [pl.BlockSpec((1, tm, D_MODEL), row)]
    out_shapes = [jax.ShapeDtypeStruct(x.shape, F32)]
    hosted, limit = None, VMEM_LIMIT
    if sample is not None:
        q8, k8, v8, kt, vt, cache, bc, bn, g, t_len = sample
        assert cache.shape[0] == nb * n_i
        s_in, s_out, s_shapes = _attn_sample_specs(cache.shape, bc, bn, g, t_len, 1, lambda b, i: b * n_i + i)
        in_specs += s_in
        operands += [q8, k8, v8, kt, vt, cache, bc, bn]
        out_specs += s_out
        out_shapes += s_shapes
        hosted, limit = (cache.shape[-1], t_len), HOST_VMEM_LIMIT
    outs = pl.pallas_call(
        functools.partial(_out_kernel, hosted=hosted),
        grid=(nb, n_i),
        in_specs=in_specs, out_specs=out_specs, out_shape=out_shapes,
        scratch_shapes=[pltpu.VMEM((OUT_PARTS * MERGE_SLABS, tm // OUT_PARTS, LANES), F32)],
        compiler_params=pltpu.CompilerParams(dimension_semantics=("arbitrary", "arbitrary"),
                                             vmem_limit_bytes=limit),
        name="out",
    )(*operands)
    return outs[0] if sample is None else outs


def _buffer_to_channel_major(cache):
    B, L = cache.shape[:2]
    return cache.transpose(0, 2, 3, 4, 1).reshape(B, 2, GROUP_DIM, L)


def _buffer_from_channel_major(buf):
    B, _, _, L = buf.shape
    return buf.reshape(B, 2, GROUP_HEADS, HEAD_DIM, L).transpose(0, 4, 1, 2, 3)[None]


def kernel(x_prompt, x_sample, state_conv, cache_kv1, cache_kv2, cache_kv3, c_prompt, c_sample, w_ada, b_ada, norm1_g, norm2_g, w_in, conv_w, gn_conv, gn_attn, w_out, w_gate, w_up, w_down, rel_bias, final_g):
    B, S, _ = x_prompt.shape
    nb, t_len, _ = x_sample.shape
    rows_s = nb * t_len
    assert w_ada.shape[0] == 1 and LANES % t_len == 0 and 2 <= t_len <= SUBLANES and rows_s % LANES == 0

    w_in_b = w_in[0].astype(BF16)
    w_out_b = w_out[0].astype(BF16)
    w_gate_b = w_gate[0].astype(BF16)
    w_up_b = w_up[0].astype(BF16)
    w_down_b = w_down[0].astype(BF16)
    final_g2 = final_g.reshape(1, D_MODEL)
    tables = [_group_table(rel_bias, g) for g in range(N_DIL)]

    c_all = jnp.concatenate([c_prompt, jnp.repeat(c_sample, t_len, axis=0)], axis=0)
    mod_p, mod_s = _ada(c_all, w_ada[0], b_ada, B)
    mod_p = mod_p.reshape(B, 1, 6 * D_MODEL)

    xs = x_sample.reshape(rows_s, D_MODEL)
    prev0 = jnp.repeat(state_conv[0, :, 0], t_len, axis=0)
    prev1 = jnp.repeat(state_conv[0, :, 1], t_len, axis=0)
    convn_s, u_s, q_s, k_s, v_s, kt_s, vt_s = _inproj_sample(
        xs, mod_s, norm1_g, w_in_b, conv_w[0], gn_conv, prev0, prev1, t_len)
    pad8 = lambda a: jnp.pad(a.reshape(nb, t_len, ATTN_DIM), ((0, 0), (0, SUBLANES - t_len), (0, 0)))
    q8, k8, v8 = pad8(q_s), pad8(k_s), pad8(v_s)
    sample_ops = []
    for g, (cache, (_, d)) in enumerate(zip((cache_kv1, cache_kv2, cache_kv3), DIL_PAIRS)):
        buf = _buffer_to_channel_major(cache[0])
        bc, bn = _sample_bias(tables[g], buf.shape[-1], d, t_len)
        sample_ops.append((q8, k8, v8, kt_s, vt_s, buf, bc, bn, g, t_len))

    host_out = N_DIL - 1 if B * (S // min(OUT_TILE, S)) == nb else None
    inproj_steps = B * (S // INPROJ_TILE)
    host_in = [g for g in range(N_DIL) if g != host_out and nb % inproj_steps == 0
               and (LANES // t_len) % (nb // inproj_steps) == 0]
    sample_res = {}
    (convn, q1, q2, q3, k1, k2, k3, v1, v2, v3, nconv_p, kv1_p, kv2_p, kv3_p, *hosted_res) = _inproj_prompt(
        x_prompt, mod_p, norm1_g, w_in_b, conv_w[0], gn_conv, samples=[sample_ops[g] for g in host_in])
    for j, g in enumerate(host_in):
        sample_res[g] = hosted_res[j * N_SAMPLE_OUTPUTS:(j + 1) * N_SAMPLE_OUTPUTS]
    os_, ls_ = [], []
    for g, (qg, kg, vg) in enumerate(((q1, k1, v1), (q2, k2, v2), (q3, k3, v3))):
        _, d, M, _ = qg.shape
        merge = lambda a: a.reshape(B * d, M, GROUP_DIM)
        o, l = _attn_prompt(merge(qg), merge(kg), merge(vg), *_prompt_bias(tables[g]))
        os_.append(o.reshape(B, d, M, GROUP_DIM))
        ls_.append(l.reshape(B, d, M, LANES))
    outs = _out(x_prompt, mod_p, convn, os_, ls_, gn_attn, w_out_b, norm2_g, w_gate_b, w_up_b, w_down_b,
                final_g2, sample=None if host_out is None else sample_ops[host_out])
    y_prompt = outs
    if host_out is not None:
        y_prompt, *sample_res[host_out] = outs

    os_s, ls_s, new_bufs = [], [], []
    for g in range(N_DIL):
        o, l, nbuf = sample_res[g] if g in sample_res else _attn_sample(*sample_ops[g])
        os_s.append(o[:, :t_len].reshape(1, 1, rows_s, GROUP_DIM))
        ls_s.append(l[:, :t_len].reshape(1, 1, rows_s, LANES))
        new_bufs.append(_buffer_from_channel_major(nbuf))
    y_sample = _out(xs[None], mod_s[None], convn_s[None], os_s, ls_s, gn_attn, w_out_b, norm2_g,
                    w_gate_b, w_up_b, w_down_b, final_g2)
    y_sample = y_sample.reshape(nb, t_len, D_MODEL)
    nconv_s = u_s.reshape(nb, t_len, CONV_DIM)[:, t_len - 2:]

    return (y_prompt, y_sample, nconv_p[None],
            _buffer_from_channel_major(kv1_p), _buffer_from_channel_major(kv2_p),
            _buffer_from_channel_major(kv3_p),
            nconv_s[None], new_bufs[0], new_bufs[1], new_bufs[2])
```

```python
import functools

import numpy as np
import jax
import jax.numpy as jnp
from jax import lax
from jax.experimental import pallas as pl
from jax.experimental.pallas import tpu as pltpu

F32 = jnp.float32
BF16 = jnp.bfloat16

D_MODEL = 1024
HEAD_DIM = 64
CONV_DIM = 256
ATTN_DIM = 768
GROUP_HEADS = 4
GROUP_DIM = GROUP_HEADS * HEAD_DIM
DIL_PAIRS = ((128, 1), (512, 4), (2048, 16))
N_DIL = len(DIL_PAIRS)
QB = 128
N_BUCKETS = 32
MAX_DISTANCE = 2048
D_FF = 2816
PROJ_DIM = 3072
EPS = 1e-6
NEG_INF = -1e30
ATTN_SCALE = HEAD_DIM ** -0.5

LANES = 128
SUBLANES = 8
VMEM_LIMIT = 56 * 1024 * 1024
HOST_VMEM_LIMIT = 62 * 1024 * 1024


def _cparams(sem):
    return pltpu.CompilerParams(dimension_semantics=sem, vmem_limit_bytes=VMEM_LIMIT)


def _const_spec(shape):
    nd = len(shape)
    return pl.BlockSpec(shape, lambda *_: (0,) * nd, pipeline_mode=pl.Buffered(1))


def _head_masks(width, dtype):
    lane = lax.broadcasted_iota(jnp.int32, (1, width), 1)
    if width == GROUP_DIM:
        lane = lane // HEAD_DIM
    return [(lane == h).astype(dtype) for h in range(GROUP_HEADS)]


ADA_TILE = 768


def _ada_kernel(c_ref, w_ref, b_ref, op_ref, os_ref):
    c = c_ref[...]
    s = (c * jax.nn.sigmoid(c)).astype(BF16)
    mod = jnp.dot(s, w_ref[...].astype(BF16), preferred_element_type=F32) + b_ref[...]
    rows_p = op_ref.shape[0]
    op_ref[...] = mod[:rows_p]
    os_ref[...] = mod[rows_p:]


def _ada(c_all, w_ada, b_ada, rows_p):
    rows = c_all.shape[0]
    tn = ADA_TILE
    assert rows_p % SUBLANES == 0 and (6 * D_MODEL) % tn == 0
    return pl.pallas_call(
        _ada_kernel,
        grid=(6 * D_MODEL // tn,),
        in_specs=[pl.BlockSpec((rows, D_MODEL), lambda j: (0, 0)),
                  pl.BlockSpec((D_MODEL, tn), lambda j: (0, j)),
                  pl.BlockSpec((1, tn), lambda j: (0, j))],
        out_specs=[pl.BlockSpec((rows_p, tn), lambda j: (0, j)),
                   pl.BlockSpec((rows - rows_p, tn), lambda j: (0, j))],
        out_shape=[jax.ShapeDtypeStruct((rows_p, 6 * D_MODEL), F32),
                   jax.ShapeDtypeStruct((rows - rows_p, 6 * D_MODEL), F32)],
        compiler_params=_cparams(("arbitrary",)),
        name="ada",
    )(c_all, w_ada, b_ada)


def _rms(x, g):
    return x * lax.rsqrt(jnp.mean(x * x, axis=-1, keepdims=True) + EPS) * g


STREAM_SLABS = sum(GROUP_DIM // LANES for _, d in DIL_PAIRS if d > 1)
INPROJ_PARTS = 2


def _store_streams(p, out_refs, slab_ref, slab0, r0):
    rows = p.shape[0]
    for g, ((_, d), out_ref) in enumerate(zip(DIL_PAIRS, out_refs)):
        lo = g * GROUP_DIM
        if d == 1:
            out_ref[0, 0, r0:r0 + rows, :] = p[:, lo:lo + GROUP_DIM].astype(BF16)
            continue
        for s in range(GROUP_DIM // LANES):
            slab_ref[slab0 + s] = p[:, lo + s * LANES:lo + (s + 1) * LANES]
        for r in range(d):
            for s in range(GROUP_DIM // LANES):
                out_ref[0, r, r0 // d:(r0 + rows) // d, s * LANES:(s + 1) * LANES] = (
                    slab_ref[slab0 + s, pl.ds(r, rows // d, stride=d), :].astype(BF16))
        slab0 += GROUP_DIM // LANES


N_INPROJ_INPUTS = 7
N_INPROJ_OUTPUTS = 14
N_SAMPLE_INPUTS = 8
N_SAMPLE_OUTPUTS = 3


def _inproj_prompt_kernel(*refs, tm, hosted):
    n_in = N_INPROJ_INPUTS + N_SAMPLE_INPUTS * len(hosted)
    x_ref, sh_ref, sc_ref, g1_ref, w_ref, cw_ref, gnc_ref = refs[:N_INPROJ_INPUTS]
    (convn_ref, q1_ref, q2_ref, q3_ref, k1_ref, k2_ref, k3_ref, v1_ref, v2_ref, v3_ref,
     nconv_ref, kv1_ref, kv2_ref, kv3_ref) = refs[n_in:n_in + N_INPROJ_OUTPUTS]
    ubuf_ref, slab_ref = refs[-2:]
    i = pl.program_id(1)
    n = pl.num_programs(1)
    pr = tm // INPROJ_PARTS
    parts = range(INPROJ_PARTS)
    c0 = 3 * CONV_DIM
    cw = cw_ref[...]

    hbs = [(_rms(x_ref[0, a * pr:(a + 1) * pr, :], g1_ref[...]) * (1.0 + sc_ref[0]) + sh_ref[0]).astype(BF16)
           for a in parts]
    projs = [jnp.dot(hb, w_ref[...], preferred_element_type=F32) for hb in hbs]

    for j, dims in enumerate(hosted):
        s_in = refs[N_INPROJ_INPUTS + j * N_SAMPLE_INPUTS:N_INPROJ_INPUTS + (j + 1) * N_SAMPLE_INPUTS]
        s_out = refs[n_in + N_INPROJ_OUTPUTS + j * N_SAMPLE_OUTPUTS:
                     n_in + N_INPROJ_OUTPUTS + (j + 1) * N_SAMPLE_OUTPUTS]
        _attn_sample_rows(pl.program_id(0) * n + i, *s_in, *s_out, *dims)

    tail = ubuf_ref[tm:tm + SUBLANES, :]
    ubuf_ref[0:SUBLANES, :] = jnp.where(i == 0, jnp.zeros_like(tail), tail)
    for a in parts:
        r0 = a * pr
        p = projs[a]
        gb = p[:, 0:CONV_DIM]
        u = p[:, CONV_DIM:2 * CONV_DIM] * p[:, 2 * CONV_DIM:c0]
        ubuf_ref[SUBLANES + r0:SUBLANES + r0 + pr, :] = u
        um1 = ubuf_ref[SUBLANES - 1 + r0:SUBLANES - 1 + r0 + pr, :]
        um2 = ubuf_ref[SUBLANES - 2 + r0:SUBLANES - 2 + r0 + pr, :]
        z = cw[0:1, :] * um2 + cw[1:2, :] * um1 + cw[2:3, :] * u
        convn_ref[0, r0:r0 + pr, :] = _rms(gb * z, gnc_ref[...]).astype(BF16)
        slab0 = a * 3 * STREAM_SLABS
        _store_streams(p[:, c0:c0 + ATTN_DIM] * ATTN_SCALE, (q1_ref, q2_ref, q3_ref), slab_ref, slab0, r0)
        _store_streams(p[:, c0 + ATTN_DIM:c0 + 2 * ATTN_DIM], (k1_ref, k2_ref, k3_ref), slab_ref,
                       slab0 + STREAM_SLABS, r0)
        _store_streams(p[:, c0 + 2 * ATTN_DIM:c0 + 3 * ATTN_DIM], (v1_ref, v2_ref, v3_ref), slab_ref,
                       slab0 + 2 * STREAM_SLABS, r0)
    nconv_ref[0] = ubuf_ref[tm + SUBLANES - 2:tm + SUBLANES, :]

    def put(ref, g, rows):
        for a in parts:
            lo_r, hi_r = max(tm - rows, a * pr), (a + 1) * pr
            if hi_r <= lo_r:
                continue
            for kv in range(2):
                lo = c0 + (1 + kv) * ATTN_DIM + g * GROUP_DIM
                ref[0, kv, :, lo_r - (tm - rows):hi_r - (tm - rows)] = (
                    projs[a][lo_r - a * pr:hi_r - a * pr, lo:lo + GROUP_DIM].T)

    w3 = DIL_PAIRS[2][0]

    @pl.when(i >= n - w3 // tm)
    def _():
        put(kv3_ref, 2, tm)

    @pl.when(i == n - 1)
    def _():
        put(kv2_ref, 1, DIL_PAIRS[1][0])
        put(kv1_ref, 0, DIL_PAIRS[0][0])


INPROJ_TILE = 1024


def _inproj_prompt(x, mod, g1, w_in, conv_w, gn_conv, samples=()):
    B, S, _ = x.shape
    tm = INPROJ_TILE
    n = S // tm
    w1, w2, w3 = (w for w, _ in DIL_PAIRS)
    assert S % tm == 0 and w3 % tm == 0 and tm >= w2 and tm % (INPROJ_PARTS * LANES) == 0
    n3 = w3 // tm
    row = lambda b, i: (b, i, 0)
    stream_specs = [pl.BlockSpec((1, d, tm // d, GROUP_DIM), lambda b, i: (b, 0, i, 0)) for _, d in DIL_PAIRS]
    stream_shapes = [jax.ShapeDtypeStruct((B, d, S // d, GROUP_DIM), BF16) for _, d in DIL_PAIRS]
    in_specs = [pl.BlockSpec((1, tm, D_MODEL), row),
                pl.BlockSpec((1, 1, D_MODEL), lambda b, i: (b, 0, 0)),
                pl.BlockSpec((1, 1, D_MODEL), lambda b, i: (b, 0, 1)),
                _const_spec((1, D_MODEL)),
                _const_spec((D_MODEL, PROJ_DIM)),
                _const_spec((3, CONV_DIM)),
                _const_spec((1, CONV_DIM))]
    out_specs = [pl.BlockSpec((1, tm, CONV_DIM), row)] + stream_specs * 3 + [
        pl.BlockSpec((1, 2, CONV_DIM), lambda b, i: (b, 0, 0)),
        pl.BlockSpec((1, 2, GROUP_DIM, w1), lambda b, i: (b, 0, 0, 0)),
        pl.BlockSpec((1, 2, GROUP_DIM, w2), lambda b, i: (b, 0, 0, 0)),
        pl.BlockSpec((1, 2, GROUP_DIM, tm), lambda b, i: (b, 0, 0, jnp.maximum(i - (n - n3), 0)))]
    out_shapes = [jax.ShapeDtypeStruct((B, S, CONV_DIM), BF16)] + stream_shapes * 3 + [
        jax.ShapeDtypeStruct((B, 2, CONV_DIM), F32),
        jax.ShapeDtypeStruct((B, 2, GROUP_DIM, w1), F32),
        jax.ShapeDtypeStruct((B, 2, GROUP_DIM, w2), F32),
        jax.ShapeDtypeStruct((B, 2, GROUP_DIM, w3), F32)]
    operands = [x, mod, mod, g1, w_in, conv_w, gn_conv]
    assert len(operands) == N_INPROJ_INPUTS and len(out_specs) == N_INPROJ_OUTPUTS
    hosted = []
    for q8, k8, v8, kt, vt, cache, bc, bn, g, t_len in samples:
        nbb = cache.shape[0] // (B * n)
        assert nbb * B * n == cache.shape[0]
        s_in, s_out, s_shapes = _attn_sample_specs(cache.shape, bc, bn, g, t_len, nbb, lambda b, i: b * n + i)
        in_specs += s_in
        operands += [q8, k8, v8, kt, vt, cache, bc, bn]
        out_specs += s_out
        out_shapes += s_shapes
        hosted.append((cache.shape[-1], t_len))
    return pl.pallas_call(
        functools.partial(_inproj_prompt_kernel, tm=tm, hosted=tuple(hosted)),
        grid=(B, n),
        in_specs=in_specs, out_specs=out_specs, out_shape=out_shapes,
        scratch_shapes=[pltpu.VMEM((tm + SUBLANES, CONV_DIM), F32),
                        pltpu.VMEM((INPROJ_PARTS * 3 * STREAM_SLABS, tm // INPROJ_PARTS, LANES), F32)],
        compiler_params=pltpu.CompilerParams(dimension_semantics=("arbitrary", "arbitrary"),
                                             vmem_limit_bytes=HOST_VMEM_LIMIT if hosted else VMEM_LIMIT),
        name="inproj_prompt",
    )(*operands)


def _inproj_sample_kernel(x_ref, sh_ref, sc_ref, g1_ref, w_ref, cw_ref, gnc_ref, p0_ref, p1_ref,
                          convn_ref, u_ref, q_ref, k_ref, v_ref, kt_ref, vt_ref, ubuf_ref, *, tm, t_len):
    h = _rms(x_ref[...], g1_ref[...]) * (1.0 + sc_ref[...]) + sh_ref[...]
    hb = h.astype(BF16)

    pc = jnp.dot(hb, w_ref[:, 0:3 * CONV_DIM], preferred_element_type=F32)
    gb = pc[:, 0:CONV_DIM]
    u = pc[:, CONV_DIM:2 * CONV_DIM] * pc[:, 2 * CONV_DIM:3 * CONV_DIM]
    u_ref[...] = u

    ubuf_ref[0:SUBLANES, :] = jnp.zeros((SUBLANES, CONV_DIM), F32)
    ubuf_ref[SUBLANES:tm + SUBLANES, :] = u
    t = lax.broadcasted_iota(jnp.int32, (tm, 1), 0) % t_len
    um1 = jnp.where(t == 0, p1_ref[...], ubuf_ref[SUBLANES - 1:tm + SUBLANES - 1, :])
    um2 = jnp.where(t == 0, p0_ref[...], jnp.where(t == 1, p1_ref[...], ubuf_ref[SUBLANES - 2:tm + SUBLANES - 2, :]))
    cw = cw_ref[...]
    z = cw[0:1, :] * um2 + cw[1:2, :] * um1 + cw[2:3, :] * u
    convn_ref[...] = _rms(gb * z, gnc_ref[...]).astype(BF16)

    c0 = 3 * CONV_DIM
    pq = jnp.dot(hb, w_ref[:, c0:c0 + ATTN_DIM], preferred_element_type=F32)
    q_ref[...] = pq * ATTN_SCALE
    pk = jnp.dot(hb, w_ref[:, c0 + ATTN_DIM:c0 + 2 * ATTN_DIM], preferred_element_type=F32)
    k_ref[...] = pk
    kt_ref[...] = pk.T
    pv = jnp.dot(hb, w_ref[:, c0 + 2 * ATTN_DIM:c0 + 3 * ATTN_DIM], preferred_element_type=F32)
    v_ref[...] = pv
    vt_ref[...] = pv.T


def _inproj_sample(x, mod, g1, w_in, conv_w, gn_conv, prev0, prev1, t_len):
    tm = x.shape[0]
    full = lambda shape: pl.BlockSpec(shape, lambda i: (0,) * len(shape))
    return pl.pallas_call(
        functools.partial(_inproj_sample_kernel, tm=tm, t_len=t_len),
        grid=(1,),
        in_specs=[full((tm, D_MODEL)),
                  pl.BlockSpec((tm, D_MODEL), lambda i: (0, 0)),
                  pl.BlockSpec((tm, D_MODEL), lambda i: (0, 1)),
                  full((1, D_MODEL)), full((D_MODEL, PROJ_DIM)), full((3, CONV_DIM)), full((1, CONV_DIM)),
                  full((tm, CONV_DIM)), full((tm, CONV_DIM))],
        out_specs=[full((tm, CONV_DIM)), full((tm, CONV_DIM)),
                   full((tm, ATTN_DIM)), full((tm, ATTN_DIM)), full((tm, ATTN_DIM)),
                   full((ATTN_DIM, tm)), full((ATTN_DIM, tm))],
        out_shape=[jax.ShapeDtypeStruct((tm, CONV_DIM), BF16),
                   jax.ShapeDtypeStruct((tm, CONV_DIM), F32),
                   jax.ShapeDtypeStruct((tm, ATTN_DIM), F32),
                   jax.ShapeDtypeStruct((tm, ATTN_DIM), F32),
                   jax.ShapeDtypeStruct((tm, ATTN_DIM), F32),
                   jax.ShapeDtypeStruct((ATTN_DIM, tm), F32),
                   jax.ShapeDtypeStruct((ATTN_DIM, tm), F32)],
        scratch_shapes=[pltpu.VMEM((tm + SUBLANES, CONV_DIM), F32)],
        compiler_params=_cparams(("arbitrary",)),
        name="inproj_sample",
    )(x, mod, mod, g1, w_in, conv_w, gn_conv, prev0, prev1)


def _t5_buckets(dist):
    dist = np.asarray(dist, np.int32)
    max_exact = N_BUCKETS // 2
    large = max_exact + (np.log(np.maximum(dist, 1).astype(np.float32) / max_exact)
                         / np.log(MAX_DISTANCE / max_exact) * (N_BUCKETS - max_exact)).astype(np.int32)
    large = np.minimum(large, N_BUCKETS - 1)
    return np.where(dist < max_exact, dist, large).astype(np.int32)


def _group_table(rel_bias, g):
    w, d = DIL_PAIRS[g]
    buckets = _t5_buckets(np.arange(w // d + 1) * d)
    return rel_bias[buckets][:, g * GROUP_HEADS:(g + 1) * GROUP_HEADS].T.astype(F32)


def _prompt_bias(table):
    period = 3 * QB - 1
    base = jnp.concatenate([table[:, ::-1], jnp.full((GROUP_HEADS, period - QB - 1), NEG_INF, F32)], axis=1)
    skew = jnp.tile(base, (1, QB))[:, :QB * (period - 1)].reshape(GROUP_HEADS, QB, period - 1)
    prev, cur = skew[:, :, 0:QB], skew[:, :, QB:2 * QB]
    tri = (np.arange(QB)[None, :] <= np.arange(QB)[:, None])[None]
    fold = jnp.stack([jnp.where(tri, cur, prev), jnp.where(tri, cur, NEG_INF)])
    fold = fold.transpose(0, 3, 1, 2).reshape(2, QB, GROUP_HEADS * QB)
    lone = jnp.broadcast_to(table[:, QB:QB + 1], (GROUP_HEADS, QB)).reshape(1, 1, GROUP_HEADS * QB)
    return fold, jnp.concatenate([lone, jnp.full_like(lone, NEG_INF)], axis=0)


def _sample_bias(table, buf_len, d, t_len):
    assert buf_len == QB * d
    asc = table[:, ::-1]
    inter = jnp.concatenate([asc[:, :, None], jnp.full((GROUP_HEADS, QB + 1, d - 1), NEG_INF, F32)],
                            axis=2).reshape(GROUP_HEADS, (QB + 1) * d)
    ext_len = buf_len + LANES
    rows = []
    for t in range(SUBLANES):
        lead = t if t < t_len else 0
        rows.append(jnp.pad(inter, ((0, 0), (lead, ext_len - lead - (QB + 1) * d)), constant_values=NEG_INF))
    ext = jnp.stack(rows, axis=1)
    return (ext[:, :, :buf_len].reshape(GROUP_HEADS * SUBLANES, buf_len),
            ext[:, :, buf_len:].reshape(GROUP_HEADS * SUBLANES, LANES))


def _attn_prompt_kernel(q_ref, kc_ref, vc_ref, kp_ref, vp_ref, bias_ref, dbias_ref, o_ref, lse_ref):
    i = pl.program_id(1)
    sb, tq, _ = q_ref.shape
    ch_head = lax.broadcasted_iota(jnp.int32, (GROUP_DIM, QB), 0) // HEAD_DIM
    rm_b = [(ch_head == h).astype(F32).astype(BF16) for h in range(GROUP_HEADS)]
    first = jnp.where(i == 0, 1, 0)
    c_idx = lax.broadcasted_iota(jnp.int32, (QB, QB), 0)
    q_idx = lax.broadcasted_iota(jnp.int32, (QB, QB), 1)
    tri = c_idx <= q_idx
    tri_b = tri.astype(F32).astype(BF16)
    diag_f = (c_idx == q_idx).astype(F32)
    diag_b = diag_f.astype(BF16)
    work = [(s, j) for s in range(sb) for j in range(tq // QB)]

    def scores(s, j):
        qt = q_ref[s, j * QB:(j + 1) * QB, :].T
        q4t = jnp.concatenate([qt * rm_b[h] for h in range(GROUP_HEADS)], axis=1)
        if j == 0:
            kk = jnp.concatenate([kp_ref[s], kc_ref[s, 0:QB, :]], axis=0)
        else:
            kk = kc_ref[s, (j - 1) * QB:(j + 1) * QB, :]
        return jnp.dot(kk, q4t, preferred_element_type=F32)

    def values_t(s, j):
        if j == 0:
            return jnp.concatenate([vp_ref[s].T, vc_ref[s, 0:QB, :].T], axis=1)
        return vc_ref[s, (j - 1) * QB:(j + 1) * QB, :].T

    st_next = scores(*work[0])
    for n, (s, j) in enumerate(work):
        st = st_next
        if n + 1 < len(work):
            st_next = scores(*work[n + 1])
        slot = first if j == 0 else 0
        bias, dbias = bias_ref[slot], dbias_ref[slot]
        vt = values_t(s, j)
        ots, lses = [], []
        for h in range(GROUP_HEADS):
            cols = slice(h * QB, (h + 1) * QB)
            sp = st[0:QB, cols]
            sc = st[QB:2 * QB, cols]
            w = jnp.where(tri, sc, sp) + bias[:, cols]
            sd = jnp.sum(sp * diag_f, axis=0, keepdims=True) + dbias[:, cols]
            m = jnp.maximum(jnp.max(w, axis=0, keepdims=True), sd)
            e = jnp.exp(w - m)
            ed = jnp.exp(sd - m)
            den = jnp.sum(e, axis=0, keepdims=True) + ed
            lses.append(m + jnp.log(den))
            e_b = e.astype(BF16)
            p_cur = e_b * tri_b
            p_prev = (e_b - p_cur) + jnp.broadcast_to(ed.astype(BF16), e_b.shape) * diag_b
            pt = jnp.concatenate([p_prev, p_cur], axis=0)
            ot = jnp.dot(vt[h * HEAD_DIM:(h + 1) * HEAD_DIM, :], pt, preferred_element_type=F32)
            ots.append(ot * (1.0 / den))
        o_ref[s, j * QB:(j + 1) * QB, :] = jnp.concatenate(ots, axis=0).T.astype(o_ref.dtype)
        lrows = jnp.concatenate(lses + [jnp.zeros((LANES - GROUP_HEADS, QB), F32)], axis=0)
        lse_ref[s, j * QB:(j + 1) * QB, :] = lrows.T


def _attn_prompt(q, k, v, bias, dbias, block_rows=4096):
    ns, M, _ = q.shape
    tq = min(block_rows, M)
    sb = block_rows // tq
    assert ns % sb == 0 and M % tq == 0 and tq % QB == 0
    r = tq // QB
    cur = pl.BlockSpec((sb, tq, GROUP_DIM), lambda s, i: (s, i, 0))
    prev = pl.BlockSpec((sb, QB, GROUP_DIM), lambda s, i: (s, jnp.maximum(i * r - 1, 0), 0))
    return pl.pallas_call(
        _attn_prompt_kernel,
        grid=(ns // sb, M // tq),
        in_specs=[cur, cur, cur, prev, prev, _const_spec(bias.shape), _const_spec(dbias.shape)],
        out_specs=[pl.BlockSpec((sb, tq, GROUP_DIM), lambda s, i: (s, i, 0)),
                   pl.BlockSpec((sb, tq, LANES), lambda s, i: (s, i, 0))],
        out_shape=[jax.ShapeDtypeStruct((ns, M, GROUP_DIM), BF16),
                   jax.ShapeDtypeStruct((ns, M, LANES), F32)],
        compiler_params=_cparams(("arbitrary", "arbitrary")),
        name="attn_prompt",
    )(q, k, v, k, v, bias, dbias)


def _attn_sample_kernel(q_ref, k8_ref, v8_ref, kt_ref, vt_ref, cache_ref, bc_ref, bn_ref,
                        o_ref, lse_ref, newc_ref, *, buf_len, t_len):
    _attn_sample_rows(pl.program_id(0), q_ref, k8_ref, v8_ref, kt_ref, vt_ref, cache_ref, bc_ref, bn_ref,
                      o_ref, lse_ref, newc_ref, buf_len, t_len)


def _attn_sample_rows(step, q_ref, k8_ref, v8_ref, kt_ref, vt_ref, cache_ref, bc_ref, bn_ref,
                      o_ref, lse_ref, newc_ref, buf_len, t_len):
    rows = range(q_ref.shape[0])
    hm_f = _head_masks(GROUP_DIM, F32)
    lm_f = _head_masks(LANES, F32)
    lane = lax.broadcasted_iota(jnp.int32, (1, LANES), 1)

    q4bs = [jnp.concatenate([q_ref[i] * hm_f[h] for h in range(GROUP_HEADS)], axis=0).astype(BF16)
            for i in rows]
    s_cs = [jnp.dot(q4bs[i], cache_ref[i, 0].astype(BF16), preferred_element_type=F32) + bc_ref[...]
            for i in rows]

    stats = []
    for i in rows:
        q4r = q4bs[i].astype(F32)
        k8r = k8_ref[i].astype(BF16).astype(F32)
        s_n = bn_ref[...]
        for t in range(t_len):
            col = jnp.sum(q4r * k8r[t:t + 1, :], axis=-1, keepdims=True)
            s_n = s_n + jnp.where(lane == t, col, 0.0)
        m = jnp.maximum(jnp.max(s_cs[i], axis=-1, keepdims=True), jnp.max(s_n, axis=-1, keepdims=True))
        e_c = jnp.exp(s_cs[i] - m)
        e_n = jnp.exp(s_n - m)
        den = jnp.sum(e_c, axis=-1, keepdims=True) + jnp.sum(e_n, axis=-1, keepdims=True)
        stats.append((m, e_c, e_n, den))

    pvs = [lax.dot_general(stats[i][1].astype(BF16), cache_ref[i, 1].astype(BF16), (((1,), (1,)), ((), ())),
                           preferred_element_type=F32) for i in rows]
    for i in rows:
        m, _, e_n, den = stats[i]
        pv = pvs[i]
        v8r = v8_ref[i].astype(BF16).astype(F32)
        e_nr = e_n.astype(BF16).astype(F32)
        for t in range(t_len):
            w_t = jnp.sum(jnp.where(lane == t, e_nr, 0.0), axis=-1, keepdims=True)
            pv = pv + w_t * v8r[t:t + 1, :]
        pv = pv / den
        lse = m + jnp.log(den)
        o = jnp.zeros((SUBLANES, GROUP_DIM), F32)
        l = jnp.zeros((SUBLANES, LANES), F32)
        for h in range(GROUP_HEADS):
            o = o + pv[h * SUBLANES:(h + 1) * SUBLANES, :] * hm_f[h]
            l = l + lse[h * SUBLANES:(h + 1) * SUBLANES, :] * lm_f[h]
        o_ref[i] = o
        lse_ref[i] = l

    keep = LANES - t_len
    per_block = LANES // t_len
    for i in rows:
        shift_new = keep - ((step * len(rows) + i) % per_block) * t_len
        for kv, new_ref in ((0, kt_ref), (1, vt_ref)):
            new_r = pltpu.roll(new_ref[...], shift_new, 1)
            prev_r = None
            for c in range(buf_len // LANES):
                r_c = pltpu.roll(cache_ref[i, kv, :, c * LANES:(c + 1) * LANES], keep, 1)
                if c > 0:
                    newc_ref[i, kv, :, (c - 1) * LANES:c * LANES] = jnp.where(lane < keep, prev_r, r_c)
                prev_r = r_c
            newc_ref[i, kv, :, buf_len - LANES:buf_len] = jnp.where(lane < keep, prev_r, new_r)


SAMPLE_STEP_POSITIONS = 2048
SAMPLE_STEP_ROWS = 8


def _attn_sample_specs(cache_shape, bc, bn, g, t_len, nbb, step_of):
    nb, _, _, buf_len = cache_shape
    per_block = LANES // t_len
    assert nb % nbb == 0 and per_block % nbb == 0
    new_spec = pl.BlockSpec((nbb, SUBLANES,GROUP_DIM), lambda *i: (step_of(*i), 0, g))
    newt_spec = pl.BlockSpec((GROUP_DIM, LANES), lambda *i: (g, (step_of(*i) * nbb) // per_block))
    cache_spec = pl.BlockSpec((nbb, 2, GROUP_DIM, buf_len), lambda *i: (step_of(*i), 0, 0, 0))
    in_specs = [new_spec, new_spec, new_spec, newt_spec, newt_spec, cache_spec,
                _const_spec(bc.shape), _const_spec(bn.shape)]
    out_specs = [pl.BlockSpec((nbb, SUBLANES,GROUP_DIM), lambda *i: (step_of(*i), 0, 0)),
                 pl.BlockSpec((nbb, SUBLANES,LANES), lambda *i: (step_of(*i), 0, 0)),
                 cache_spec]
    out_shapes = [jax.ShapeDtypeStruct((nb, SUBLANES,GROUP_DIM), F32),
                  jax.ShapeDtypeStruct((nb, SUBLANES,LANES), F32),
                  jax.ShapeDtypeStruct(cache_shape, F32)]
    return in_specs, out_specs, out_shapes


def _attn_sample(q8, k8, v8, kt, vt, cache, bc, bn, g, t_len):
    nb, _, _, buf_len = cache.shape
    nbb = max(1, min(LANES // t_len, SAMPLE_STEP_ROWS, SAMPLE_STEP_POSITIONS // buf_len))
    in_specs, out_specs, out_shapes = _attn_sample_specs(cache.shape, bc, bn, g, t_len, nbb, lambda b: b)
    return pl.pallas_call(
        functools.partial(_attn_sample_kernel, buf_len=buf_len, t_len=t_len),
        grid=(nb // nbb,),
        in_specs=in_specs, out_specs=out_specs, out_shape=out_shapes,
        compiler_params=_cparams(("arbitrary",)),
        name="attn_sample",
    )(q8, k8, v8, kt, vt, cache, bc, bn)


OUT_PARTS = 2
MERGE_SLABS = N_DIL * (GROUP_DIM + LANES) // LANES


def _load_positions(ref, slab_ref, slab0, r0, rows):
    _, d, _, width = ref.shape
    if d == 1:
        return ref[0, 0, r0:r0 + rows, :].astype(F32)
    for r in range(d):
        for s in range(width // LANES):
            slab_ref[slab0 + s, pl.ds(r, rows // d, stride=d), :] = (
                ref[0, r, r0 // d:(r0 + rows) // d, s * LANES:(s + 1) * LANES].astype(F32))
    return jnp.concatenate([slab_ref[slab0 + s] for s in range(width // LANES)], axis=-1)


N_OUT_INPUTS = 19


def _out_kernel(*refs, hosted):
    (x_ref, ga1_ref, sh2_ref, sc2_ref, ga2_ref, convn_ref, o1_ref, o2_ref, o3_ref, l1_ref, l2_ref, l3_ref,
     gna_ref, wout_ref, g2_ref, wg_ref, wu_ref, wd_ref, fg_ref) = refs[:N_OUT_INPUTS]
    if hosted is None:
        y_ref, slab_ref = refs[N_OUT_INPUTS:]
    else:
        sample_in = refs[N_OUT_INPUTS:N_OUT_INPUTS + N_SAMPLE_INPUTS]
        y_ref, *sample_out, slab_ref = refs[N_OUT_INPUTS + N_SAMPLE_INPUTS:]
    tm = x_ref.shape[1]
    pr = tm // OUT_PARTS
    parts = range(OUT_PARTS)
    hm_f = _head_masks(GROUP_DIM, F32)
    lm_f = _head_masks(LANES, F32)
    gna = gna_ref[...]

    def mod_rows(ref, a):
        return ref[0] if ref.shape[1] == 1 else ref[0, a * pr:(a + 1) * pr, :]

    def merge(a):
        r0 = a * pr
        slab0 = a * MERGE_SLABS
        ls = [_load_positions(l_ref, slab_ref, slab0 + g, r0, pr)
              for g, l_ref in enumerate((l1_ref, l2_ref, l3_ref))]
        m = jnp.maximum(jnp.maximum(ls[0], ls[1]), ls[2])
        es = [jnp.exp(l - m) for l in ls]
        tot = es[0] + es[1] + es[2]
        aos = []
        ssq = jnp.zeros((pr, 1), F32)
        for g, o_ref in enumerate((o1_ref, o2_ref, o3_ref)):
            alpha = es[g] / tot
            wide = jnp.zeros((pr, GROUP_DIM), F32)
            for h in range(GROUP_HEADS):
                a_h = jnp.sum(alpha * lm_f[h], axis=-1, keepdims=True)
                wide = wide + a_h * hm_f[h]
            ao = wide * _load_positions(o_ref, slab_ref, slab0 + N_DIL + g * (GROUP_DIM // LANES), r0, pr)
            ssq = ssq + jnp.sum(ao * ao, axis=-1, keepdims=True)
            aos.append(ao)
        rinv = lax.rsqrt(ssq / ATTN_DIM + EPS)
        return jnp.concatenate(
            [convn_ref[0, r0:r0 + pr, :]]
            + [(aos[g] * rinv * gna[:, g * GROUP_DIM:(g + 1) * GROUP_DIM]).astype(BF16) for g in range(N_DIL)],
            axis=-1)

    def out_proj(a, mixed):
        mix = jnp.dot(mixed, wout_ref[...], preferred_element_type=F32)
        x1 = x_ref[0, a * pr:(a + 1) * pr, :] + mod_rows(ga1_ref, a) * mix
        h2 = (_rms(x1, g2_ref[...]) * (1.0 + mod_rows(sc2_ref, a)) + mod_rows(sh2_ref, a)).astype(BF16)
        return x1, h2

    def swiglu(h2):
        gate = jnp.dot(h2, wg_ref[...], preferred_element_type=F32)
        up = jnp.dot(h2, wu_ref[...], preferred_element_type=F32)
        act = (gate * jax.nn.sigmoid(gate) * up).astype(BF16)
        return jnp.dot(act, wd_ref[...], preferred_element_type=F32)

    if hosted is not None:
        step = pl.program_id(0) * pl.num_programs(1) + pl.program_id(1)
        _attn_sample_rows(step, *sample_in, *sample_out, *hosted)
    mixed = [merge(a) for a in parts]
    x1h2 = [out_proj(a, mixed[a]) for a in parts]
    ffn = [swiglu(h2) for _, h2 in x1h2]
    for a in parts:
        x2 = x1h2[a][0] + mod_rows(ga2_ref, a) * ffn[a]
        y_ref[0, a * pr:(a + 1) * pr, :] = _rms(x2, fg_ref[...])


OUT_TILE = 512


def _out(x, mod, convn, os_, ls_, gn_attn, w_out, g2, w_gate, w_up, w_down, final_g, sample=None):
    nb, R, _ = x.shape
    tm = min(OUT_TILE, R)
    n_i = R // tm
    per_row = mod.shape[1] != 1
    mrows = tm if per_row else 1
    row = lambda b, i: (b, i, 0)

    def mod_spec(c):
        return pl.BlockSpec((1, mrows, D_MODEL), lambda b, i: (b, i if per_row else 0, c))

    def stream_spec(a):
        _, d, _, width = a.shape
        return pl.BlockSpec((1, d, tm // d, width), lambda b, i: (b, 0, i, 0))

    in_specs = ([pl.BlockSpec((1, tm, D_MODEL), row),
                 mod_spec(2), mod_spec(3), mod_spec(4), mod_spec(5),
                 pl.BlockSpec((1, tm, CONV_DIM), row)]
                + [stream_spec(a) for a in os_] + [stream_spec(a) for a in ls_]
                + [_const_spec((1, ATTN_DIM)), _const_spec((D_MODEL, D_MODEL)), _const_spec((1, D_MODEL)),
                   _const_spec((D_MODEL, D_FF)), _const_spec((D_MODEL, D_FF)), _const_spec((D_FF, D_MODEL)),
                   _const_spec((1, D_MODEL))])
    operands = [x, mod, mod, mod, mod, convn, *os_, *ls_, gn_attn, w_out, g2, w_gate, w_up, w_down, final_g]
    assert len(operands) == N_OUT_INPUTS
    out_specs = [pl.BlockSpec((1, tm, D_MODEL), row)]
    out_shapes = [jax.ShapeDtypeStruct(x.shape, F32)]
    hosted, limit = None, VMEM_LIMIT
    if sample is not None:
        q8, k8, v8, kt, vt, cache, bc, bn, g, t_len = sample
        assert cache.shape[0] == nb * n_i
        s_in, s_out, s_shapes = _attn_sample_specs(cache.shape, bc, bn, g, t_len, 1, lambda b, i: b * n_i + i)
        in_specs += s_in
        operands += [q8, k8, v8, kt, vt, cache, bc, bn]
        out_specs += s_out
        out_shapes += s_shapes
        hosted, limit = (cache.shape[-1], t_len), HOST_VMEM_LIMIT
    outs = pl.pallas_call(
        functools.partial(_out_kernel, hosted=hosted),
        grid=(nb, n_i),
        in_specs=in_specs, out_specs=out_specs, out_shape=out_shapes,
        scratch_shapes=[pltpu.VMEM((OUT_PARTS * MERGE_SLABS, tm // OUT_PARTS, LANES), F32)],
        compiler_params=pltpu.CompilerParams(dimension_semantics=("arbitrary", "arbitrary"),
                                             vmem_limit_bytes=limit),
        name="out",
    )(*operands)
    return outs[0] if sample is None else outs


def _buffer_to_channel_major(cache):
    B, L = cache.shape[:2]
    return cache.transpose(0, 2, 3, 4, 1).reshape(B, 2, GROUP_DIM, L)


def _buffer_from_channel_major(buf):
    B, _, _, L = buf.shape
    return buf.reshape(B, 2, GROUP_HEADS, HEAD_DIM, L).transpose(0, 4, 1, 2, 3)[None]


def kernel(x_prompt, x_sample, state_conv, cache_kv1, cache_kv2, cache_kv3, c_prompt, c_sample, w_ada, b_ada, norm1_g, norm2_g, w_in, conv_w, gn_conv, gn_attn, w_out, w_gate, w_up, w_down, rel_bias, final_g):
    B, S, _ = x_prompt.shape
    nb, t_len, _ = x_sample.shape
    rows_s = nb * t_len
    assert w_ada.shape[0] == 1 and LANES % t_len == 0 and 2 <= t_len <= SUBLANES and rows_s % LANES == 0

    w_in_b = w_in[0].astype(BF16)
    w_out_b = w_out[0].astype(BF16)
    w_gate_b = w_gate[0].astype(BF16)
    w_up_b = w_up[0].astype(BF16)
    w_down_b = w_down[0].astype(BF16)
    final_g2 = final_g.reshape(1, D_MODEL)
    tables = [_group_table(rel_bias, g) for g in range(N_DIL)]

    c_all = jnp.concatenate([c_prompt, jnp.repeat(c_sample, t_len, axis=0)], axis=0)
    mod_p, mod_s = _ada(c_all, w_ada[0], b_ada, B)
    mod_p = mod_p.reshape(B, 1, 6 * D_MODEL)

    xs = x_sample.reshape(rows_s, D_MODEL)
    prev0 = jnp.repeat(state_conv[0, :, 0], t_len, axis=0)
    prev1 = jnp.repeat(state_conv[0, :, 1], t_len, axis=0)
    convn_s, u_s, q_s, k_s, v_s, kt_s, vt_s = _inproj_sample(
        xs, mod_s, norm1_g, w_in_b, conv_w[0], gn_conv, prev0, prev1, t_len)
    pad8 = lambda a: jnp.pad(a.reshape(nb, t_len, ATTN_DIM), ((0, 0), (0, SUBLANES - t_len), (0, 0)))
    q8, k8, v8 = pad8(q_s), pad8(k_s), pad8(v_s)
    sample_ops = []
    for g, (cache, (_, d)) in enumerate(zip((cache_kv1, cache_kv2, cache_kv3), DIL_PAIRS)):
        buf = _buffer_to_channel_major(cache[0])
        bc, bn = _sample_bias(tables[g], buf.shape[-1], d, t_len)
        sample_ops.append((q8, k8, v8, kt_s, vt_s, buf, bc, bn, g, t_len))

    host_out = N_DIL - 1 if B * (S // min(OUT_TILE, S)) == nb else None
    inproj_steps = B * (S // INPROJ_TILE)
    host_in = [g for g in range(N_DIL) if g != host_out and nb % inproj_steps == 0
               and (LANES // t_len) % (nb // inproj_steps) == 0]
    sample_res = {}
    (convn, q1, q2, q3, k1, k2, k3, v1, v2, v3, nconv_p, kv1_p, kv2_p, kv3_p, *hosted_res) = _inproj_prompt(
        x_prompt, mod_p, norm1_g, w_in_b, conv_w[0], gn_conv, samples=[sample_ops[g] for g in host_in])
    for j, g in enumerate(host_in):
        sample_res[g] = hosted_res[j * N_SAMPLE_OUTPUTS:(j + 1) * N_SAMPLE_OUTPUTS]
    os_, ls_ = [], []
    for g, (qg, kg, vg) in enumerate(((q1, k1, v1), (q2, k2, v2), (q3, k3, v3))):
        _, d, M, _ = qg.shape
        merge = lambda a: a.reshape(B * d, M, GROUP_DIM)
        o, l = _attn_prompt(merge(qg), merge(kg), merge(vg), *_prompt_bias(tables[g]))
        os_.append(o.reshape(B, d, M, GROUP_DIM))
        ls_.append(l.reshape(B, d, M, LANES))
    outs = _out(x_prompt, mod_p, convn, os_, ls_, gn_attn, w_out_b, norm2_g, w_gate_b, w_up_b, w_down_b,
                final_g2, sample=None if host_out is None else sample_ops[host_out])
    y_prompt = outs
    if host_out is not None:
        y_prompt, *sample_res[host_out] = outs

    os_s, ls_s, new_bufs = [], [], []
    for g in range(N_DIL):
        o, l, nbuf = sample_res[g] if g in sample_res else _attn_sample(*sample_ops[g])
        os_s.append(o[:, :t_len].reshape(1, 1, rows_s, GROUP_DIM))
        ls_s.append(l[:, :t_len].reshape(1, 1, rows_s, LANES))
        new_bufs.append(_buffer_from_channel_major(nbuf))
    y_sample = _out(xs[None], mod_s[None], convn_s[None], os_s, ls_s, gn_attn, w_out_b, norm2_g,
                    w_gate_b, w_up_b, w_down_b, final_g2)
    y_sample = y_sample.reshape(nb, t_len, D_MODEL)
    nconv_s = u_s.reshape(nb, t_len, CONV_DIM)[:, t_len - 2:]

    return (y_prompt, y_sample, nconv_p[None],
            _buffer_from_channel_major(kv1_p), _buffer_from_channel_major(kv2_p),
            _buffer_from_channel_major(kv3_p),
            nconv_s[None], new_bufs[0], new_bufs[1], new_bufs[2])
```
